```python
import math
import jax, jax.numpy as jnp
from jax import lax
import numpy as np

D_MODEL = 2048
BATCH = 32
SEQ = 256
DEPTH = 2
DEC_BATCH = 2
DEC_SEQ = 1024
PAST_LEN = 256

GRID_W = 64
N_BRANCH = 4
MIX_W = 512
ATT_HEADS = 8
ATT_KV_HEADS = 2
HEAD_DIM = 64
ATT_GROUP = ATT_HEADS // ATT_KV_HEADS
Q_BLOCK = 128
ROPE_THETA = 10000.0
FNET_GROUPS = 4
FNET_GW = MIX_W // FNET_GROUPS
SSM_HEADS = 8
SSM_HEAD_DIM = 64
SSM_GROUPS = 2
SSM_STATE = 128
SSM_CHUNK = 128
SSM_CONV = 3
SSM_D_INNER = SSM_HEADS * SSM_HEAD_DIM
SSM_CONV_CH = SSM_D_INNER + 2 * SSM_GROUPS * SSM_STATE
GMLP_GROUPS = 4
GMLP_CHUNK = 128
GMLP_GW = MIX_W // GMLP_GROUPS
D_FF = 4 * D_MODEL
EPS = 1e-6

IN_SPLITS = (ATT_HEADS * HEAD_DIM, ATT_KV_HEADS * HEAD_DIM, ATT_KV_HEADS * HEAD_DIM, MIX_W,
             SSM_CONV_CH, SSM_D_INNER, 2 * SSM_HEADS, 2 * MIX_W, N_BRANCH * D_MODEL)
N_IN = sum(IN_SPLITS)

kernel_name = "hybrid_diffusion_ctx_prefix_step"


def _rms(x):
    xf = x.astype(jnp.float32)
    return (xf * lax.rsqrt(jnp.mean(xf * xf, axis=-1, keepdims=True) + EPS)).astype(x.dtype)


def _rot_half(u, pos):
    half = u.shape[-1] // 2
    freqs = ROPE_THETA ** (-jnp.arange(half, dtype=jnp.float32) / half)
    ang = pos.astype(jnp.float32)[:, None] * freqs[None, :]
    cos = jnp.cos(ang)[None, :, None, :]
    sin = jnp.sin(ang)[None, :, None, :]
    uf = u.astype(jnp.float32)
    u1, u2 = uf[..., :half], uf[..., half:]
    return jnp.concatenate([u1 * cos - u2 * sin, u1 * sin + u2 * cos], axis=-1)


def _rope_2d(t, rows, cols):
    half = t.shape[-1] // 2
    out = jnp.concatenate([_rot_half(t[..., :half], rows), _rot_half(t[..., half:], cols)], axis=-1)
    return out.astype(t.dtype)


def _attend(q, k, v):
    b, L = q.shape[0], q.shape[1]
    nb = L // Q_BLOCK
    qb = jnp.moveaxis(q.reshape(b, nb, Q_BLOCK, ATT_KV_HEADS, ATT_GROUP, HEAD_DIM), 1, 0)
    scale = HEAD_DIM ** -0.5

    def block(qi):
        s = jnp.einsum('bqkgd,bskd->bkgqs', qi, k, preferred_element_type=jnp.float32) * scale
        p = jax.nn.softmax(s, axis=-1).astype(v.dtype)
        return jnp.einsum('bkgqs,bskd->bqkgd', p, v)

    o = lax.map(block, qb)
    return jnp.moveaxis(o, 0, 1).reshape(b, L, ATT_HEADS * HEAD_DIM)


def _fourier(f):
    b, L, _ = f.shape
    fg = f.reshape(b, L, FNET_GROUPS, FNET_GW).astype(jnp.float32)
    out = jnp.fft.fft2(fg, axes=(1, 3), norm="ortho").real
    return out.reshape(b, L, MIX_W).astype(f.dtype)


def _dwconv(u, w, bias):
    C = u.shape[-1]
    out = lax.conv_general_dilated(u, w.astype(u.dtype)[:, None, :], window_strides=(1,),
                                   padding=((SSM_CONV // 2, SSM_CONV // 2),),
                                   dimension_numbers=('NWC', 'WIO', 'NWC'),
                                   feature_group_count=C)
    return out + bias.astype(u.dtype)


def _ssd(x, dt, A, B, C, h0):
    b, L, h, p = x.shape
    n = B.shape[-1]
    nc = L // SSM_CHUNK
    Q = SSM_CHUNK
    x = x.reshape(b, nc, Q, h, p)
    dt = dt.reshape(b, nc, Q, h)
    B = B.reshape(b, nc, Q, h, n)
    C = C.reshape(b, nc, Q, h, n)
    a_cum = jnp.cumsum(dt * A, axis=2)
    mask = jnp.tril(jnp.ones((Q, Q), dtype=bool))[None, None, :, :, None]
    seg = a_cum[:, :, :, None, :] - a_cum[:, :, None, :, :]
    Lmat = jnp.exp(jnp.where(mask, seg, -jnp.inf))
    xdt = x * dt[..., None]
    cb = jnp.einsum('bcihn,bcjhn->bcijh', C, B)
    y_diag = jnp.einsum('bcijh,bcjhp->bcihp', cb * Lmat, xdt)
    decay_to_end = jnp.exp(a_cum[:, :, -1:, :] - a_cum)
    chunk_states = jnp.einsum('bcjhn,bcjh,bcjhp->bchpn', B, decay_to_end, xdt)
    chunk_decay = jnp.exp(a_cum[:, :, -1, :])

    def step(s, inp):
        st, dec = inp
        return s * dec[..., None, None] + st, s

    final, s_in = lax.scan(step, h0, (jnp.moveaxis(chunk_states, 1, 0), jnp.moveaxis(chunk_decay, 1, 0)))
    s_in = jnp.moveaxis(s_in, 0, 1)
    y_off = jnp.einsum('bcihn,bchpn,bcih->bcihp', C, s_in, jnp.exp(a_cum))
    return (y_diag + y_off).reshape(b, L, h, p), final


def _mamba(xbc, z, dt_raw, conv_w, conv_b, a_log, dt_bias, d_skip, norm_g, h0):
    b, L, _ = xbc.shape
    xbc = jax.nn.silu(_dwconv(xbc, conv_w, conv_b))
    xs, bm, cm = jnp.split(xbc, [SSM_D_INNER, SSM_D_INNER + SSM_GROUPS * SSM_STATE], axis=-1)
    hpg = SSM_HEADS // SSM_GROUPS
    xs = xs.reshape(b, L, SSM_HEADS, SSM_HEAD_DIM).astype(jnp.float32)
    bm = jnp.repeat(bm.reshape(b, L, SSM_GROUPS, SSM_STATE), hpg, axis=2).astype(jnp.float32)
    cm = jnp.repeat(cm.reshape(b, L, SSM_GROUPS, SSM_STATE), hpg, axis=2).astype(jnp.float32)
    dt = jax.nn.softplus(dt_raw.reshape(b, L, 2, SSM_HEADS).astype(jnp.float32) + dt_bias.astype(jnp.float32))
    A = -jnp.exp(a_log.astype(jnp.float32))
    h0 = h0.astype(jnp.float32)
    y_f, s_f = _ssd(xs, dt[:, :, 0], A[0], bm, cm, h0[:, 0])
    y_b, s_b = _ssd(jnp.flip(xs, 1), jnp.flip(dt[:, :, 1], 1), A[1], jnp.flip(bm, 1), jnp.flip(cm, 1), h0[:, 1])
    y = y_f + jnp.flip(y_b, 1) + d_skip.astype(jnp.float32)[:, None] * xs
    y = y.reshape(b, L, SSM_D_INNER) * jax.nn.silu(z.astype(jnp.float32))
    y = (_rms(y) * norm_g.astype(jnp.float32)).astype(z.dtype)
    return y, jnp.stack([s_f, s_b], axis=1)


def _gmlp(uv, norm_g, w_s, b_s):
    b, L, _ = uv.shape
    u, v = jnp.split(jax.nn.gelu(uv), 2, axis=-1)
    v = _rms(v) * norm_g
    nc = L // GMLP_CHUNK
    vc = v.reshape(b, nc, GMLP_CHUNK, GMLP_GROUPS, GMLP_GW)
    mixed = jnp.einsum('gij,bcjgd->bcigd', w_s, vc) + b_s.T[:, :, None]
    return u * mixed.reshape(b, L, MIX_W)


def _layer(x, cond, lw, h0, lat):
    (w_mod, b_mod, w_in, q_g, k_g, conv_w, conv_b, a_log, dt_bias, d_skip, ssm_g,
     gmlp_g, w_s, b_s, w_branch, w_out, w_ff1, w_ff2) = lw
    b, L, _ = x.shape
    mod = (jax.nn.silu(cond) @ w_mod + b_mod)[:, None, :]
    sh1, sc1, g1, sh2, sc2, g2 = jnp.split(mod, 6, axis=-1)
    h = _rms(x) * (1 + sc1) + sh1
    proj = h @ w_in
    q, k, v, f_in, xbc, z, dt_raw, uv, gate_logits = jnp.split(
        proj, np.cumsum(IN_SPLITS)[:-1].tolist(), axis=-1)
    q = _rms(q.reshape(b, L, ATT_HEADS, HEAD_DIM)) * q_g
    k = _rms(k.reshape(b, L, ATT_KV_HEADS, HEAD_DIM)) * k_g
    v = v.reshape(b, L, ATT_KV_HEADS, HEAD_DIM)
    if lat is None:
        q_use, keys, vals = q, k, v
    else:
        rows, cols, ctx_k, ctx_v = lat
        q_use = _rope_2d(q, rows, cols)
        keys = jnp.concatenate([ctx_k.astype(k.dtype), _rope_2d(k, rows, cols)], axis=1)
        vals = jnp.concatenate([ctx_v.astype(v.dtype), v], axis=1)
    o_att = _attend(q_use.reshape(b, L, ATT_KV_HEADS, ATT_GROUP, HEAD_DIM), keys, vals)
    o_fnet = _fourier(f_in)
    o_ssm, s_fin = _mamba(xbc, z, dt_raw, conv_w, conv_b, a_log, dt_bias, d_skip, ssm_g, h0)
    o_gmlp = _gmlp(uv, gmlp_g, w_s, b_s)
    br = jnp.stack([o_att, o_fnet, o_ssm.astype(o_att.dtype), o_gmlp], axis=2)
    gates = jax.nn.sigmoid(gate_logits.reshape(b, L, N_BRANCH, D_MODEL))
    merged = jnp.sum(gates * jnp.einsum('blkw,kwd->blkd', br, w_branch), axis=2)
    x = x + g1 * (merged @ w_out)
    h2 = _rms(x) * (1 + sc2) + sh2
    x = x + g2 * (jnp.square(jax.nn.relu(h2 @ w_ff1)) @ w_ff2)
    return x, k, v, s_fin


def setup_inputs(seed: int = 0) -> dict:
    key = jax.random.key(seed)
    ks = jax.random.split(key, 32)
    f32 = jnp.float32

    def nrm(k, shape, scale):
        return jax.random.normal(k, shape, f32) * scale

    dt0 = jnp.exp(jax.random.uniform(ks[14], (DEPTH, 2, SSM_HEADS), f32, math.log(1e-3), math.log(1e-1)))
    return {
        "x_prompt": nrm(ks[0], (BATCH, SEQ, D_MODEL), 1.0),
        "x_sample": nrm(ks[1], (DEC_BATCH, DEC_SEQ, D_MODEL), 1.0),
        "c": nrm(ks[2], (DEC_BATCH, D_MODEL), 1.0),
        "cache_k": nrm(ks[3], (DEC_BATCH, DEPTH, PAST_LEN, ATT_KV_HEADS, HEAD_DIM), 1.0),
        "cache_v": nrm(ks[4], (DEC_BATCH, DEPTH, PAST_LEN, ATT_KV_HEADS, HEAD_DIM), 1.0),
        "state_ssm": nrm(ks[5], (DEC_BATCH, DEPTH, 2, SSM_HEADS, SSM_HEAD_DIM, SSM_STATE), 0.5),
        "c_ctx": nrm(ks[6], (D_MODEL,), 1.0),
        "w_mod": nrm(ks[7], (DEPTH, D_MODEL, 6 * D_MODEL), 0.5 * D_MODEL ** -0.5),
        "b_mod": nrm(ks[8], (DEPTH, 6 * D_MODEL), 0.02),
        "w_in": nrm(ks[9], (DEPTH, D_MODEL, N_IN), D_MODEL ** -0.5),
        "q_norm_g": 1.0 + nrm(ks[10], (DEPTH, HEAD_DIM), 0.02),
        "k_norm_g": 1.0 + nrm(ks[11], (DEPTH, HEAD_DIM), 0.02),
        "conv_w": nrm(ks[12], (DEPTH, SSM_CONV, SSM_CONV_CH), SSM_CONV ** -0.5),
        "conv_b": nrm(ks[13], (DEPTH, SSM_CONV_CH), 0.02),
        "a_log": jnp.log(jax.random.uniform(ks[15], (DEPTH, 2, SSM_HEADS), f32, 1.0, 16.0)),
        "dt_bias": dt0 + jnp.log(-jnp.expm1(-dt0)),
        "d_skip": 1.0 + nrm(ks[16], (DEPTH, SSM_HEADS), 0.02),
        "ssm_norm_g": 1.0 + nrm(ks[17], (DEPTH, SSM_D_INNER), 0.02),
        "gmlp_norm_g": 1.0 + nrm(ks[18], (DEPTH, MIX_W), 0.02),
        "w_spatial": nrm(ks[19], (DEPTH, GMLP_GROUPS, GMLP_CHUNK, GMLP_CHUNK), GMLP_CHUNK ** -0.5),
        "b_spatial": 1.0 + nrm(ks[20], (DEPTH, GMLP_GROUPS, GMLP_CHUNK), 0.02),
        "w_branch": nrm(ks[21], (DEPTH, N_BRANCH, MIX_W, D_MODEL), MIX_W ** -0.5),
        "w_out": nrm(ks[22], (DEPTH, D_MODEL, D_MODEL), D_MODEL ** -0.5),
        "w_ff1": nrm(ks[23], (DEPTH, D_MODEL, D_FF), D_MODEL ** -0.5),
        "w_ff2": nrm(ks[24], (DEPTH, D_FF, D_MODEL), D_FF ** -0.5),
    }


def reference(x_prompt, x_sample, c, cache_k, cache_v, state_ssm, c_ctx, w_mod, b_mod, w_in,
              q_norm_g, k_norm_g, conv_w, conv_b, a_log, dt_bias, d_skip, ssm_norm_g,
              gmlp_norm_g, w_spatial, b_spatial, w_branch, w_out, w_ff1, w_ff2):
    n_rows = x_sample.shape[1] // GRID_W
    rows = jnp.repeat(jnp.arange(n_rows), GRID_W)
    cols = jnp.tile(jnp.arange(GRID_W), n_rows)
    cond_ctx = c_ctx[None, :]
    h0_ctx = jnp.zeros((x_prompt.shape[0], 2, SSM_HEADS, SSM_HEAD_DIM, SSM_STATE), jnp.float32)
    yp, ys = x_prompt, x_sample
    new_k, new_v, new_s = [], [], []
    for l in range(DEPTH):
        lw = (w_mod[l], b_mod[l], w_in[l], q_norm_g[l], k_norm_g[l], conv_w[l], conv_b[l],
              a_log[l], dt_bias[l], d_skip[l], ssm_norm_g[l], gmlp_norm_g[l], w_spatial[l],
              b_spatial[l], w_branch[l], w_out[l], w_ff1[l], w_ff2[l])
        yp, k_l, v_l, s_l = _layer(yp, cond_ctx, lw, h0_ctx, None)
        new_k.append(k_l)
        new_v.append(v_l)
        new_s.append(s_l)
        ys, _, _, _ = _layer(ys, c, lw, state_ssm[:, l], (rows, cols, cache_k[:, l], cache_v[:, l]))
    return (yp, ys, jnp.stack(new_k, axis=1), jnp.stack(new_v, axis=1), jnp.stack(new_s, axis=1))
```

```python
import functools
import math

import numpy as np
import jax
import jax.numpy as jnp
from jax import lax
from jax.experimental import pallas as pl
from jax.experimental.pallas import tpu as pltpu

F32 = jnp.float32
BF16 = jnp.bfloat16

D_MODEL = 2048
DEPTH = 2
GRID_W = 64
MIX_W = 512
N_BRANCH = 4
ATT_HEADS = 8
ATT_KV_HEADS = 2
HEAD_DIM = 64
ROPE_THETA = 10000.0
FNET_GW = 128
SSM_HEADS = 8
SSM_HEAD_DIM = 64
SSM_STATE = 128
SSM_CHUNK = 128
SSM_D_INNER = 512
SSM_CONV_CH = 1024
GMLP_CHUNK = 128
D_FF = 4 * D_MODEL
EPS = 1e-6
LAT_LEN = 1024

QKV_W = 768
DT_PAD = 128
W_A_COLS = 768 + 512 + 1024 + 512 + DT_PAD + 1024
COL_F = 768
COL_XBC = 1280
COL_Z = 2304
COL_DT = 2816
COL_UV = 2944
GATE_COL0 = 3856

VMEM_LIMIT = 56 * 1024 * 1024
NEG_BIG = -1e30


def _params(*sem):
    return pltpu.CompilerParams(dimension_semantics=sem, vmem_limit_bytes=VMEM_LIMIT)


def _dot(a, b):
    return jnp.dot(a, b, preferred_element_type=F32)


def _dot_nt(a, b):
    return lax.dot_general(a, b, (((1,), (1,)), ((), ())), preferred_element_type=F32)


def _dot_tn(a, b):
    return lax.dot_general(a, b, (((0,), (0,)), ((), ())), preferred_element_type=F32)


def _sigmoid(x):
    return 1.0 / (1.0 + jnp.exp(-x))


def _split2(x):
    hi = x.astype(BF16)
    lo = (x - hi.astype(F32)).astype(BF16)
    return hi, lo


def _rms_mod(x, sc, sh):
    ms = jnp.mean(x * x, axis=-1, keepdims=True)
    return x * lax.rsqrt(ms + EPS) * (1.0 + sc) + sh


def _mod_kernel(cond_ref, w_ref, b_ref, o_ref):
    c = cond_ref[...]
    s = c * _sigmoid(c)
    o_ref[...] = _dot(s.astype(BF16), w_ref[...].astype(BF16)) + b_ref[...]


def _modulation(cond8, w_mod, b_mod):
    tn = 1024
    n6 = 6 * D_MODEL
    return pl.pallas_call(
        _mod_kernel,
        grid=(DEPTH, n6 // tn),
        in_specs=[
            pl.BlockSpec((8, D_MODEL), lambda l, n: (0, 0)),
            pl.BlockSpec((None, D_MODEL, tn), lambda l, n: (l, 0, n)),
            pl.BlockSpec((None, 1, tn), lambda l, n: (l, 0, n)),
        ],
        out_specs=pl.BlockSpec((None, 8, tn), lambda l, n: (l, 0, n)),
        out_shape=jax.ShapeDtypeStruct((DEPTH, 8, n6), F32),
        compiler_params=_params("arbitrary", "arbitrary"),
        name="modulation",
    )(cond8, w_mod, b_mod.reshape(DEPTH, 1, n6))


def _mod_spec(chunk, row_fn):
    return pl.BlockSpec((None, 1, D_MODEL), lambda m, *_: (row_fn(m), 0, chunk))


_IN_PIECES = ((0, QKV_W), (COL_F, COL_XBC), (COL_XBC, COL_Z), (COL_Z, COL_DT), (COL_DT, COL_UV),
              (COL_UV, W_A_COLS))


def _inproj_kernel(x_ref, sc_ref, sh_ref, w_ref, *out_refs):
    hb = _rms_mod(x_ref[...], sc_ref[...], sh_ref[...]).astype(BF16)
    for o_ref, (a, b) in zip(out_refs, _IN_PIECES):
        o_ref[...] = _dot(hb, w_ref[:, a:b])


def _inproj(x, mod3, w_a, row_fn):
    t = x.shape[0]
    tm = 256
    widths = [b - a for a, b in _IN_PIECES]
    return pl.pallas_call(
        _inproj_kernel,
        grid=(t // tm,),
        in_specs=[
            pl.BlockSpec((tm, D_MODEL), lambda m: (m, 0)),
            _mod_spec(1, lambda m: row_fn(m * tm)),
            _mod_spec(0, lambda m: row_fn(m * tm)),
            pl.BlockSpec((D_MODEL, W_A_COLS), lambda m: (0, 0), pipeline_mode=pl.Buffered(1)),
        ],
        out_specs=[pl.BlockSpec((tm, w), lambda m: (m, 0)) for w in widths],
        out_shape=[jax.ShapeDtypeStruct((t, w), F32) for w in widths],
        compiler_params=_params("arbitrary"),
        name="inproj",
    )(x, mod3, mod3, w_a)


def _head_rms(t, gsum, gain):
    t2 = t * t
    hi, lo = _split2(t2)
    ss = _dot(hi, gsum) + _dot(lo, gsum)
    return t * lax.rsqrt(ss * (1.0 / HEAD_DIM) + EPS) * gain


def _rope(t, cos, sin_signed):
    w = t.shape[-1]
    up = pltpu.roll(t, w - 16, 1)
    dn = pltpu.roll(t, 16, 1)
    lane = lax.broadcasted_iota(jnp.int32, t.shape, 1)
    partner = jnp.where((lane & 31) < 16, up, dn)
    return t * cos + partner * sin_signed


def _attend_heads(q, k_scr, v_scr, o_ref):
    qb = (q * (HEAD_DIM ** -0.5)).astype(BF16)
    group = ATT_HEADS // ATT_KV_HEADS
    for h in range(ATT_HEADS):
        j = h // group
        s = _dot_nt(qb[:, h * HEAD_DIM:(h + 1) * HEAD_DIM], k_scr[:, j * HEAD_DIM:(j + 1) * HEAD_DIM])
        m = jnp.max(s, axis=-1, keepdims=True)
        e = jnp.exp(s - m)
        den = jnp.sum(e, axis=-1, keepdims=True)
        o = _dot(e.astype(BF16), v_scr[:, j * HEAD_DIM:(j + 1) * HEAD_DIM]) / den
        o_ref[:, h * HEAD_DIM:(h + 1) * HEAD_DIM] = o.astype(o_ref.dtype)


def _attn_ctx_kernel(q_ref, kv_ref, qg_ref, kg_ref, gsum_ref, o_ref, kout_ref, vout_ref, k_scr, v_scr):
    gsum = gsum_ref[...]
    kv = kv_ref[...]
    k = _head_rms(kv[:, :128], gsum[:128, :128], kg_ref[...])
    v = kv[:, 128:]
    kout_ref[...] = k
    vout_ref[...] = v
    k_scr[...] = k.astype(BF16)
    v_scr[...] = v.astype(BF16)
    q = _head_rms(q_ref[...], gsum, qg_ref[...])
    _attend_heads(q, k_scr, v_scr, o_ref)


def _attn_lat_kernel(q_ref, kv_ref, qg_ref, kg_ref, gsum_ref, cq_ref, sq_ref, ck_ref, sk_ref,
                     cache_k_ref, cache_v_ref, o_ref, k_scr, v_scr):
    gsum = gsum_ref[...]
    past = cache_k_ref.shape[0]

    @pl.when(pl.program_id(1) == 0)
    def _():
        kv = kv_ref[...]
        k = _head_rms(kv[:, :128], gsum[:128, :128], kg_ref[...])
        k = _rope(k, ck_ref[...], sk_ref[...])
        k_scr[:past, :] = cache_k_ref[...].astype(BF16)
        v_scr[:past, :] = cache_v_ref[...].astype(BF16)
        k_scr[past:, :] = k.astype(BF16)
        v_scr[past:, :] = kv[:, 128:].astype(BF16)

    q = _head_rms(q_ref[...], gsum, qg_ref[...])
    q = _rope(q, cq_ref[...], sq_ref[...])
    _attend_heads(q, k_scr, v_scr, o_ref)


def _attention_ctx(qkv, qg, kg, gsum, nb, seq):
    t = qkv.shape[0]
    const = lambda shape: pl.BlockSpec(shape, lambda b: (0, 0))
    return pl.pallas_call(
        _attn_ctx_kernel,
        grid=(nb,),
        in_specs=[
            pl.BlockSpec((seq, 512), lambda b: (b, 0)),
            pl.BlockSpec((seq, 256), lambda b: (b, 2)),
            const((1, 512)), const((1, 128)), const((512, 512)),
        ],
        out_specs=[
            pl.BlockSpec((seq, 512), lambda b: (b, 0)),
            pl.BlockSpec((seq, 128), lambda b: (b, 0)),
            pl.BlockSpec((seq, 128), lambda b: (b, 0)),
        ],
        out_shape=[
            jax.ShapeDtypeStruct((t, 512), BF16),
            jax.ShapeDtypeStruct((t, 128), F32),
            jax.ShapeDtypeStruct((t, 128), F32),
        ],
        scratch_shapes=[pltpu.VMEM((seq, 128), BF16), pltpu.VMEM((seq, 128), BF16)],
        compiler_params=_params("arbitrary"),
        name="attn_ctx",
    )(qkv, qkv, qg, kg, gsum)


def _attention_lat(qkv, qg, kg, gsum, rope_tabs, cache_k, cache_v, nb, seq):
    t = qkv.shape[0]
    qblk = 256
    nq = seq // qblk
    past = cache_k.shape[1]
    cq, sq, ck, sk = rope_tabs
    const = lambda shape: pl.BlockSpec(shape, lambda b, i: (0, 0))
    return pl.pallas_call(
        _attn_lat_kernel,
        grid=(nb, nq),
        in_specs=[
            pl.BlockSpec((qblk, 512), lambda b, i: (b * nq + i, 0)),
            pl.BlockSpec((seq, 256), lambda b, i: (b, 2)),
            const((1, 512)), const((1, 128)), const((512, 512)),
            pl.BlockSpec((qblk, 512), lambda b, i: (i, 0)),
            pl.BlockSpec((qblk, 512), lambda b, i: (i, 0)),
            const((seq, 128)), const((seq, 128)),
            pl.BlockSpec((None, past, 128), lambda b, i: (b, 0, 0)),
            pl.BlockSpec((None, past, 128), lambda b, i: (b, 0, 0)),
        ],
        out_specs=pl.BlockSpec((qblk, 512), lambda b, i: (b * nq + i, 0)),
        out_shape=jax.ShapeDtypeStruct((t, 512), BF16),
        scratch_shapes=[pltpu.VMEM((past + seq, 128), BF16), pltpu.VMEM((past + seq, 128), BF16)],
        compiler_params=_params("arbitrary", "arbitrary"),
        name="attn_lat",
    )(qkv, qkv, qg, kg, gsum, cq, sq, ck, sk, cache_k, cache_v)


def _fourier_kernel(f_ref, bdh_ref, bdl_ref, clh_ref, cll_ref, slh_ref, sll_ref, o_ref, *, scale):
    xh, xl = _split2(f_ref[...])
    bdh = bdh_ref[...]
    y = _dot(xh, bdh) + _dot(xl, bdh) + _dot(xh, bdl_ref[...])
    ych, ycl = _split2(y[:, :MIX_W])
    ysh, ysl = _split2(y[:, MIX_W:])
    clh = clh_ref[...]
    slh = slh_ref[...]
    re = _dot(clh, ych) + _dot(cll_ref[...], ych) + _dot(clh, ycl)
    im = _dot(slh, ysh) + _dot(sll_ref[...], ysh) + _dot(slh, ysl)
    o_ref[...] = ((re - im) * scale).astype(o_ref.dtype)


def _dft_consts(seq):
    def cs(n):
        k = np.arange(n, dtype=np.int64)
        ang = 2.0 * np.pi * ((k[:, None] * k[None, :]) % n).astype(np.float64) / n
        return np.cos(ang), np.sin(ang)

    cc, sc = cs(FNET_GW)
    groups = MIX_W // FNET_GW
    bd = np.zeros((MIX_W, 2 * MIX_W), np.float64)
    for g in range(groups):
        sl = slice(g * FNET_GW, (g + 1) * FNET_GW)
        bd[sl, sl] = cc
        bd[sl, MIX_W + g * FNET_GW:MIX_W + (g + 1) * FNET_GW] = sc
    cl, sl_ = cs(seq)

    def hl(a):
        a32 = jnp.asarray(a, F32)
        hi = a32.astype(BF16)
        return hi, (a32 - hi.astype(F32)).astype(BF16)

    return hl(bd) + hl(cl) + hl(sl_)


def _fourier(f, nb, seq):
    t = f.shape[0]
    consts = _dft_consts(seq)
    const = lambda a: pl.BlockSpec(a.shape, lambda b: (0, 0))
    return pl.pallas_call(
        functools.partial(_fourier_kernel, scale=1.0 / math.sqrt(seq * FNET_GW)),
        grid=(nb,),
        in_specs=[pl.BlockSpec((seq, MIX_W), lambda b: (b, 0))] + [const(a) for a in consts],
        out_specs=pl.BlockSpec((seq, MIX_W), lambda b: (b, 0)),
        out_shape=jax.ShapeDtypeStruct((t, MIX_W), BF16),
        compiler_params=_params("arbitrary"),
        name="fourier",
    )(f, *consts)


def _ssd_kernel(*refs, seq, has_h0, emit_state):
    (xbc_ref, z_ref, dt_ref, cw_ref, cb_ref, bias_ref, alog_ref, dskip_ref, ng_ref, tri_ref) = refs[:10]
    rest = list(refs[10:])
    h0_ref = rest.pop(0) if has_h0 else None
    o_ref = rest.pop(0)
    sfin_ref = rest.pop(0) if emit_state else None
    xact, tab, y_scr, s_scr = rest
    nc = seq // SSM_CHUNK
    q = SSM_CHUNK

    lane = lax.broadcasted_iota(jnp.int32, (q, 128), 1)
    ii = lax.broadcasted_iota(jnp.int32, (q, q), 0)
    jj = lax.broadcasted_iota(jnp.int32, (q, q), 1)
    lower = ii >= jj
    upper = ii <= jj
    row = lax.broadcasted_iota(jnp.int32, (q, 1), 0)
    cw = cw_ref[...]
    amul = jnp.where(lane < 16, 1.0, jnp.where(lane < 48, -jnp.exp(alog_ref[...]), 0.0))
    tri = tri_ref[...]

    if has_h0:
        s_scr[...] = h0_ref[...]
    else:
        s_scr[...] = jnp.zeros_like(s_scr)

    def prep(c, carry):
        r0 = pl.multiple_of(c * q, q)
        cur = xbc_ref[pl.ds(r0, q), :]
        prev8 = xbc_ref[pl.ds(pl.multiple_of(jnp.maximum(r0 - 8, 0), 8), 8), :]
        next8 = xbc_ref[pl.ds(pl.multiple_of(jnp.minimum(r0 + q, seq - 8), 8), 8), :]
        prev_row = jnp.where(c > 0, prev8[7:8, :], 0.0)
        next_row = jnp.where(c < nc - 1, next8[0:1, :], 0.0)
        up = jnp.where(row == 0, prev_row, pltpu.roll(cur, 1, 0))
        dn = jnp.where(row == q - 1, next_row, pltpu.roll(cur, q - 1, 0))
        conv = up * cw[0:1, :] + cur * cw[1:2, :] + dn * cw[2:3, :] + cb_ref[...]
        xact[pl.ds(r0, q), :] = conv * _sigmoid(conv)

        v = dt_ref[pl.ds(r0, q), :] + bias_ref[...]
        sp = jnp.maximum(v, 0.0) + jnp.log1p(jnp.exp(-jnp.abs(v)))
        t = sp * amul
        hi = t.astype(BF16)
        r1 = t - hi.astype(F32)
        mid = r1.astype(BF16)
        lo = (r1 - mid.astype(F32)).astype(BF16)
        cum = _dot(tri, hi) + _dot(tri, mid) + _dot(tri, lo)
        tab[pl.ds(r0, q), :] = jnp.where(lane < 32, t, cum)
        return carry

    lax.fori_loop(0, nc, prep, 0)

    def chunk_operands(r0):
        pc = tab[pl.ds(r0, q), :]
        xa = xact[pl.ds(r0, q), :]
        return pc, xa

    def pass_fwd(c, carry):
        r0 = pl.multiple_of(c * q, q)
        pc, xa = chunk_operands(r0)
        pt = pc.T
        for g in range(2):
            bg = xa[:, 512 + 128 * g:640 + 128 * g].astype(BF16)
            cg = xa[:, 768 + 128 * g:896 + 128 * g].astype(BF16)
            cb = _dot_nt(cg, bg)
            s_in = s_scr[0, 256 * g:256 * (g + 1), :]
            y_in = _dot_nt(cg, s_in.astype(BF16))
            for hh in range(4):
                h = 4 * g + hh
                acf_col = pc[:, 32 + h:33 + h]
                acf_row = pt[32 + h:33 + h, :]
                eb_col = pc[:, 40 + h:41 + h] - pc[:, 24 + h:25 + h]
                eb_row = pt[40 + h:41 + h, :] - pt[24 + h:25 + h, :]
                lf = jnp.exp(jnp.where(lower, acf_col - acf_row, NEG_BIG))
                lb = jnp.exp(jnp.where(upper, eb_row - eb_col, NEG_BIG))
                mp = cb * (lf * pt[h:h + 1, :] + lb * pt[8 + h:9 + h, :])
                xh = xa[:, 64 * h:64 * (h + 1)]
                yd = _dot(mp.astype(BF16), xh.astype(BF16))
                yo = y_in[:, 64 * hh:64 * (hh + 1)] * jnp.exp(acf_col)
                y_scr[pl.ds(r0, q), 64 * h:64 * (h + 1)] = yd + yo + dskip_ref[:, 64 * h:64 * (h + 1)] * xh
                tot = pc[q - 1:q, 32 + h:33 + h]
                wf = pc[:, h:h + 1] * jnp.exp(tot - acf_col)
                st = _dot_tn((xh * wf).astype(BF16), bg)
                s_scr[0, 64 * h:64 * (h + 1), :] = s_scr[0, 64 * h:64 * (h + 1), :] * jnp.exp(tot) + st
        return carry

    lax.fori_loop(0, nc, pass_fwd, 0)

    def pass_bwd(ci, carry):
        c = nc - 1 - ci
        r0 = pl.multiple_of(c * q, q)
        pc, xa = chunk_operands(r0)
        for g in range(2):
            bg = xa[:, 512 + 128 * g:640 + 128 * g].astype(BF16)
            cg = xa[:, 768 + 128 * g:896 + 128 * g].astype(BF16)
            s_in = s_scr[1, 256 * g:256 * (g + 1), :]
            y_in = _dot_nt(cg, s_in.astype(BF16))
            for hh in range(4):
                h = 4 * g + hh
                eb_col = pc[:, 40 + h:41 + h] - pc[:, 24 + h:25 + h]
                tot = pc[q - 1:q, 40 + h:41 + h]
                xh = xa[:, 64 * h:64 * (h + 1)]
                y_scr[pl.ds(r0, q), 64 * h:64 * (h + 1)] += y_in[:, 64 * hh:64 * (hh + 1)] * jnp.exp(tot - eb_col)
                wb = pc[:, 8 + h:9 + h] * jnp.exp(eb_col)
                st = _dot_tn((xh * wb).astype(BF16), bg)
                s_scr[1, 64 * h:64 * (h + 1), :] = s_scr[1, 64 * h:64 * (h + 1), :] * jnp.exp(tot) + st
        zc = z_ref[pl.ds(r0, q), :]
        yg = y_scr[pl.ds(r0, q), :] * (zc * _sigmoid(zc))
        ms = jnp.mean(yg * yg, axis=-1, keepdims=True)
        o_ref[pl.ds(r0, q), :] = (yg * lax.rsqrt(ms + EPS) * ng_ref[...]).astype(o_ref.dtype)
        return carry

    lax.fori_loop(0, nc, pass_bwd, 0)

    if emit_state:
        sfin_ref[...] = s_scr[...]


def _ssd(xbc, z, dt, conv_w, conv_b, bias128, alog128, dskip, norm_g, tri, h0, nb, seq, emit_state):
    t = xbc.shape[0]
    has_h0 = h0 is not None
    const = lambda shape: pl.BlockSpec(shape, lambda b: (0,) * len(shape))
    in_specs = [
        pl.BlockSpec((seq, SSM_CONV_CH), lambda b: (b, 0)),
        pl.BlockSpec((seq, SSM_D_INNER), lambda b: (b, 0)),
        pl.BlockSpec((seq, DT_PAD), lambda b: (b, 0)),
        const((3, SSM_CONV_CH)), const((1, SSM_CONV_CH)), const((1, 128)), const((1, 128)),
        const((1, SSM_D_INNER)), const((1, SSM_D_INNER)), const((128, 128)),
    ]
    args = [xbc, z, dt, conv_w, conv_b, bias128, alog128, dskip, norm_g, tri]
    if has_h0:
        in_specs.append(pl.BlockSpec((None, 2, 512, 128), lambda b: (b, 0, 0, 0)))
        args.append(h0)
    out_specs = [pl.BlockSpec((seq, SSM_D_INNER), lambda b: (b, 0))]
    out_shape = [jax.ShapeDtypeStruct((t, SSM_D_INNER), BF16)]
    if emit_state:
        out_specs.append(pl.BlockSpec((None, 2, 512, 128), lambda b: (b, 0, 0, 0)))
        out_shape.append(jax.ShapeDtypeStruct((nb, 2, 512, 128), F32))
    return pl.pallas_call(
        functools.partial(_ssd_kernel, seq=seq, has_h0=has_h0, emit_state=emit_state),
        grid=(nb,),
        in_specs=in_specs,
        out_specs=out_specs,
        out_shape=out_shape,
        scratch_shapes=[
            pltpu.VMEM((seq, SSM_CONV_CH), F32),
            pltpu.VMEM((seq, 128), F32),
            pltpu.VMEM((seq, SSM_D_INNER), F32),
            pltpu.VMEM((2, 512, 128), F32),
        ],
        compiler_params=_params("arbitrary"),
        name="ssd",
    )(*args)


def _gmlp_kernel(uv_ref, g_ref, ws_ref, bs_ref, o_ref):
    rows = uv_ref.shape[0]
    c0 = math.sqrt(2.0 / math.pi)
    for c in range(rows // GMLP_CHUNK):
        sl = slice(c * GMLP_CHUNK, (c + 1) * GMLP_CHUNK)
        x = uv_ref[sl, :]
        ge = x * (0.5 * (1.0 + jnp.tanh(c0 * (x + 0.044715 * (x * x * x)))))
        u = ge[:, :MIX_W]
        v = ge[:, MIX_W:]
        v = v * lax.rsqrt(jnp.mean(v * v, axis=-1, keepdims=True) + EPS) * g_ref[...]
        for g in range(4):
            gl = slice(g * 128, (g + 1) * 128)
            mixed = _dot(ws_ref[g], v[:, gl].astype(BF16)) + bs_ref[:, g:g + 1]
            o_ref[sl, gl] = (u[:, gl] * mixed).astype(o_ref.dtype)


def _gmlp(uv, norm_g, ws, bs_t):
    t = uv.shape[0]
    tm = 256
    return pl.pallas_call(
        _gmlp_kernel,
        grid=(t // tm,),
        in_specs=[
            pl.BlockSpec((tm, 2 * MIX_W), lambda m: (m, 0)),
            pl.BlockSpec((1, MIX_W), lambda m: (0, 0)),
            pl.BlockSpec((4, 128, 128), lambda m: (0, 0, 0)),
            pl.BlockSpec((128, 4), lambda m: (0, 0)),
        ],
        out_specs=pl.BlockSpec((tm, MIX_W), lambda m: (m, 0)),
        out_shape=jax.ShapeDtypeStruct((t, MIX_W), BF16),
        compiler_params=_params("arbitrary"),
        name="gmlp",
    )(uv, norm_g, ws, bs_t)


def _merge_kernel(x_ref, sc_ref, sh_ref, g_ref, b0, b1, b2, b3, wg0, wg1, wg2, wg3, wb_ref, wo_ref,
                  o_ref, h_scr, acc_scr):
    n = pl.program_id(1)

    @pl.when(n == 0)
    def _():
        h_scr[...] = _rms_mod(x_ref[...], sc_ref[...], sh_ref[...]).astype(BF16)
        acc_scr[...] = jnp.zeros_like(acc_scr)

    h = h_scr[...]
    merged = None
    for k, (b_ref, wg_ref) in enumerate(((b0, wg0), (b1, wg1), (b2, wg2), (b3, wg3))):
        gate = _sigmoid(_dot(h, wg_ref[...]))
        term = gate * _dot(b_ref[...], wb_ref[k])
        merged = term if merged is None else merged + term
    acc_scr[...] += _dot(merged.astype(BF16), wo_ref[...])

    @pl.when(n == pl.num_programs(1) - 1)
    def _():
        o_ref[...] = x_ref[...] + g_ref[...] * acc_scr[...]


def _merge(x, mod3, branches, w_gate, w_branch, w_out, row_fn):
    t = x.shape[0]
    tm, tn = 512, 256
    nn = D_MODEL // tn
    rf = lambda m: row_fn(m * tm)
    gate_spec = lambda k: pl.BlockSpec((D_MODEL, tn), lambda m, n: (0, k * nn + n))
    return pl.pallas_call(
        _merge_kernel,
        grid=(t // tm, nn),
        in_specs=[
            pl.BlockSpec((tm, D_MODEL), lambda m, n: (m, 0)),
            _mod_spec(1, rf), _mod_spec(0, rf), _mod_spec(2, rf),
        ] + [pl.BlockSpec((tm, MIX_W), lambda m, n: (m, 0))] * 4
          + [gate_spec(k) for k in range(4)] + [
            pl.BlockSpec((N_BRANCH, MIX_W, tn), lambda m, n: (0, 0, n)),
            pl.BlockSpec((tn, D_MODEL), lambda m, n: (n, 0)),
        ],
        out_specs=pl.BlockSpec((tm, D_MODEL), lambda m, n: (m, 0)),
        out_shape=jax.ShapeDtypeStruct((t, D_MODEL), F32),
        scratch_shapes=[pltpu.VMEM((tm, D_MODEL), BF16), pltpu.VMEM((tm, D_MODEL), F32)],
        compiler_params=_params("arbitrary", "arbitrary"),
        name="merge",
    )(x, mod3, mod3, mod3, *branches, w_gate, w_gate, w_gate, w_gate, w_branch, w_out)


def _ffn_kernel(x_ref, sc_ref, sh_ref, g_ref, w1_ref, w2_ref, o_ref, h_scr, acc_scr):
    n = pl.program_id(1)

    @pl.when(n == 0)
    def _():
        h_scr[...] = _rms_mod(x_ref[...], sc_ref[...], sh_ref[...]).astype(BF16)
        acc_scr[...] = jnp.zeros_like(acc_scr)

    a = jnp.maximum(_dot(h_scr[...], w1_ref[...]), 0.0)
    acc_scr[...] += _dot((a * a).astype(BF16), w2_ref[...])

    @pl.when(n == pl.num_programs(1) - 1)
    def _():
        o_ref[...] = x_ref[...] + g_ref[...] * acc_scr[...]


def _ffn(x, mod3, w1, w2, row_fn):
    t = x.shape[0]
    tm, tf = 512, 512
    rf = lambda m: row_fn(m * tm)
    return pl.pallas_call(
        _ffn_kernel,
        grid=(t // tm, D_FF // tf),
        in_specs=[
            pl.BlockSpec((tm, D_MODEL), lambda m, n: (m, 0)),
            _mod_spec(4, rf), _mod_spec(3, rf), _mod_spec(5, rf),
            pl.BlockSpec((D_MODEL, tf), lambda m, n: (0, n)),
            pl.BlockSpec((tf, D_MODEL), lambda m, n: (n, 0)),
        ],
        out_specs=pl.BlockSpec((tm, D_MODEL), lambda m, n: (m, 0)),
        out_shape=jax.ShapeDtypeStruct((t, D_MODEL), F32),
        scratch_shapes=[pltpu.VMEM((tm, D_MODEL), BF16), pltpu.VMEM((tm, D_MODEL), F32)],
        compiler_params=_params("arbitrary", "arbitrary"),
        name="ffn",
    )(x, mod3, mod3, mod3, w1, w2)


def _rope_tables(seq):
    half = 16
    freqs = ROPE_THETA ** (-jnp.arange(half, dtype=F32) / half)
    n_rows = seq // GRID_W
    rows = jnp.repeat(jnp.arange(n_rows), GRID_W).astype(F32)
    cols = jnp.tile(jnp.arange(GRID_W), n_rows).astype(F32)
    ar = rows[:, None] * freqs[None, :]
    ac = cols[:, None] * freqs[None, :]
    cos64 = jnp.concatenate([jnp.cos(ar), jnp.cos(ar), jnp.cos(ac), jnp.cos(ac)], axis=-1)
    sin64 = jnp.concatenate([-jnp.sin(ar), jnp.sin(ar), -jnp.sin(ac), jnp.sin(ac)], axis=-1)
    return (jnp.tile(cos64, (1, ATT_HEADS)), jnp.tile(sin64, (1, ATT_HEADS)),
            jnp.tile(cos64, (1, ATT_KV_HEADS)), jnp.tile(sin64, (1, ATT_KV_HEADS)))


def _layer_path(x, mod3, lw, row_fn, nb, seq, lat):
    (w_a, w_gate, qg, kg, gsum, conv_w, conv_b, bias128, alog128, dskip, ssm_g, tri, gmlp_g, ws, bs_t,
     w_branch, w_out, w_ff1, w_ff2) = lw
    qkv, f_in, xbc, z, dt, uv = _inproj(x, mod3, w_a, row_fn)
    if lat is None:
        o_att, k_new, v_new = _attention_ctx(qkv, qg, kg, gsum, nb, seq)
        h0 = None
    else:
        rope_tabs, cache_k, cache_v, h0 = lat
        o_att = _attention_lat(qkv, qg, kg, gsum, rope_tabs, cache_k, cache_v, nb, seq)
        k_new = v_new = None
    o_fnet = _fourier(f_in, nb, seq)
    ssd_out = _ssd(xbc, z, dt, conv_w, conv_b, bias128, alog128, dskip, ssm_g, tri, h0, nb, seq,
                   emit_state=lat is None)
    o_ssm = ssd_out[0]
    s_fin = ssd_out[1] if lat is None else None
    o_gmlp = _gmlp(uv, gmlp_g, ws, bs_t)
    x = _merge(x, mod3, (o_att, o_fnet, o_ssm, o_gmlp), w_gate, w_branch, w_out, row_fn)
    x = _ffn(x, mod3, w_ff1, w_ff2, row_fn)
    return x, k_new, v_new, s_fin


def kernel(x_prompt, x_sample, c, cache_k, cache_v, state_ssm, c_ctx, w_mod, b_mod, w_in, q_norm_g, k_norm_g,
           conv_w, conv_b, a_log, dt_bias, d_skip, ssm_norm_g, gmlp_norm_g, w_spatial, b_spatial, w_branch,
           w_out, w_ff1, w_ff2):
    nb_ctx, seq_ctx, _ = x_prompt.shape
    nb_lat, seq_lat, _ = x_sample.shape
    assert seq_lat == LAT_LEN and 1 + nb_lat <= 8

    cond8 = jnp.zeros((8, D_MODEL), F32).at[0].set(c_ctx).at[1:1 + nb_lat].set(c)
    mod = _modulation(cond8, w_mod, b_mod)

    gsum = jnp.asarray(np.kron(np.eye(ATT_HEADS), np.ones((HEAD_DIM, HEAD_DIM))), BF16)
    tri = jnp.asarray(np.tril(np.ones((SSM_CHUNK, SSM_CHUNK))), BF16)
    rope_tabs = _rope_tables(seq_lat)

    xp = x_prompt.reshape(nb_ctx * seq_ctx, D_MODEL)
    xs = x_sample.reshape(nb_lat * seq_lat, D_MODEL)
    ctx_row = lambda tok: 0
    lat_row = lambda tok: 1 + tok // LAT_LEN

    new_k, new_v, new_s = [], [], []
    for l in range(DEPTH):
        wl = w_in[l]
        w_a = jnp.concatenate(
            [wl[:, :COL_DT], jnp.tile(wl[:, COL_DT:COL_DT + 16], (1, 3)),
             jnp.zeros((D_MODEL, DT_PAD - 48), F32), wl[:, COL_DT + 16:GATE_COL0]], axis=1).astype(BF16)
        w_gate = wl[:, GATE_COL0:].astype(BF16)
        zeros80 = jnp.zeros((DT_PAD - 48,), F32)
        bias128 = jnp.concatenate([jnp.tile(dt_bias[l].reshape(16), 3), zeros80]).reshape(1, 128)
        alog128 = jnp.concatenate([jnp.zeros((16,), F32), jnp.tile(a_log[l].reshape(16), 2), zeros80]).reshape(1, 128)
        lw = (
            w_a, w_gate,
            jnp.tile(q_norm_g[l], ATT_HEADS).reshape(1, 512), jnp.tile(k_norm_g[l], ATT_KV_HEADS).reshape(1, 128),
            gsum, conv_w[l], conv_b[l].reshape(1, SSM_CONV_CH), bias128, alog128,
            jnp.repeat(d_skip[l], SSM_HEAD_DIM).reshape(1, SSM_D_INNER), ssm_norm_g[l].reshape(1, SSM_D_INNER), tri,
            gmlp_norm_g[l].reshape(1, MIX_W), w_spatial[l].astype(BF16), b_spatial[l].T,
            w_branch[l].astype(BF16), w_out[l].astype(BF16), w_ff1[l].astype(BF16), w_ff2[l].astype(BF16),
        )
        mod3 = mod[l].reshape(8, 1, 6 * D_MODEL)
        xp, k_l, v_l, s_l = _layer_path(xp, mod3, lw, ctx_row, nb_ctx, seq_ctx, None)
        new_k.append(k_l.reshape(nb_ctx, seq_ctx, ATT_KV_HEADS, HEAD_DIM))
        new_v.append(v_l.reshape(nb_ctx, seq_ctx, ATT_KV_HEADS, HEAD_DIM))
        new_s.append(s_l.reshape(nb_ctx, 2, SSM_HEADS, SSM_HEAD_DIM, SSM_STATE))
        lat = (rope_tabs,
               cache_k[:, l].reshape(nb_lat, -1, ATT_KV_HEADS * HEAD_DIM),
               cache_v[:, l].reshape(nb_lat, -1, ATT_KV_HEADS * HEAD_DIM),
               state_ssm[:, l].reshape(nb_lat, 2, SSM_HEADS * SSM_HEAD_DIM, SSM_STATE))
        xs, _, _, _ = _layer_path(xs, mod3, lw, lat_row, nb_lat, seq_lat, lat)

    return (xp.reshape(nb_ctx, seq_ctx, D_MODEL), xs.reshape(nb_lat, seq_lat, D_MODEL),
            jnp.stack(new_k, axis=1), jnp.stack(new_v, axis=1), jnp.stack(new_s, axis=1))
```

```python
import functools
import math

import numpy as np
import jax
import jax.numpy as jnp
from jax import lax
from jax.experimental import pallas as pl
from jax.experimental.pallas import tpu as pltpu

F32 = jnp.float32
BF16 = jnp.bfloat16

D_MODEL = 2048
DEPTH = 2
GRID_W = 64
MIX_W = 512
N_BRANCH = 4
ATT_HEADS = 8
ATT_KV_HEADS = 2
HEAD_DIM = 64
ROPE_THETA = 10000.0
FNET_GW = 128
SSM_HEADS = 8
SSM_HEAD_DIM = 64
SSM_STATE = 128
SSM_CHUNK = 128
SSM_D_INNER = 512
SSM_CONV_CH = 1024
GMLP_CHUNK = 128
D_FF = 4 * D_MODEL
EPS = 1e-6
LAT_LEN = 1024

QKV_W = 768
DT_PAD = 128
W_A_COLS = 768 + 512 + 1024 + 512 + DT_PAD + 1024
COL_F = 768
COL_XBC = 1280
COL_Z = 2304
COL_DT = 2816
COL_UV = 2944
GATE_COL0 = 3856

VMEM_LIMIT = 56 * 1024 * 1024
NEG_BIG = -1e30


def _params(*sem):
    return pltpu.CompilerParams(dimension_semantics=sem, vmem_limit_bytes=VMEM_LIMIT)


def _dot(a, b):
    return jnp.dot(a, b, preferred_element_type=F32)


def _dot_nt(a, b):
    return lax.dot_general(a, b, (((1,), (1,)), ((), ())), preferred_element_type=F32)


def _dot_tn(a, b):
    return lax.dot_general(a, b, (((0,), (0,)), ((), ())), preferred_element_type=F32)


def _sigmoid(x):
    return 1.0 / (1.0 + jnp.exp(-x))


def _split2(x):
    hi = x.astype(BF16)
    lo = (x - hi.astype(F32)).astype(BF16)
    return hi, lo


def _rms_mod(x, sc, sh):
    ms = jnp.mean(x * x, axis=-1, keepdims=True)
    return x * lax.rsqrt(ms + EPS) * (1.0 + sc) + sh


def _mod_kernel(cond_ref, w_ref, b_ref, o_ref):
    c = cond_ref[...]
    s = c * _sigmoid(c)
    o_ref[...] = _dot(s.astype(BF16), w_ref[...].astype(BF16)) + b_ref[...]


def _modulation(cond8, w_mod, b_mod):
    tn = 1024
    n6 = 6 * D_MODEL
    return pl.pallas_call(
        _mod_kernel,
        grid=(DEPTH, n6 // tn),
        in_specs=[
            pl.BlockSpec((8, D_MODEL), lambda l, n: (0, 0)),
            pl.BlockSpec((None, D_MODEL, tn), lambda l, n: (l, 0, n)),
            pl.BlockSpec((None, 1, tn), lambda l, n: (l, 0, n)),
        ],
        out_specs=pl.BlockSpec((None, 8, tn), lambda l, n: (l, 0, n)),
        out_shape=jax.ShapeDtypeStruct((DEPTH, 8, n6), F32),
        compiler_params=_params("arbitrary", "arbitrary"),
        name="modulation",
    )(cond8, w_mod, b_mod.reshape(DEPTH, 1, n6))


def _mod_spec(chunk, row_fn):
    return pl.BlockSpec((None, 1, D_MODEL), lambda m, *_: (row_fn(m), 0, chunk))


_IN_PIECES = ((0, QKV_W), (COL_F, COL_XBC), (COL_XBC, COL_Z), (COL_Z, COL_DT), (COL_DT, COL_UV),
              (COL_UV, W_A_COLS))


def _inproj_kernel(x_ref, sc_ref, sh_ref, w_ref, *out_refs):
    hb = _rms_mod(x_ref[...], sc_ref[...], sh_ref[...]).astype(BF16)
    for o_ref, (a, b) in zip(out_refs, _IN_PIECES):
        o_ref[...] = _dot(hb, w_ref[:, a:b])


def _inproj(x, mod3, w_a, row_fn):
    t = x.shape[0]
    tm = 256
    widths = [b - a for a, b in _IN_PIECES]
    return pl.pallas_call(
        _inproj_kernel,
        grid=(t // tm,),
        in_specs=[
            pl.BlockSpec((tm, D_MODEL), lambda m: (m, 0)),
            _mod_spec(1, lambda m: row_fn(m * tm)),
            _mod_spec(0, lambda m: row_fn(m * tm)),
            pl.BlockSpec((D_MODEL, W_A_COLS), lambda m: (0, 0), pipeline_mode=pl.Buffered(1)),
        ],
        out_specs=[pl.BlockSpec((tm, w), lambda m: (m, 0)) for w in widths],
        out_shape=[jax.ShapeDtypeStruct((t, w), F32) for w in widths],
        compiler_params=_params("arbitrary"),
        name="inproj",
    )(x, mod3, mod3, w_a)


def _head_rms(t, gsum, gain):
    t2 = t * t
    hi, lo = _split2(t2)
    ss = _dot(hi, gsum) + _dot(lo, gsum)
    return t * lax.rsqrt(ss * (1.0 / HEAD_DIM) + EPS) * gain


def _rope(t, cos, sin_signed):
    w = t.shape[-1]
    up = pltpu.roll(t, w - 16, 1)
    dn = pltpu.roll(t, 16, 1)
    lane = lax.broadcasted_iota(jnp.int32, t.shape, 1)
    partner = jnp.where((lane & 31) < 16, up, dn)
    return t * cos + partner * sin_signed


def _attend_heads(q, k_scr, v_scr, o_ref):
    qb = (q * (HEAD_DIM ** -0.5)).astype(BF16)
    group = ATT_HEADS // ATT_KV_HEADS
    for h in range(ATT_HEADS):
        j = h // group
        s = _dot_nt(qb[:, h * HEAD_DIM:(h + 1) * HEAD_DIM], k_scr[:, j * HEAD_DIM:(j + 1) * HEAD_DIM])
        m = jnp.max(s, axis=-1, keepdims=True)
        e = jnp.exp(s - m)
        den = jnp.sum(e, axis=-1, keepdims=True)
        o = _dot(e.astype(BF16), v_scr[:, j * HEAD_DIM:(j + 1) * HEAD_DIM]) / den
        o_ref[:, h * HEAD_DIM:(h + 1) * HEAD_DIM] = o.astype(o_ref.dtype)


def _attn_ctx_kernel(q_ref, kv_ref, qg_ref, kg_ref, gsum_ref, o_ref, kout_ref, vout_ref, k_scr, v_scr):
    gsum = gsum_ref[...]
    kv = kv_ref[...]
    k = _head_rms(kv[:, :128], gsum[:128, :128], kg_ref[...])
    v = kv[:, 128:]
    kout_ref[...] = k
    vout_ref[...] = v
    k_scr[...] = k.astype(BF16)
    v_scr[...] = v.astype(BF16)
    q = _head_rms(q_ref[...], gsum, qg_ref[...])
    _attend_heads(q, k_scr, v_scr, o_ref)


def _attn_lat_kernel(q_ref, kv_ref, qg_ref, kg_ref, gsum_ref, cq_ref, sq_ref, ck_ref, sk_ref,
                     cache_k_ref, cache_v_ref, o_ref, k_scr, v_scr):
    gsum = gsum_ref[...]
    past = cache_k_ref.shape[0]

    @pl.when(pl.program_id(1) == 0)
    def _():
        kv = kv_ref[...]
        k = _head_rms(kv[:, :128], gsum[:128, :128], kg_ref[...])
        k = _rope(k, ck_ref[...], sk_ref[...])
        k_scr[:past, :] = cache_k_ref[...].astype(BF16)
        v_scr[:past, :] = cache_v_ref[...].astype(BF16)
        k_scr[past:, :] = k.astype(BF16)
        v_scr[past:, :] = kv[:, 128:].astype(BF16)

    q = _head_rms(q_ref[...], gsum, qg_ref[...])
    q = _rope(q, cq_ref[...], sq_ref[...])
    _attend_heads(q, k_scr, v_scr, o_ref)


def _attention_ctx(qkv, qg, kg, gsum, nb, seq):
    t = qkv.shape[0]
    const = lambda shape: pl.BlockSpec(shape, lambda b: (0, 0))
    return pl.pallas_call(
        _attn_ctx_kernel,
        grid=(nb,),
        in_specs=[
            pl.BlockSpec((seq, 512), lambda b: (b, 0)),
            pl.BlockSpec((seq, 256), lambda b: (b, 2)),
            const((1, 512)), const((1, 128)), const((512, 512)),
        ],
        out_specs=[
            pl.BlockSpec((seq, 512), lambda b: (b, 0)),
            pl.BlockSpec((seq, 128), lambda b: (b, 0)),
            pl.BlockSpec((seq, 128), lambda b: (b, 0)),
        ],
        out_shape=[
            jax.ShapeDtypeStruct((t, 512), BF16),
            jax.ShapeDtypeStruct((t, 128), F32),
            jax.ShapeDtypeStruct((t, 128), F32),
        ],
        scratch_shapes=[pltpu.VMEM((seq, 128), BF16), pltpu.VMEM((seq, 128), BF16)],
        compiler_params=_params("arbitrary"),
        name="attn_ctx",
    )(qkv, qkv, qg, kg, gsum)


def _attention_lat(qkv, qg, kg, gsum, rope_tabs, cache_k, cache_v, nb, seq):
    t = qkv.shape[0]
    qblk = 256
    nq = seq // qblk
    past = cache_k.shape[1]
    cq, sq, ck, sk = rope_tabs
    const = lambda shape: pl.BlockSpec(shape, lambda b, i: (0, 0))
    return pl.pallas_call(
        _attn_lat_kernel,
        grid=(nb, nq),
        in_specs=[
            pl.BlockSpec((qblk, 512), lambda b, i: (b * nq + i, 0)),
            pl.BlockSpec((seq, 256), lambda b, i: (b, 2)),
            const((1, 512)), const((1, 128)), const((512, 512)),
            pl.BlockSpec((qblk, 512), lambda b, i: (i, 0)),
            pl.BlockSpec((qblk, 512), lambda b, i: (i, 0)),
            const((seq, 128)), const((seq, 128)),
            pl.BlockSpec((None, past, 128), lambda b, i: (b, 0, 0)),
            pl.BlockSpec((None, past, 128), lambda b, i: (b, 0, 0)),
        ],
        out_specs=pl.BlockSpec((qblk, 512), lambda b, i: (b * nq + i, 0)),
        out_shape=jax.ShapeDtypeStruct((t, 512), BF16),
        scratch_shapes=[pltpu.VMEM((past + seq, 128), BF16), pltpu.VMEM((past + seq, 128), BF16)],
        compiler_params=_params("arbitrary", "arbitrary"),
        name="attn_lat",
    )(qkv, qkv, qg, kg, gsum, cq, sq, ck, sk, cache_k, cache_v)


def _fourier_kernel(f_ref, bdh_ref, bdl_ref, clh_ref, cll_ref, slh_ref, sll_ref, o_ref, *, scale):
    xh, xl = _split2(f_ref[...])
    bdh = bdh_ref[...]
    y = _dot(xh, bdh) + _dot(xl, bdh) + _dot(xh, bdl_ref[...])
    ych, ycl = _split2(y[:, :MIX_W])
    ysh, ysl = _split2(y[:, MIX_W:])
    clh = clh_ref[...]
    slh = slh_ref[...]
    re = _dot(clh, ych) + _dot(cll_ref[...], ych) + _dot(clh, ycl)
    im = _dot(slh, ysh) + _dot(sll_ref[...], ysh) + _dot(slh, ysl)
    o_ref[...] = ((re - im) * scale).astype(o_ref.dtype)


def _dft_consts(seq):
    def cs(n):
        k = np.arange(n, dtype=np.int64)
        ang = 2.0 * np.pi * ((k[:, None] * k[None, :]) % n).astype(np.float64) / n
        return np.cos(ang), np.sin(ang)

    cc, sc = cs(FNET_GW)
    groups = MIX_W // FNET_GW
    bd = np.zeros((MIX_W, 2 * MIX_W), np.float64)
    for g in range(groups):
        sl = slice(g * FNET_GW, (g + 1) * FNET_GW)
        bd[sl, sl] = cc
        bd[sl, MIX_W + g * FNET_GW:MIX_W + (g + 1) * FNET_GW] = sc
    cl, sl_ = cs(seq)

    def hl(a):
        a32 = jnp.asarray(a, F32)
        hi = a32.astype(BF16)
        return hi, (a32 - hi.astype(F32)).astype(BF16)

    return hl(bd) + hl(cl) + hl(sl_)


def _fourier(f, nb, seq):
    t = f.shape[0]
    consts = _dft_consts(seq)
    const = lambda a: pl.BlockSpec(a.shape, lambda b: (0, 0))
    return pl.pallas_call(
        functools.partial(_fourier_kernel, scale=1.0 / math.sqrt(seq * FNET_GW)),
        grid=(nb,),
        in_specs=[pl.BlockSpec((seq, MIX_W), lambda b: (b, 0))] + [const(a) for a in consts],
        out_specs=pl.BlockSpec((seq, MIX_W), lambda b: (b, 0)),
        out_shape=jax.ShapeDtypeStruct((t, MIX_W), BF16),
        compiler_params=_params("arbitrary"),
        name="fourier",
    )(f, *consts)


def _ssd_kernel(*refs, seq, has_h0, emit_state):
    (xbc_ref, z_ref, dt_ref, cw_ref, cb_ref, bias_ref, alog_ref, dskip_ref, ng_ref, tri_ref) = refs[:10]
    rest = list(refs[10:])
    h0_ref = rest.pop(0) if has_h0 else None
    o_ref = rest.pop(0)
    sfin_ref = rest.pop(0) if emit_state else None
    xt_scr, bc_scr, tab, tabt, yt_scr, s_scr = rest
    nc = seq // SSM_CHUNK
    q = SSM_CHUNK

    lane = lax.broadcasted_iota(jnp.int32, (q, 128), 1)
    ii = lax.broadcasted_iota(jnp.int32, (q, q), 0)
    jj = lax.broadcasted_iota(jnp.int32, (q, q), 1)
    below = ii > jj
    above = ii < jj
    row = lax.broadcasted_iota(jnp.int32, (q, 1), 0)
    cw = cw_ref[...]
    amul = jnp.where(lane < 16, 1.0, jnp.where(lane < 48, -jnp.exp(alog_ref[...]), 0.0))
    tri = tri_ref[...]

    if has_h0:
        s_scr[...] = h0_ref[...]
    else:
        s_scr[...] = jnp.zeros_like(s_scr)

    def prep(c, carry):
        r0 = pl.multiple_of(c * q, q)
        cur = xbc_ref[pl.ds(r0, q), :]
        prev8 = xbc_ref[pl.ds(pl.multiple_of(jnp.maximum(r0 - 8, 0), 8), 8), :]
        next8 = xbc_ref[pl.ds(pl.multiple_of(jnp.minimum(r0 + q, seq - 8), 8), 8), :]
        prev_row = jnp.where(c > 0, prev8[7:8, :], 0.0)
        next_row = jnp.where(c < nc - 1, next8[0:1, :], 0.0)
        up = jnp.where(row == 0, prev_row, pltpu.roll(cur, 1, 0))
        dn = jnp.where(row == q - 1, next_row, pltpu.roll(cur, q - 1, 0))
        conv = up * cw[0:1, :] + cur * cw[1:2, :] + dn * cw[2:3, :] + cb_ref[...]
        act = conv * _sigmoid(conv)
        xt_scr[c] = act[:, :SSM_D_INNER].T
        bc_scr[pl.ds(r0, q), :] = act[:, SSM_D_INNER:].astype(BF16)

        v = dt_ref[pl.ds(r0, q), :] + bias_ref[...]
        sp = jnp.maximum(v, 0.0) + jnp.log1p(jnp.exp(-jnp.abs(v)))
        t = sp * amul
        hi = t.astype(BF16)
        r1 = t - hi.astype(F32)
        mid = r1.astype(BF16)
        lo = (r1 - mid.astype(F32)).astype(BF16)
        cum = _dot(tri, hi) + _dot(tri, mid) + _dot(tri, lo)
        tc = jnp.where(lane < 32, t, cum)
        tab[pl.ds(r0, q), :] = tc
        tabt[c] = tc.T
        return carry

    lax.fori_loop(0, nc, prep, 0)

    def pass_fwd(c, carry):
        r0 = pl.multiple_of(c * q, q)
        pc = tab[pl.ds(r0, q), :]
        pt = tabt[c]
        for g in range(2):
            bg = bc_scr[pl.ds(r0, q), 128 * g:128 * (g + 1)]
            cg = bc_scr[pl.ds(r0, q), 256 + 128 * g:256 + 128 * (g + 1)]
            cb = _dot_nt(cg, bg)
            y_in = _dot_nt(s_scr[0, 256 * g:256 * (g + 1), :].astype(BF16), cg)
            for hh in range(4):
                h = 4 * g + hh
                acf_row = pt[32 + h:33 + h, :]
                eb_col = pc[:, 40 + h:41 + h] - pc[:, 24 + h:25 + h]
                eb_row = pt[40 + h:41 + h, :] - pt[24 + h:25 + h, :]
                dtf_row = pt[h:h + 1, :]
                dtb_row = pt[8 + h:9 + h, :]
                seg = jnp.where(below, pc[:, 32 + h:33 + h] - acf_row, jnp.where(above, eb_row - eb_col, 0.0))
                wgt = jnp.where(below, dtf_row, jnp.where(above, dtb_row, dtf_row + dtb_row))
                mp = cb * (jnp.exp(seg) * wgt)
                xth = xt_scr[c, 64 * h:64 * (h + 1), :]
                yd = _dot_nt(xth.astype(BF16), mp.astype(BF16))
                yo = y_in[64 * hh:64 * (hh + 1), :] * jnp.exp(acf_row)
                yt_scr[c, 64 * h:64 * (h + 1), :] = yd + yo + dskip_ref[h] * xth
                tot = pt[32 + h:33 + h, q - 1:q]
                wf_row = dtf_row * jnp.exp(tot - acf_row)
                st = _dot((xth * wf_row).astype(BF16), bg)
                s_scr[0, 64 * h:64 * (h + 1), :] = s_scr[0, 64 * h:64 * (h + 1), :] * jnp.exp(tot) + st
        return carry

    lax.fori_loop(0, nc, pass_fwd, 0)

    def pass_bwd(ci, carry):
        c = nc - 1 - ci
        r0 = pl.multiple_of(c * q, q)
        pt = tabt[c]
        for g in range(2):
            bg = bc_scr[pl.ds(r0, q), 128 * g:128 * (g + 1)]
            cg = bc_scr[pl.ds(r0, q), 256 + 128 * g:256 + 128 * (g + 1)]
            y_in = _dot_nt(s_scr[1, 256 * g:256 * (g + 1), :].astype(BF16), cg)
            for hh in range(4):
                h = 4 * g + hh
                eb_row = pt[40 + h:41 + h, :] - pt[24 + h:25 + h, :]
                tot = pt[40 + h:41 + h, q - 1:q]
                xth = xt_scr[c, 64 * h:64 * (h + 1), :]
                yt_scr[c, 64 * h:64 * (h + 1), :] += y_in[64 * hh:64 * (hh + 1), :] * jnp.exp(tot - eb_row)
                wb_row = pt[8 + h:9 + h, :] * jnp.exp(eb_row)
                st = _dot((xth * wb_row).astype(BF16), bg)
                s_scr[1, 64 * h:64 * (h + 1), :] = s_scr[1, 64 * h:64 * (h + 1), :] * jnp.exp(tot) + st
        zc = z_ref[pl.ds(r0, q), :]
        yg = yt_scr[c].T * (zc * _sigmoid(zc))
        ms = jnp.mean(yg * yg, axis=-1, keepdims=True)
        o_ref[pl.ds(r0, q), :] = (yg * lax.rsqrt(ms + EPS) * ng_ref[...]).astype(o_ref.dtype)
        return carry

    lax.fori_loop(0, nc, pass_bwd, 0)

    if emit_state:
        sfin_ref[...] = s_scr[...]


def _ssd(xbc, z, dt, conv_w, conv_b, bias128, alog128, dskip, norm_g, tri, h0, nb, seq, emit_state):
    t = xbc.shape[0]
    has_h0 = h0 is not None
    const = lambda shape: pl.BlockSpec(shape, lambda b: (0,) * len(shape))
    in_specs = [
        pl.BlockSpec((seq, SSM_CONV_CH), lambda b: (b, 0)),
        pl.BlockSpec((seq, SSM_D_INNER), lambda b: (b, 0)),
        pl.BlockSpec((seq, DT_PAD), lambda b: (b, 0)),
        const((3, SSM_CONV_CH)), const((1, SSM_CONV_CH)), const((1, 128)), const((1, 128)),
        pl.BlockSpec(memory_space=pltpu.SMEM), const((1, SSM_D_INNER)), const((128, 128)),
    ]
    args = [xbc, z, dt, conv_w, conv_b, bias128, alog128, dskip, norm_g, tri]
    if has_h0:
        in_specs.append(pl.BlockSpec((None, 2, 512, 128), lambda b: (b, 0, 0, 0)))
        args.append(h0)
    out_specs = [pl.BlockSpec((seq, SSM_D_INNER), lambda b: (b, 0))]
    out_shape = [jax.ShapeDtypeStruct((t, SSM_D_INNER), BF16)]
    if emit_state:
        out_specs.append(pl.BlockSpec((None, 2, 512, 128), lambda b: (b, 0, 0, 0)))
        out_shape.append(jax.ShapeDtypeStruct((nb, 2, 512, 128), F32))
    return pl.pallas_call(
        functools.partial(_ssd_kernel, seq=seq, has_h0=has_h0, emit_state=emit_state),
        grid=(nb,),
        in_specs=in_specs,
        out_specs=out_specs,
        out_shape=out_shape,
        scratch_shapes=[
            pltpu.VMEM((seq // SSM_CHUNK, SSM_D_INNER, SSM_CHUNK), F32),
            pltpu.VMEM((seq, 512), BF16),
            pltpu.VMEM((seq, 128), F32),
            pltpu.VMEM((seq // SSM_CHUNK, 128, SSM_CHUNK), F32),
            pltpu.VMEM((seq // SSM_CHUNK, SSM_D_INNER, SSM_CHUNK), F32),
            pltpu.VMEM((2, 512, 128), F32),
        ],
        compiler_params=_params("arbitrary"),
        name="ssd",
    )(*args)


def _gmlp_kernel(uv_ref, g_ref, ws_ref, bs_ref, o_ref):
    rows = uv_ref.shape[0]
    c0 = math.sqrt(2.0 / math.pi)
    for c in range(rows // GMLP_CHUNK):
        sl = slice(c * GMLP_CHUNK, (c + 1) * GMLP_CHUNK)
        x = uv_ref[sl, :]
        ge = x * (0.5 * (1.0 + jnp.tanh(c0 * (x + 0.044715 * (x * x * x)))))
        u = ge[:, :MIX_W]
        v = ge[:, MIX_W:]
        v = v * lax.rsqrt(jnp.mean(v * v, axis=-1, keepdims=True) + EPS) * g_ref[...]
        for g in range(4):
            gl = slice(g * 128, (g + 1) * 128)
            mixed = _dot(ws_ref[g], v[:, gl].astype(BF16)) + bs_ref[:, g:g + 1]
            o_ref[sl, gl] = (u[:, gl] * mixed).astype(o_ref.dtype)


def _gmlp(uv, norm_g, ws, bs_t):
    t = uv.shape[0]
    tm = 256
    return pl.pallas_call(
        _gmlp_kernel,
        grid=(t // tm,),
        in_specs=[
            pl.BlockSpec((tm, 2 * MIX_W), lambda m: (m, 0)),
            pl.BlockSpec((1, MIX_W), lambda m: (0, 0)),
            pl.BlockSpec((4, 128, 128), lambda m: (0, 0, 0)),
            pl.BlockSpec((128, 4), lambda m: (0, 0)),
        ],
        out_specs=pl.BlockSpec((tm, MIX_W), lambda m: (m, 0)),
        out_shape=jax.ShapeDtypeStruct((t, MIX_W), BF16),
        compiler_params=_params("arbitrary"),
        name="gmlp",
    )(uv, norm_g, ws, bs_t)


def _merge_kernel(x_ref, sc_ref, sh_ref, g_ref, b0, b1, b2, b3, wg0, wg1, wg2, wg3, wb_ref, wo_ref,
                  o_ref, h_scr, acc_scr):
    n = pl.program_id(1)

    @pl.when(n == 0)
    def _():
        h_scr[...] = _rms_mod(x_ref[...], sc_ref[...], sh_ref[...]).astype(BF16)
        acc_scr[...] = jnp.zeros_like(acc_scr)

    h = h_scr[...]
    merged = None
    for k, (b_ref, wg_ref) in enumerate(((b0, wg0), (b1, wg1), (b2, wg2), (b3, wg3))):
        gate = _sigmoid(_dot(h, wg_ref[...]))
        term = gate * _dot(b_ref[...], wb_ref[k])
        merged = term if merged is None else merged + term
    acc_scr[...] += _dot(merged.astype(BF16), wo_ref[...])

    @pl.when(n == pl.num_programs(1) - 1)
    def _():
        o_ref[...] = x_ref[...] + g_ref[...] * acc_scr[...]


def _merge(x, mod3, branches, w_gate, w_branch, w_out, row_fn):
    t = x.shape[0]
    tm, tn = 512, 256
    nn = D_MODEL // tn
    rf = lambda m: row_fn(m * tm)
    gate_spec = lambda k: pl.BlockSpec((D_MODEL, tn), lambda m, n: (0, k * nn + n))
    return pl.pallas_call(
        _merge_kernel,
        grid=(t // tm, nn),
        in_specs=[
            pl.BlockSpec((tm, D_MODEL), lambda m, n: (m, 0)),
            _mod_spec(1, rf), _mod_spec(0, rf), _mod_spec(2, rf),
        ] + [pl.BlockSpec((tm, MIX_W), lambda m, n: (m, 0))] * 4
          + [gate_spec(k) for k in range(4)] + [
            pl.BlockSpec((N_BRANCH, MIX_W, tn), lambda m, n: (0, 0, n)),
            pl.BlockSpec((tn, D_MODEL), lambda m, n: (n, 0)),
        ],
        out_specs=pl.BlockSpec((tm, D_MODEL), lambda m, n: (m, 0)),
        out_shape=jax.ShapeDtypeStruct((t, D_MODEL), F32),
        scratch_shapes=[pltpu.VMEM((tm, D_MODEL), BF16), pltpu.VMEM((tm, D_MODEL), F32)],
        compiler_params=_params("arbitrary", "arbitrary"),
        name="merge",
    )(x, mod3, mod3, mod3, *branches, w_gate, w_gate, w_gate, w_gate, w_branch, w_out)


def _ffn_kernel(x_ref, sc_ref, sh_ref, g_ref, w1_ref, w2_ref, o_ref, h_scr):
    n = pl.program_id(1)

    @pl.when(n == 0)
    def _():
        h_scr[...] = _rms_mod(x_ref[...], sc_ref[...], sh_ref[...]).astype(BF16)
        o_ref[...] = jnp.zeros_like(o_ref)

    a = jnp.maximum(_dot(h_scr[...], w1_ref[...].astype(BF16)), 0.0)
    o_ref[...] += _dot((a * a).astype(BF16), w2_ref[...].astype(BF16))

    @pl.when(n == pl.num_programs(1) - 1)
    def _():
        o_ref[...] = x_ref[...] + g_ref[...] * o_ref[...]


def _ffn(x, mod3, w1, w2, layer, row_fn):
    t = x.shape[0]
    tm, tf = 1024, 512
    assert t % tm == 0
    rf = lambda m: row_fn(m * tm)
    return pl.pallas_call(
        _ffn_kernel,
        grid=(t // tm, D_FF // tf),
        in_specs=[
            pl.BlockSpec((tm, D_MODEL), lambda m, n: (m, 0), pipeline_mode=pl.Buffered(1)),
            _mod_spec(4, rf), _mod_spec(3, rf), _mod_spec(5, rf),
            pl.BlockSpec((None, D_MODEL, tf), lambda m, n: (layer, 0, n)),
            pl.BlockSpec((None, tf, D_MODEL), lambda m, n: (layer, n, 0)),
        ],
        out_specs=pl.BlockSpec((tm, D_MODEL), lambda m, n: (m, 0)),
        out_shape=jax.ShapeDtypeStruct((t, D_MODEL), F32),
        scratch_shapes=[pltpu.VMEM((tm, D_MODEL), BF16)],
        compiler_params=_params("arbitrary", "arbitrary"),
        name="ffn",
    )(x, mod3, mod3, mod3, w1, w2)


def _rope_tables(seq):
    half = 16
    freqs = ROPE_THETA ** (-jnp.arange(half, dtype=F32) / half)
    n_rows = seq // GRID_W
    rows = jnp.repeat(jnp.arange(n_rows), GRID_W).astype(F32)
    cols = jnp.tile(jnp.arange(GRID_W), n_rows).astype(F32)
    ar = rows[:, None] * freqs[None, :]
    ac = cols[:, None] * freqs[None, :]
    cos64 = jnp.concatenate([jnp.cos(ar), jnp.cos(ar), jnp.cos(ac), jnp.cos(ac)], axis=-1)
    sin64 = jnp.concatenate([-jnp.sin(ar), jnp.sin(ar), -jnp.sin(ac), jnp.sin(ac)], axis=-1)
    return (jnp.tile(cos64, (1, ATT_HEADS)), jnp.tile(sin64, (1, ATT_HEADS)),
            jnp.tile(cos64, (1, ATT_KV_HEADS)), jnp.tile(sin64, (1, ATT_KV_HEADS)))


def _layer_path(x, mod3, lw, row_fn, nb, seq, lat):
    (w_a, w_gate, qg, kg, gsum, conv_w, conv_b, bias128, alog128, dskip, ssm_g, tri, gmlp_g, ws, bs_t,
     w_branch, w_out, w_ff1, w_ff2, layer) = lw
    qkv, f_in, xbc, z, dt, uv = _inproj(x, mod3, w_a, row_fn)
    if lat is None:
        o_att, k_new, v_new = _attention_ctx(qkv, qg, kg, gsum, nb, seq)
        h0 = None
    else:
        rope_tabs, cache_k, cache_v, h0 = lat
        o_att = _attention_lat(qkv, qg, kg, gsum, rope_tabs, cache_k, cache_v, nb, seq)
        k_new = v_new = None
    o_fnet = _fourier(f_in, nb, seq)
    ssd_out = _ssd(xbc, z, dt, conv_w, conv_b, bias128, alog128, dskip, ssm_g, tri, h0, nb, seq,
                   emit_state=lat is None)
    o_ssm = ssd_out[0]
    s_fin = ssd_out[1] if lat is None else None
    o_gmlp = _gmlp(uv, gmlp_g, ws, bs_t)
    x = _merge(x, mod3, (o_att, o_fnet, o_ssm, o_gmlp), w_gate, w_branch, w_out, row_fn)
    x = _ffn(x, mod3, w_ff1, w_ff2, layer, row_fn)
    return x, k_new, v_new, s_fin


def kernel(x_prompt, x_sample, c, cache_k, cache_v, state_ssm, c_ctx, w_mod, b_mod, w_in, q_norm_g, k_norm_g,
           conv_w, conv_b, a_log, dt_bias, d_skip, ssm_norm_g, gmlp_norm_g, w_spatial, b_spatial, w_branch,
           w_out, w_ff1, w_ff2):
    nb_ctx, seq_ctx, _ = x_prompt.shape
    nb_lat, seq_lat, _ = x_sample.shape
    assert seq_lat == LAT_LEN and 1 + nb_lat <= 8

    cond8 = jnp.zeros((8, D_MODEL), F32).at[0].set(c_ctx).at[1:1 + nb_lat].set(c)
    mod = _modulation(cond8, w_mod, b_mod)

    gsum = jnp.asarray(np.kron(np.eye(ATT_HEADS), np.ones((HEAD_DIM, HEAD_DIM))), BF16)
    tri = jnp.asarray(np.tril(np.ones((SSM_CHUNK, SSM_CHUNK))), BF16)
    rope_tabs = _rope_tables(seq_lat)

    xp = x_prompt.reshape(nb_ctx * seq_ctx, D_MODEL)
    xs = x_sample.reshape(nb_lat * seq_lat, D_MODEL)
    ctx_row = lambda tok: 0
    lat_row = lambda tok: 1 + tok // LAT_LEN

    new_k, new_v, new_s = [], [], []
    for l in range(DEPTH):
        wl = w_in[l]
        w_a = jnp.concatenate(
            [wl[:, :COL_DT], jnp.tile(wl[:, COL_DT:COL_DT + 16], (1, 3)),
             jnp.zeros((D_MODEL, DT_PAD - 48), F32), wl[:, COL_DT + 16:GATE_COL0]], axis=1).astype(BF16)
        w_gate = wl[:, GATE_COL0:].astype(BF16)
        zeros80 = jnp.zeros((DT_PAD - 48,), F32)
        bias128 = jnp.concatenate([jnp.tile(dt_bias[l].reshape(16), 3), zeros80]).reshape(1, 128)
        alog128 = jnp.concatenate([jnp.zeros((16,), F32), jnp.tile(a_log[l].reshape(16), 2), zeros80]).reshape(1, 128)
        lw = (
            w_a, w_gate,
            jnp.tile(q_norm_g[l], ATT_HEADS).reshape(1, 512), jnp.tile(k_norm_g[l], ATT_KV_HEADS).reshape(1, 128),
            gsum, conv_w[l], conv_b[l].reshape(1, SSM_CONV_CH), bias128, alog128,
            d_skip[l], ssm_norm_g[l].reshape(1, SSM_D_INNER), tri,
            gmlp_norm_g[l].reshape(1, MIX_W), w_spatial[l].astype(BF16), b_spatial[l].T,
            w_branch[l].astype(BF16), w_out[l].astype(BF16), w_ff1, w_ff2, l,
        )
        mod3 = mod[l].reshape(8, 1, 6 * D_MODEL)
        xp, k_l, v_l, s_l = _layer_path(xp, mod3, lw, ctx_row, nb_ctx, seq_ctx, None)
        new_k.append(k_l.reshape(nb_ctx, seq_ctx, ATT_KV_HEADS, HEAD_DIM))
        new_v.append(v_l.reshape(nb_ctx, seq_ctx, ATT_KV_HEADS, HEAD_DIM))
        new_s.append(s_l.reshape(nb_ctx, 2, SSM_HEADS, SSM_HEAD_DIM, SSM_STATE))
        lat = (rope_tabs,
               cache_k[:, l].reshape(nb_lat, -1, ATT_KV_HEADS * HEAD_DIM),
               cache_v[:, l].reshape(nb_lat, -1, ATT_KV_HEADS * HEAD_DIM),
               state_ssm[:, l].reshape(nb_lat, 2, SSM_HEADS * SSM_HEAD_DIM, SSM_STATE))
        xs, _, _, _ = _layer_path(xs, mod3, lw, lat_row, nb_lat, seq_lat, lat)

    return (xp.reshape(nb_ctx, seq_ctx, D_MODEL), xs.reshape(nb_lat, seq_lat, D_MODEL),
            jnp.stack(new_k, axis=1), jnp.stack(new_v, axis=1), jnp.stack(new_s, axis=1))
```

```python
import functools
import math

import numpy as np
import jax
import jax.numpy as jnp
from jax import lax
from jax.experimental import pallas as pl
from jax.experimental.pallas import tpu as pltpu

F32 = jnp.float32
BF16 = jnp.bfloat16

D_MODEL = 2048
DEPTH = 2
GRID_W = 64
MIX_W = 512
N_BRANCH = 4
ATT_HEADS = 8
ATT_KV_HEADS = 2
HEAD_DIM = 64
ROPE_THETA = 10000.0
FNET_GW = 128
SSM_HEADS = 8
SSM_HEAD_DIM = 64
SSM_STATE = 128
SSM_CHUNK = 128
SSM_D_INNER = 512
SSM_CONV_CH = 1024
GMLP_CHUNK = 128
D_FF = 4 * D_MODEL
EPS = 1e-6
LAT_LEN = 1024

QKV_W = 768
COL_F = 768
COL_XBC = 1280
COL_Z = 2304
COL_DT = 2816
UV_SHIFT = 16
DTUV_W = 1152
W_A_COLS = COL_DT + DTUV_W
GATE_COL0 = 3856
N_IN = GATE_COL0 + N_BRANCH * D_MODEL

VMEM_LIMIT = 56 * 1024 * 1024
NEG_BIG = -1e30


def _params(*sem):
    return pltpu.CompilerParams(dimension_semantics=sem, vmem_limit_bytes=VMEM_LIMIT)


def _dot(a, b):
    return jnp.dot(a, b, preferred_element_type=F32)


def _dot_nt(a, b):
    return lax.dot_general(a, b, (((1,), (1,)), ((), ())), preferred_element_type=F32)


def _dot_tn(a, b):
    return lax.dot_general(a, b, (((0,), (0,)), ((), ())), preferred_element_type=F32)


def _sigmoid(x):
    return 1.0 / (1.0 + jnp.exp(-x))


def _split2(x):
    hi = x.astype(BF16)
    lo = (x - hi.astype(F32)).astype(BF16)
    return hi, lo


def _rms_mod(x, sc, sh):
    ms = jnp.mean(x * x, axis=-1, keepdims=True)
    return x * lax.rsqrt(ms + EPS) * (1.0 + sc) + sh


def _mod_kernel(cond_ref, w_ref, b_ref, o_ref):
    c = cond_ref[...]
    s = c * _sigmoid(c)
    o_ref[...] = _dot(s.astype(BF16), w_ref[...].astype(BF16)) + b_ref[...]


def _modulation(cond8, w_mod, b_mod):
    tn = 1024
    n6 = 6 * D_MODEL
    return pl.pallas_call(
        _mod_kernel,
        grid=(DEPTH, n6 // tn),
        in_specs=[
            pl.BlockSpec((8, D_MODEL), lambda l, n: (0, 0)),
            pl.BlockSpec((None, D_MODEL, tn), lambda l, n: (l, 0, n)),
            pl.BlockSpec((None, 1, tn), lambda l, n: (l, 0, n)),
        ],
        out_specs=pl.BlockSpec((None, 8, tn), lambda l, n: (l, 0, n)),
        out_shape=jax.ShapeDtypeStruct((DEPTH, 8, n6), F32),
        compiler_params=_params("arbitrary", "arbitrary"),
        name="modulation",
    )(cond8, w_mod, b_mod.reshape(DEPTH, 1, n6))


def _mod_spec(chunk, row_fn):
    return pl.BlockSpec((None, 1, D_MODEL), lambda m, *_: (row_fn(m), 0, chunk))


_IN_PIECES = ((0, QKV_W), (COL_F, COL_XBC), (COL_XBC, COL_Z), (COL_Z, COL_DT), (COL_DT, W_A_COLS))


def _inproj_kernel(x_ref, sc_ref, sh_ref, w_ref, *out_refs):
    hb = _rms_mod(x_ref[...], sc_ref[...], sh_ref[...]).astype(BF16)
    for o_ref, (a, b) in zip(out_refs, _IN_PIECES):
        o_ref[...] = _dot(hb, w_ref[:, a:b])


def _inproj(x, mod3, w_a, row_fn):
    t = x.shape[0]
    tm = 256
    widths = [b - a for a, b in _IN_PIECES]
    return pl.pallas_call(
        _inproj_kernel,
        grid=(t // tm,),
        in_specs=[
            pl.BlockSpec((tm, D_MODEL), lambda m: (m, 0)),
            _mod_spec(1, lambda m: row_fn(m * tm)),
            _mod_spec(0, lambda m: row_fn(m * tm)),
            pl.BlockSpec((D_MODEL, W_A_COLS), lambda m: (0, 0), pipeline_mode=pl.Buffered(1)),
        ],
        out_specs=[pl.BlockSpec((tm, w), lambda m: (m, 0)) for w in widths],
        out_shape=[jax.ShapeDtypeStruct((t, w), F32) for w in widths],
        compiler_params=_params("arbitrary"),
        name="inproj",
    )(x, mod3, mod3, w_a)


def _gate_weights_kernel(a_ref, tail_ref, o_ref):
    w = a_ref.shape[-1]
    body = pltpu.roll(a_ref[0], w - UV_SHIFT, 1)
    tail = pltpu.roll(tail_ref[...], 128 - UV_SHIFT, 1)
    lane = lax.broadcasted_iota(jnp.int32, tail.shape, 1)
    o_ref[:, :w - 128] = body[:, :w - 128].astype(o_ref.dtype)
    o_ref[:, w - 128:] = jnp.where(lane >= 128 - UV_SHIFT, tail, body[:, w - 128:]).astype(o_ref.dtype)


def _gate_weights(w_in, layer):
    tn = 512
    width = N_BRANCH * D_MODEL
    base = GATE_COL0 - UV_SHIFT
    assert base % 128 == 0 and width % tn == 0
    return pl.pallas_call(
        _gate_weights_kernel,
        grid=(width // tn,),
        in_specs=[
            pl.BlockSpec((pl.Element(1), pl.Element(D_MODEL), pl.Element(tn)),
                         lambda n: (layer, 0, pl.multiple_of(base + n * tn, 128))),
            pl.BlockSpec((None, D_MODEL, 128), lambda n: (layer, 0, base // 128 + (n + 1) * (tn // 128))),
        ],
        out_specs=pl.BlockSpec((D_MODEL, tn), lambda n: (0, n)),
        out_shape=jax.ShapeDtypeStruct((D_MODEL, width), BF16),
        compiler_params=_params("arbitrary"),
        name="gate_weights",
    )(w_in, w_in)


def _head_rms(t, gsum, gain):
    t2 = t * t
    hi, lo = _split2(t2)
    ss = _dot(hi, gsum) + _dot(lo, gsum)
    return t * lax.rsqrt(ss * (1.0 / HEAD_DIM) + EPS) * gain


def _rope(t, cos, sin_signed):
    w = t.shape[-1]
    up = pltpu.roll(t, w - 16, 1)
    dn = pltpu.roll(t, 16, 1)
    lane = lax.broadcasted_iota(jnp.int32, t.shape, 1)
    partner = jnp.where((lane & 31) < 16, up, dn)
    return t * cos + partner * sin_signed


def _attend_heads(q, k_scr, v_scr, o_ref):
    qb = (q * (HEAD_DIM ** -0.5)).astype(BF16)
    group = ATT_HEADS // ATT_KV_HEADS
    for h in range(ATT_HEADS):
        j = h // group
        s = _dot_nt(qb[:, h * HEAD_DIM:(h + 1) * HEAD_DIM], k_scr[:, j * HEAD_DIM:(j + 1) * HEAD_DIM])
        m = jnp.max(s, axis=-1, keepdims=True)
        e = jnp.exp(s - m)
        den = jnp.sum(e, axis=-1, keepdims=True)
        o = _dot(e.astype(BF16), v_scr[:, j * HEAD_DIM:(j + 1) * HEAD_DIM]) / den
        o_ref[:, h * HEAD_DIM:(h + 1) * HEAD_DIM] = o.astype(o_ref.dtype)


def _attn_ctx_kernel(q_ref, kv_ref, qg_ref, kg_ref, gsum_ref, o_ref, kout_ref, vout_ref, k_scr, v_scr):
    gsum = gsum_ref[...]
    kv = kv_ref[...]
    k = _head_rms(kv[:, :128], gsum[:128, :128], kg_ref[...])
    v = kv[:, 128:]
    kout_ref[...] = k
    vout_ref[...] = v
    k_scr[...] = k.astype(BF16)
    v_scr[...] = v.astype(BF16)
    q = _head_rms(q_ref[...], gsum, qg_ref[...])
    _attend_heads(q, k_scr, v_scr, o_ref)


def _attn_lat_kernel(q_ref, kv_ref, qg_ref, kg_ref, gsum_ref, cq_ref, sq_ref, ck_ref, sk_ref,
                     cache_k_ref, cache_v_ref, o_ref, k_scr, v_scr):
    gsum = gsum_ref[...]
    past = cache_k_ref.shape[0]

    @pl.when(pl.program_id(1) == 0)
    def _():
        kv = kv_ref[...]
        k = _head_rms(kv[:, :128], gsum[:128, :128], kg_ref[...])
        k = _rope(k, ck_ref[...], sk_ref[...])
        k_scr[:past, :] = cache_k_ref[...].astype(BF16)
        v_scr[:past, :] = cache_v_ref[...].astype(BF16)
        k_scr[past:, :] = k.astype(BF16)
        v_scr[past:, :] = kv[:, 128:].astype(BF16)

    q = _head_rms(q_ref[...], gsum, qg_ref[...])
    q = _rope(q, cq_ref[...], sq_ref[...])
    _attend_heads(q, k_scr, v_scr, o_ref)


def _attention_ctx(qkv, qg, kg, gsum, nb, seq):
    t = qkv.shape[0]
    const = lambda shape: pl.BlockSpec(shape, lambda b: (0, 0))
    return pl.pallas_call(
        _attn_ctx_kernel,
        grid=(nb,),
        in_specs=[
            pl.BlockSpec((seq, 512), lambda b: (b, 0)),
            pl.BlockSpec((seq, 256), lambda b: (b, 2)),
            const((1, 512)), const((1, 128)), const((512, 512)),
        ],
        out_specs=[
            pl.BlockSpec((seq, 512), lambda b: (b, 0)),
            pl.BlockSpec((seq, 128), lambda b: (b, 0)),
            pl.BlockSpec((seq, 128), lambda b: (b, 0)),
        ],
        out_shape=[
            jax.ShapeDtypeStruct((t, 512), BF16),
            jax.ShapeDtypeStruct((t, 128), F32),
            jax.ShapeDtypeStruct((t, 128), F32),
        ],
        scratch_shapes=[pltpu.VMEM((seq, 128), BF16), pltpu.VMEM((seq, 128), BF16)],
        compiler_params=_params("arbitrary"),
        name="attn_ctx",
    )(qkv, qkv, qg, kg, gsum)


def _attention_lat(qkv, qg, kg, gsum, rope_tabs, cache_k, cache_v, nb, seq):
    t = qkv.shape[0]
    qblk = 256
    nq = seq // qblk
    past = cache_k.shape[1]
    cq, sq, ck, sk = rope_tabs
    const = lambda shape: pl.BlockSpec(shape, lambda b, i: (0, 0))
    return pl.pallas_call(
        _attn_lat_kernel,
        grid=(nb, nq),
        in_specs=[
            pl.BlockSpec((qblk, 512), lambda b, i: (b * nq + i, 0)),
            pl.BlockSpec((seq, 256), lambda b, i: (b, 2)),
            const((1, 512)), const((1, 128)), const((512, 512)),
            pl.BlockSpec((qblk, 512), lambda b, i: (i, 0)),
            pl.BlockSpec((qblk, 512), lambda b, i: (i, 0)),
            const((seq, 128)), const((seq, 128)),
            pl.BlockSpec((None, past, 128), lambda b, i: (b, 0, 0)),
            pl.BlockSpec((None, past, 128), lambda b, i: (b, 0, 0)),
        ],
        out_specs=pl.BlockSpec((qblk, 512), lambda b, i: (b * nq + i, 0)),
        out_shape=jax.ShapeDtypeStruct((t, 512), BF16),
        scratch_shapes=[pltpu.VMEM((past + seq, 128), BF16), pltpu.VMEM((past + seq, 128), BF16)],
        compiler_params=_params("arbitrary", "arbitrary"),
        name="attn_lat",
    )(qkv, qkv, qg, kg, gsum, cq, sq, ck, sk, cache_k, cache_v)


def _fourier_kernel(f_ref, bdh_ref, bdl_ref, clh_ref, cll_ref, slh_ref, sll_ref, o_ref, *, scale):
    xh, xl = _split2(f_ref[...])
    bdh = bdh_ref[...]
    bdl = bdl_ref[...]
    yc, ys = [], []
    for g in range(MIX_W // FNET_GW):
        gl = slice(g * FNET_GW, (g + 1) * FNET_GW)
        y = _dot(xh[:, gl], bdh) + _dot(xl[:, gl], bdh) + _dot(xh[:, gl], bdl)
        yc.append(y[:, :FNET_GW])
        ys.append(y[:, FNET_GW:])
    ych, ycl = _split2(jnp.concatenate(yc, axis=1))
    ysh, ysl = _split2(jnp.concatenate(ys, axis=1))
    clh = clh_ref[...]
    slh = slh_ref[...]
    re = _dot(clh, ych) + _dot(cll_ref[...], ych) + _dot(clh, ycl)
    im = _dot(slh, ysh) + _dot(sll_ref[...], ysh) + _dot(slh, ysl)
    o_ref[...] = ((re - im) * scale).astype(o_ref.dtype)


def _dft_consts(seq):
    def cs(n):
        k = np.arange(n, dtype=np.int64)
        ang = 2.0 * np.pi * ((k[:, None] * k[None, :]) % n).astype(np.float64) / n
        return np.cos(ang), np.sin(ang)

    bd = np.concatenate(cs(FNET_GW), axis=1)
    cl, sl_ = cs(seq)

    def hl(a):
        a32 = jnp.asarray(a, F32)
        hi = a32.astype(BF16)
        return hi, (a32 - hi.astype(F32)).astype(BF16)

    return hl(bd) + hl(cl) + hl(sl_)


def _fourier(f, nb, seq):
    t = f.shape[0]
    consts = _dft_consts(seq)
    const = lambda a: pl.BlockSpec(a.shape, lambda b: (0, 0))
    return pl.pallas_call(
        functools.partial(_fourier_kernel, scale=1.0 / math.sqrt(seq * FNET_GW)),
        grid=(nb,),
        in_specs=[pl.BlockSpec((seq, MIX_W), lambda b: (b, 0))] + [const(a) for a in consts],
        out_specs=pl.BlockSpec((seq, MIX_W), lambda b: (b, 0)),
        out_shape=jax.ShapeDtypeStruct((t, MIX_W), BF16),
        compiler_params=_params("arbitrary"),
        name="fourier",
    )(f, *consts)


def _ssd_kernel(*refs, seq, has_h0, emit_state):
    (xbc_ref, z_ref, dt_ref, cw_ref, cb_ref, bias_ref, alog_ref, dskip_ref, ng_ref, tri_ref) = refs[:10]
    rest = list(refs[10:])
    h0_ref = rest.pop(0) if has_h0 else None
    o_ref = rest.pop(0)
    sfin_ref = rest.pop(0) if emit_state else None
    xt_scr, bc_scr, tab, tabt, yt_scr, s_scr = rest
    nc = seq // SSM_CHUNK
    q = SSM_CHUNK

    lane = lax.broadcasted_iota(jnp.int32, (q, 128), 1)
    ii = lax.broadcasted_iota(jnp.int32, (q, q), 0)
    jj = lax.broadcasted_iota(jnp.int32, (q, q), 1)
    below = ii > jj
    above = ii < jj
    row = lax.broadcasted_iota(jnp.int32, (q, 1), 0)
    cw = cw_ref[...]
    amul = jnp.where(lane < 16, 1.0, jnp.where(lane < 48, -jnp.exp(alog_ref[...]), 0.0))
    tri = tri_ref[...]

    if has_h0:
        s_scr[...] = h0_ref[...]
    else:
        s_scr[...] = jnp.zeros_like(s_scr)

    def prep(c, carry):
        r0 = pl.multiple_of(c * q, q)
        cur = xbc_ref[pl.ds(r0, q), :]
        prev8 = xbc_ref[pl.ds(pl.multiple_of(jnp.maximum(r0 - 8, 0), 8), 8), :]
        next8 = xbc_ref[pl.ds(pl.multiple_of(jnp.minimum(r0 + q, seq - 8), 8), 8), :]
        prev_row = jnp.where(c > 0, prev8[7:8, :], 0.0)
        next_row = jnp.where(c < nc - 1, next8[0:1, :], 0.0)
        up = jnp.where(row == 0, prev_row, pltpu.roll(cur, 1, 0))
        dn = jnp.where(row == q - 1, next_row, pltpu.roll(cur, q - 1, 0))
        conv = up * cw[0:1, :] + cur * cw[1:2, :] + dn * cw[2:3, :] + cb_ref[...]
        act = conv * _sigmoid(conv)
        xt_scr[c] = act[:, :SSM_D_INNER].T
        bc_scr[pl.ds(r0, q), :] = act[:, SSM_D_INNER:].astype(BF16)

        raw = dt_ref[pl.ds(r0, q), :]
        rep = jnp.where(lane < 16, raw, jnp.where(lane < 32, pltpu.roll(raw, 16, 1), pltpu.roll(raw, 32, 1)))
        v = rep + bias_ref[...]
        sp = jnp.maximum(v, 0.0) + jnp.log1p(jnp.exp(-jnp.abs(v)))
        t = sp * amul
        hi = t.astype(BF16)
        r1 = t - hi.astype(F32)
        mid = r1.astype(BF16)
        lo = (r1 - mid.astype(F32)).astype(BF16)
        cum = _dot(tri, hi) + _dot(tri, mid) + _dot(tri, lo)
        tc = jnp.where(lane < 32, t, cum)
        tab[pl.ds(r0, q), :] = tc
        tabt[c] = tc.T
        return carry

    lax.fori_loop(0, nc, prep, 0)

    def pass_fwd(c, carry):
        r0 = pl.multiple_of(c * q, q)
        pc = tab[pl.ds(r0, q), :]
        pt = tabt[c]
        for g in range(2):
            bg = bc_scr[pl.ds(r0, q), 128 * g:128 * (g + 1)]
            cg = bc_scr[pl.ds(r0, q), 256 + 128 * g:256 + 128 * (g + 1)]
            cb = _dot_nt(cg, bg)
            y_in = _dot_nt(s_scr[0, 256 * g:256 * (g + 1), :].astype(BF16), cg)
            for hh in range(4):
                h = 4 * g + hh
                acf_row = pt[32 + h:33 + h, :]
                eb_col = pc[:, 40 + h:41 + h] - pc[:, 24 + h:25 + h]
                eb_row = pt[40 + h:41 + h, :] - pt[24 + h:25 + h, :]
                dtf_row = pt[h:h + 1, :]
                dtb_row = pt[8 + h:9 + h, :]
                seg = jnp.where(below, pc[:, 32 + h:33 + h] - acf_row, jnp.where(above, eb_row - eb_col, 0.0))
                wgt = jnp.where(below, dtf_row, jnp.where(above, dtb_row, dtf_row + dtb_row))
                mp = cb * (jnp.exp(seg) * wgt)
                xth = xt_scr[c, 64 * h:64 * (h + 1), :]
                yd = _dot_nt(xth.astype(BF16), mp.astype(BF16))
                yo = y_in[64 * hh:64 * (hh + 1), :] * jnp.exp(acf_row)
                yt_scr[c, 64 * h:64 * (h + 1), :] = yd + yo + dskip_ref[h] * xth
                tot = pt[32 + h:33 + h, q - 1:q]
                wf_row = dtf_row * jnp.exp(tot - acf_row)
                st = _dot((xth * wf_row).astype(BF16), bg)
                s_scr[0, 64 * h:64 * (h + 1), :] = s_scr[0, 64 * h:64 * (h + 1), :] * jnp.exp(tot) + st
        return carry

    lax.fori_loop(0, nc, pass_fwd, 0)

    def pass_bwd(ci, carry):
        c = nc - 1 - ci
        r0 = pl.multiple_of(c * q, q)
        pt = tabt[c]
        for g in range(2):
            bg = bc_scr[pl.ds(r0, q), 128 * g:128 * (g + 1)]
            cg = bc_scr[pl.ds(r0, q), 256 + 128 * g:256 + 128 * (g + 1)]
            y_in = _dot_nt(s_scr[1, 256 * g:256 * (g + 1), :].astype(BF16), cg)
            for hh in range(4):
                h = 4 * g + hh
                eb_row = pt[40 + h:41 + h, :] - pt[24 + h:25 + h, :]
                tot = pt[40 + h:41 + h, q - 1:q]
                xth = xt_scr[c, 64 * h:64 * (h + 1), :]
                yt_scr[c, 64 * h:64 * (h + 1), :] += y_in[64 * hh:64 * (hh + 1), :] * jnp.exp(tot - eb_row)
                wb_row = pt[8 + h:9 + h, :] * jnp.exp(eb_row)
                st = _dot((xth * wb_row).astype(BF16), bg)
                s_scr[1, 64 * h:64 * (h + 1), :] = s_scr[1, 64 * h:64 * (h + 1), :] * jnp.exp(tot) + st
        zc = z_ref[pl.ds(r0, q), :]
        yg = yt_scr[c].T * (zc * _sigmoid(zc))
        ms = jnp.mean(yg * yg, axis=-1, keepdims=True)
        o_ref[pl.ds(r0, q), :] = (yg * lax.rsqrt(ms + EPS) * ng_ref[...]).astype(o_ref.dtype)
        return carry

    lax.fori_loop(0, nc, pass_bwd, 0)

    if emit_state:
        sfin_ref[...] = s_scr[...]


def _ssd(xbc, z, dt, conv_w, conv_b, bias128, alog128, dskip, norm_g, tri, h0, nb, seq, emit_state):
    t = xbc.shape[0]
    has_h0 = h0 is not None
    const = lambda shape: pl.BlockSpec(shape, lambda b: (0,) * len(shape))
    in_specs = [
        pl.BlockSpec((seq, SSM_CONV_CH), lambda b: (b, 0)),
        pl.BlockSpec((seq, SSM_D_INNER), lambda b: (b, 0)),
        pl.BlockSpec((seq, 128), lambda b: (b, 0)),
        const((3, SSM_CONV_CH)), const((1, SSM_CONV_CH)), const((1, 128)), const((1, 128)),
        pl.BlockSpec(memory_space=pltpu.SMEM), const((1, SSM_D_INNER)), const((128, 128)),
    ]
    args = [xbc, z, dt, conv_w, conv_b, bias128, alog128, dskip, norm_g, tri]
    if has_h0:
        in_specs.append(pl.BlockSpec((None, 2, 512, 128), lambda b: (b, 0, 0, 0)))
        args.append(h0)
    out_specs = [pl.BlockSpec((seq, SSM_D_INNER), lambda b: (b, 0))]
    out_shape = [jax.ShapeDtypeStruct((t, SSM_D_INNER), BF16)]
    if emit_state:
        out_specs.append(pl.BlockSpec((None, 2, 512, 128), lambda b: (b, 0, 0, 0)))
        out_shape.append(jax.ShapeDtypeStruct((nb, 2, 512, 128), F32))
    return pl.pallas_call(
        functools.partial(_ssd_kernel, seq=seq, has_h0=has_h0, emit_state=emit_state),
        grid=(nb,),
        in_specs=in_specs,
        out_specs=out_specs,
        out_shape=out_shape,
        scratch_shapes=[
            pltpu.VMEM((seq // SSM_CHUNK, SSM_D_INNER, SSM_CHUNK), F32),
            pltpu.VMEM((seq, 512), BF16),
            pltpu.VMEM((seq, 128), F32),
            pltpu.VMEM((seq // SSM_CHUNK, 128, SSM_CHUNK), F32),
            pltpu.VMEM((seq // SSM_CHUNK, SSM_D_INNER, SSM_CHUNK), F32),
            pltpu.VMEM((2, 512, 128), F32),
        ],
        compiler_params=_params("arbitrary"),
        name="ssd",
    )(*args)


def _gmlp_kernel(uv_ref, g_ref, ws_ref, bs_ref, o_ref):
    rows = uv_ref.shape[0]
    c0 = math.sqrt(2.0 / math.pi)
    for c in range(rows // GMLP_CHUNK):
        sl = slice(c * GMLP_CHUNK, (c + 1) * GMLP_CHUNK)
        x = pltpu.roll(uv_ref[sl, :], DTUV_W - UV_SHIFT, 1)[:, :2 * MIX_W]
        ge =x * (0.5 * (1.0 + jnp.tanh(c0 * (x + 0.044715 * (x * x * x)))))
        u = ge[:, :MIX_W]
        v = ge[:, MIX_W:]
        v = v * lax.rsqrt(jnp.mean(v * v, axis=-1, keepdims=True) + EPS) * g_ref[...]
        for g in range(4):
            gl = slice(g * 128, (g + 1) * 128)
            mixed = _dot(ws_ref[g], v[:, gl].astype(BF16)) + bs_ref[:, g:g + 1]
            o_ref[sl, gl] = (u[:, gl] * mixed).astype(o_ref.dtype)


def _gmlp(uv, norm_g, ws, bs_t):
    t = uv.shape[0]
    tm = 256
    return pl.pallas_call(
        _gmlp_kernel,
        grid=(t // tm,),
        in_specs=[
            pl.BlockSpec((tm, DTUV_W), lambda m: (m, 0)),
            pl.BlockSpec((1, MIX_W), lambda m: (0, 0)),
            pl.BlockSpec((4, 128, 128), lambda m: (0, 0, 0)),
            pl.BlockSpec((128, 4), lambda m: (0, 0)),
        ],
        out_specs=pl.BlockSpec((tm, MIX_W), lambda m: (m, 0)),
        out_shape=jax.ShapeDtypeStruct((t, MIX_W), BF16),
        compiler_params=_params("arbitrary"),
        name="gmlp",
    )(uv, norm_g, ws, bs_t)


def _merge_kernel(x_ref, sc_ref, sh_ref, g_ref, b0, b1, b2, b3, wg0, wg1, wg2, wg3, wb_ref, wo_ref,
                  o_ref, h_scr):
    n = pl.program_id(1)

    @pl.when(n == 0)
    def _():
        h_scr[...] = _rms_mod(x_ref[...], sc_ref[...], sh_ref[...]).astype(BF16)
        o_ref[...] = jnp.zeros_like(o_ref)

    h = h_scr[...]
    merged = None
    for k, (b_ref, wg_ref) in enumerate(((b0, wg0), (b1, wg1), (b2, wg2), (b3, wg3))):
        gate = _sigmoid(_dot(h, wg_ref[...]))
        term = gate * _dot(b_ref[...], wb_ref[k].astype(BF16))
        merged = term if merged is None else merged + term
    o_ref[...] += _dot(merged.astype(BF16), wo_ref[...].astype(BF16))

    @pl.when(n == pl.num_programs(1) - 1)
    def _():
        o_ref[...] = x_ref[...] + g_ref[...] * o_ref[...]


def _merge(x, mod3, branches, w_gate, w_branch, w_out, layer, row_fn):
    t = x.shape[0]
    tm, tn = 512, 256
    assert t % tm == 0
    nn = D_MODEL // tn
    rf = lambda m: row_fn(m * tm)
    gate_spec = lambda k: pl.BlockSpec((D_MODEL, tn), lambda m, n: (0, k * nn + n))
    return pl.pallas_call(
        _merge_kernel,
        grid=(t // tm, nn),
        in_specs=[
            pl.BlockSpec((tm, D_MODEL), lambda m, n: (m, 0)),
            _mod_spec(1, rf), _mod_spec(0, rf), _mod_spec(2, rf),
        ] + [pl.BlockSpec((tm, MIX_W), lambda m, n: (m, 0))] * 4
          + [gate_spec(k) for k in range(4)] + [
            pl.BlockSpec((None, N_BRANCH, MIX_W, tn), lambda m, n: (layer, 0, 0, n)),
            pl.BlockSpec((None, tn, D_MODEL), lambda m, n: (layer, n, 0)),
        ],
        out_specs=pl.BlockSpec((tm, D_MODEL), lambda m, n: (m, 0)),
        out_shape=jax.ShapeDtypeStruct((t, D_MODEL), F32),
        scratch_shapes=[pltpu.VMEM((tm, D_MODEL), BF16)],
        compiler_params=_params("arbitrary", "arbitrary"),
        name="merge",
    )(x, mod3, mod3, mod3, *branches, w_gate, w_gate, w_gate, w_gate, w_branch, w_out)


def _ffn_kernel(x_ref, sc_ref, sh_ref, g_ref, w1_ref, w2_ref, o_ref, h_scr):
    n = pl.program_id(1)

    @pl.when(n == 0)
    def _():
        h_scr[...] = _rms_mod(x_ref[...], sc_ref[...], sh_ref[...]).astype(BF16)
        o_ref[...] = jnp.zeros_like(o_ref)

    a = jnp.maximum(_dot(h_scr[...], w1_ref[...].astype(BF16)), 0.0)
    o_ref[...] += _dot((a * a).astype(BF16), w2_ref[...].astype(BF16))

    @pl.when(n == pl.num_programs(1) - 1)
    def _():
        o_ref[...] = x_ref[...] + g_ref[...] * o_ref[...]


def _ffn(x, mod3, w1, w2, layer, row_fn):
    t = x.shape[0]
    tm, tf = 1024, 512
    assert t % tm == 0
    rf = lambda m: row_fn(m * tm)
    return pl.pallas_call(
        _ffn_kernel,
        grid=(t // tm, D_FF // tf),
        in_specs=[
            pl.BlockSpec((tm, D_MODEL), lambda m, n: (m, 0), pipeline_mode=pl.Buffered(1)),
            _mod_spec(4, rf), _mod_spec(3, rf), _mod_spec(5, rf),
            pl.BlockSpec((None, D_MODEL, tf), lambda m, n: (layer, 0, n)),
            pl.BlockSpec((None, tf, D_MODEL), lambda m, n: (layer, n, 0)),
        ],
        out_specs=pl.BlockSpec((tm, D_MODEL), lambda m, n: (m, 0)),
        out_shape=jax.ShapeDtypeStruct((t, D_MODEL), F32),
        scratch_shapes=[pltpu.VMEM((tm, D_MODEL), BF16)],
        compiler_params=_params("arbitrary", "arbitrary"),
        name="ffn",
    )(x, mod3, mod3, mod3, w1, w2)


def _rope_tables(seq):
    half = 16
    freqs = ROPE_THETA ** (-jnp.arange(half, dtype=F32) / half)
    n_rows = seq // GRID_W
    rows = jnp.repeat(jnp.arange(n_rows), GRID_W).astype(F32)
    cols = jnp.tile(jnp.arange(GRID_W), n_rows).astype(F32)
    ar = rows[:, None] * freqs[None, :]
    ac = cols[:, None] * freqs[None, :]
    cos64 = jnp.concatenate([jnp.cos(ar), jnp.cos(ar), jnp.cos(ac), jnp.cos(ac)], axis=-1)
    sin64 = jnp.concatenate([-jnp.sin(ar), jnp.sin(ar), -jnp.sin(ac), jnp.sin(ac)], axis=-1)
    return (jnp.tile(cos64, (1, ATT_HEADS)), jnp.tile(sin64, (1, ATT_HEADS)),
            jnp.tile(cos64, (1, ATT_KV_HEADS)), jnp.tile(sin64, (1, ATT_KV_HEADS)))


def _layer_path(x, mod3, lw, row_fn, nb, seq, lat):
    (w_a, w_gate, qg, kg, gsum, conv_w, conv_b, bias128, alog128, dskip, ssm_g, tri, gmlp_g, ws, bs_t,
     w_branch, w_out, w_ff1, w_ff2, layer) = lw
    qkv, f_in, xbc, z, dtuv = _inproj(x, mod3, w_a, row_fn)
    if lat is None:
        o_att, k_new, v_new = _attention_ctx(qkv, qg, kg, gsum, nb, seq)
        h0 = None
    else:
        rope_tabs, cache_k, cache_v, h0 = lat
        o_att = _attention_lat(qkv, qg, kg, gsum, rope_tabs, cache_k, cache_v, nb, seq)
        k_new = v_new = None
    o_fnet = _fourier(f_in, nb, seq)
    ssd_out = _ssd(xbc, z, dtuv, conv_w, conv_b, bias128, alog128, dskip, ssm_g, tri, h0, nb, seq,
                   emit_state=lat is None)
    o_ssm = ssd_out[0]
    s_fin = ssd_out[1] if lat is None else None
    o_gmlp = _gmlp(dtuv, gmlp_g, ws, bs_t)
    x = _merge(x, mod3, (o_att, o_fnet, o_ssm, o_gmlp), w_gate, w_branch, w_out, layer, row_fn)
    x = _ffn(x, mod3, w_ff1, w_ff2, layer, row_fn)
    return x, k_new, v_new, s_fin


def kernel(x_prompt, x_sample, c, cache_k, cache_v, state_ssm, c_ctx, w_mod, b_mod, w_in, q_norm_g, k_norm_g,
           conv_w, conv_b, a_log, dt_bias, d_skip, ssm_norm_g, gmlp_norm_g, w_spatial, b_spatial, w_branch,
           w_out, w_ff1, w_ff2):
    nb_ctx, seq_ctx, _ = x_prompt.shape
    nb_lat, seq_lat, _ = x_sample.shape
    assert seq_lat == LAT_LEN and 1 + nb_lat <= 8

    cond8 = jnp.zeros((8, D_MODEL), F32).at[0].set(c_ctx).at[1:1 + nb_lat].set(c)
    mod = _modulation(cond8, w_mod, b_mod)

    gsum = jnp.asarray(np.kron(np.eye(ATT_HEADS), np.ones((HEAD_DIM, HEAD_DIM))), BF16)
    tri = jnp.asarray(np.tril(np.ones((SSM_CHUNK, SSM_CHUNK))), BF16)
    rope_tabs = _rope_tables(seq_lat)

    xp = x_prompt.reshape(nb_ctx * seq_ctx, D_MODEL)
    xs = x_sample.reshape(nb_lat * seq_lat, D_MODEL)
    ctx_row = lambda tok: 0
    lat_row = lambda tok: 1 + tok // LAT_LEN

    new_k, new_v, new_s = [], [], []
    for l in range(DEPTH):
        w_a = w_in[l, :, :W_A_COLS].astype(BF16)
        w_gate = _gate_weights(w_in, l)
        zeros80 = jnp.zeros((128 - 48,), F32)
        bias128 = jnp.concatenate([jnp.tile(dt_bias[l].reshape(16), 3), zeros80]).reshape(1, 128)
        alog128 = jnp.concatenate([jnp.zeros((16,), F32), jnp.tile(a_log[l].reshape(16), 2), zeros80]).reshape(1, 128)
        lw = (
            w_a, w_gate,
            jnp.tile(q_norm_g[l], ATT_HEADS).reshape(1, 512), jnp.tile(k_norm_g[l], ATT_KV_HEADS).reshape(1, 128),
            gsum, conv_w[l], conv_b[l].reshape(1, SSM_CONV_CH), bias128, alog128,
            d_skip[l], ssm_norm_g[l].reshape(1, SSM_D_INNER), tri,
            gmlp_norm_g[l].reshape(1, MIX_W), w_spatial[l].astype(BF16), b_spatial[l].T,
            w_branch, w_out, w_ff1, w_ff2, l,
        )
        mod3 = mod[l].reshape(8, 1, 6 * D_MODEL)
        xp, k_l, v_l, s_l = _layer_path(xp, mod3, lw, ctx_row, nb_ctx, seq_ctx, None)
        new_k.append(k_l.reshape(nb_ctx, seq_ctx, ATT_KV_HEADS, HEAD_DIM))
        new_v.append(v_l.reshape(nb_ctx, seq_ctx, ATT_KV_HEADS, HEAD_DIM))
        new_s.append(s_l.reshape(nb_ctx, 2, SSM_HEADS, SSM_HEAD_DIM, SSM_STATE))
        lat = (rope_tabs,
               cache_k[:, l].reshape(nb_lat, -1, ATT_KV_HEADS * HEAD_DIM),
               cache_v[:, l].reshape(nb_lat, -1, ATT_KV_HEADS * HEAD_DIM),
               state_ssm[:, l].reshape(nb_lat, 2, SSM_HEADS * SSM_HEAD_DIM, SSM_STATE))
        xs, _, _, _ = _layer_path(xs, mod3, lw, lat_row, nb_lat, seq_lat, lat)

    return (xp.reshape(nb_ctx, seq_ctx, D_MODEL), xs.reshape(nb_lat, seq_lat, D_MODEL),
            jnp.stack(new_k, axis=1), jnp.stack(new_v, axis=1), jnp.stack(new_s, axis=1))
```

```python
import functools
import math

import numpy as np
import jax
import jax.numpy as jnp
from jax import lax
from jax.experimental import pallas as pl
from jax.experimental.pallas import tpu as pltpu

F32 = jnp.float32
BF16 = jnp.bfloat16

D_MODEL = 2048
DEPTH = 2
GRID_W = 64
MIX_W = 512
N_BRANCH = 4
ATT_HEADS = 8
ATT_KV_HEADS = 2
HEAD_DIM = 64
ROPE_THETA = 10000.0
FNET_GW = 128
SSM_HEADS = 8
SSM_HEAD_DIM = 64
SSM_STATE = 128
SSM_CHUNK = 128
SSM_D_INNER = 512
SSM_CONV_CH = 1024
GMLP_CHUNK = 128
D_FF = 4 * D_MODEL
EPS = 1e-6
LAT_LEN = 1024

QKV_W = 768
COL_F = 768
COL_XBC = 1280
COL_Z = 2304
COL_DT = 2816
COL_UV = 2832
GATE_COL0 = 3856
N_IN = GATE_COL0 + N_BRANCH * D_MODEL
_IN_PIECES = ((0, QKV_W), (COL_F, COL_XBC), (COL_XBC, COL_Z), (COL_Z, COL_DT),
              (COL_DT, COL_DT + 128),
              (COL_UV, GATE_COL0))

VMEM_LIMIT = 56 * 1024 * 1024
NEG_BIG = -1e30


def _params(*sem):
    return pltpu.CompilerParams(dimension_semantics=sem, vmem_limit_bytes=VMEM_LIMIT)


def _dot(a, b):
    return jnp.dot(a, b, preferred_element_type=F32)


def _dot_nt(a, b):
    return lax.dot_general(a, b, (((1,), (1,)), ((), ())), preferred_element_type=F32)


def _dot_tn(a, b):
    return lax.dot_general(a, b, (((0,), (0,)), ((), ())), preferred_element_type=F32)


def _sigmoid(x):
    return 1.0 / (1.0 + jnp.exp(-x))


def _split2(x):
    hi = x.astype(BF16)
    lo = (x - hi.astype(F32)).astype(BF16)
    return hi, lo


def _rms_mod(x, sc, sh):
    ms = jnp.mean(x * x, axis=-1, keepdims=True)
    return x * lax.rsqrt(ms + EPS) * (1.0 + sc) + sh


def _mod_kernel(cond_ref, w_ref, b_ref, o_ref):
    c = cond_ref[...]
    s = c * _sigmoid(c)
    o_ref[...] = _dot(s.astype(BF16), w_ref[...].astype(BF16)) + b_ref[...]


def _modulation(cond8, w_mod, b_mod):
    tn = 1024
    n6 = 6 * D_MODEL
    return pl.pallas_call(
        _mod_kernel,
        grid=(DEPTH, n6 // tn),
        in_specs=[
            pl.BlockSpec((8, D_MODEL), lambda l, n: (0, 0)),
            pl.BlockSpec((None, D_MODEL, tn), lambda l, n: (l, 0, n)),
            pl.BlockSpec((None, 1, tn), lambda l, n: (l, 0, n)),
        ],
        out_specs=pl.BlockSpec((None, 8, tn), lambda l, n: (l, 0, n)),
        out_shape=jax.ShapeDtypeStruct((DEPTH, 8, n6), F32),
        compiler_params=_params("arbitrary", "arbitrary"),
        name="modulation",
    )(cond8, w_mod, b_mod.reshape(DEPTH, 1, n6))


def _mod_spec(chunk, row_fn):
    return pl.BlockSpec((None, 1, D_MODEL), lambda m, *_: (row_fn(m), 0, chunk))


def _cast_kernel(w_ref, o_ref):
    o_ref[...] = w_ref[...].astype(o_ref.dtype)


def _cast_w_in(w_in_t):
    rows = 1024
    return pl.pallas_call(
        _cast_kernel,
        grid=(DEPTH, pl.cdiv(N_IN, rows)),
        in_specs=[pl.BlockSpec((None, rows, D_MODEL), lambda l, r: (l, r, 0))],
        out_specs=pl.BlockSpec((None, rows, D_MODEL), lambda l, r: (l, r, 0)),
        out_shape=jax.ShapeDtypeStruct((DEPTH, N_IN, D_MODEL), BF16),
        compiler_params=_params("arbitrary", "arbitrary"),
        name="cast_w_in",
    )(w_in_t)


def _inproj_kernel(x_ref, sc_ref, sh_ref, w_ref, *out_refs):
    hb = _rms_mod(x_ref[...], sc_ref[...], sh_ref[...]).astype(BF16)
    for o_ref, (a, b) in zip(out_refs, _IN_PIECES):
        o_ref[...] = _dot_nt(hb, w_ref[a:b, :])


def _inproj(x, mod3, wt, layer, row_fn):
    t = x.shape[0]
    tm = 256
    assert t % tm == 0
    widths = [b - a for a, b in _IN_PIECES]
    return pl.pallas_call(
        _inproj_kernel,
        grid=(t // tm,),
        in_specs=[
            pl.BlockSpec((tm, D_MODEL), lambda m: (m, 0)),
            _mod_spec(1, lambda m: row_fn(m * tm)),
            _mod_spec(0, lambda m: row_fn(m * tm)),
            pl.BlockSpec((None, GATE_COL0, D_MODEL), lambda m: (layer, 0, 0), pipeline_mode=pl.Buffered(1)),
        ],
        out_specs=[pl.BlockSpec((tm, w), lambda m: (m, 0)) for w in widths],
        out_shape=[jax.ShapeDtypeStruct((t, w), F32) for w in widths],
        compiler_params=_params("arbitrary"),
        name="inproj",
    )(x, mod3, mod3, wt)


def _head_rms(t, gsum, gain):
    t2 = t * t
    hi, lo = _split2(t2)
    ss = _dot(hi, gsum) + _dot(lo, gsum)
    return t * lax.rsqrt(ss * (1.0 / HEAD_DIM) + EPS) * gain


def _rope(t, cos, sin_signed):
    w = t.shape[-1]
    up = pltpu.roll(t, w - 16, 1)
    dn = pltpu.roll(t, 16, 1)
    lane = lax.broadcasted_iota(jnp.int32, t.shape, 1)
    partner = jnp.where((lane & 31) < 16, up, dn)
    return t * cos + partner * sin_signed


def _attend_heads(q, k_scr, v_scr, o_ref):
    qb = (q * (HEAD_DIM ** -0.5)).astype(BF16)
    group = ATT_HEADS // ATT_KV_HEADS
    for h in range(ATT_HEADS):
        j = h // group
        s = _dot_nt(qb[:, h * HEAD_DIM:(h + 1) * HEAD_DIM], k_scr[:, j * HEAD_DIM:(j + 1) * HEAD_DIM])
        m = jnp.max(s, axis=-1, keepdims=True)
        e = jnp.exp(s - m)
        den = jnp.sum(e, axis=-1, keepdims=True)
        o = _dot(e.astype(BF16), v_scr[:, j * HEAD_DIM:(j + 1) * HEAD_DIM]) / den
        o_ref[:, h * HEAD_DIM:(h + 1) * HEAD_DIM] = o.astype(o_ref.dtype)


def _attn_ctx_kernel(q_ref, kv_ref, qg_ref, kg_ref, gsum_ref, o_ref, kout_ref, vout_ref, k_scr, v_scr):
    gsum = gsum_ref[...]
    kv = kv_ref[...]
    k = _head_rms(kv[:, :128], gsum[:128, :128], kg_ref[...])
    v = kv[:, 128:]
    kout_ref[...] = k
    vout_ref[...] = v
    k_scr[...] = k.astype(BF16)
    v_scr[...] = v.astype(BF16)
    q = _head_rms(q_ref[...], gsum, qg_ref[...])
    _attend_heads(q, k_scr, v_scr, o_ref)


def _attn_lat_kernel(q_ref, kv_ref, qg_ref, kg_ref, gsum_ref, cq_ref, sq_ref, ck_ref, sk_ref,
                     cache_k_ref, cache_v_ref, o_ref, k_scr, v_scr):
    gsum = gsum_ref[...]
    past = cache_k_ref.shape[0]

    @pl.when(pl.program_id(1) == 0)
    def _():
        kv = kv_ref[...]
        k = _head_rms(kv[:, :128], gsum[:128, :128], kg_ref[...])
        k = _rope(k, ck_ref[...], sk_ref[...])
        k_scr[:past, :] = cache_k_ref[...].astype(BF16)
        v_scr[:past, :] = cache_v_ref[...].astype(BF16)
        k_scr[past:, :] = k.astype(BF16)
        v_scr[past:, :] = kv[:, 128:].astype(BF16)

    q = _head_rms(q_ref[...], gsum, qg_ref[...])
    q = _rope(q, cq_ref[...], sq_ref[...])
    _attend_heads(q, k_scr, v_scr, o_ref)


def _attention_ctx(qkv, qg, kg, gsum, nb, seq):
    t = qkv.shape[0]
    const = lambda shape: pl.BlockSpec(shape, lambda b: (0, 0))
    return pl.pallas_call(
        _attn_ctx_kernel,
        grid=(nb,),
        in_specs=[
            pl.BlockSpec((seq, 512), lambda b: (b, 0)),
            pl.BlockSpec((seq, 256), lambda b: (b, 2)),
            const((1, 512)), const((1, 128)), const((512, 512)),
        ],
        out_specs=[
            pl.BlockSpec((seq, 512), lambda b: (b, 0)),
            pl.BlockSpec((seq, 128), lambda b: (b, 0)),
            pl.BlockSpec((seq, 128), lambda b: (b, 0)),
        ],
        out_shape=[
            jax.ShapeDtypeStruct((t, 512), BF16),
            jax.ShapeDtypeStruct((t, 128), F32),
            jax.ShapeDtypeStruct((t, 128), F32),
        ],
        scratch_shapes=[pltpu.VMEM((seq, 128), BF16), pltpu.VMEM((seq, 128), BF16)],
        compiler_params=_params("arbitrary"),
        name="attn_ctx",
    )(qkv, qkv, qg, kg, gsum)


def _attention_lat(qkv, qg, kg, gsum, rope_tabs, cache_k, cache_v, nb, seq):
    t = qkv.shape[0]
    qblk = 256
    nq = seq // qblk
    past = cache_k.shape[1]
    cq, sq, ck, sk = rope_tabs
    const = lambda shape: pl.BlockSpec(shape, lambda b, i: (0, 0))
    return pl.pallas_call(
        _attn_lat_kernel,
        grid=(nb, nq),
        in_specs=[
            pl.BlockSpec((qblk, 512), lambda b, i: (b * nq + i, 0)),
            pl.BlockSpec((seq, 256), lambda b, i: (b, 2)),
            const((1, 512)), const((1, 128)), const((512, 512)),
            pl.BlockSpec((qblk, 512), lambda b, i: (i, 0)),
            pl.BlockSpec((qblk, 512), lambda b, i: (i, 0)),
            const((seq, 128)), const((seq, 128)),
            pl.BlockSpec((None, past, 128), lambda b, i: (b, 0, 0)),
            pl.BlockSpec((None, past, 128), lambda b, i: (b, 0, 0)),
        ],
        out_specs=pl.BlockSpec((qblk, 512), lambda b, i: (b * nq + i, 0)),
        out_shape=jax.ShapeDtypeStruct((t, 512), BF16),
        scratch_shapes=[pltpu.VMEM((past + seq, 128), BF16), pltpu.VMEM((past + seq, 128), BF16)],
        compiler_params=_params("arbitrary", "arbitrary"),
        name="attn_lat",
    )(qkv, qkv, qg, kg, gsum, cq, sq, ck, sk, cache_k, cache_v)


def _fourier_kernel(f_ref, bdh_ref, bdl_ref, clh_ref, cll_ref, slh_ref, sll_ref, o_ref, *, scale):
    xh, xl = _split2(f_ref[...])
    bdh = bdh_ref[...]
    bdl = bdl_ref[...]
    yc, ys = [], []
    for g in range(MIX_W // FNET_GW):
        gl = slice(g * FNET_GW, (g + 1) * FNET_GW)
        y = _dot(xh[:, gl], bdh) + _dot(xl[:, gl], bdh) + _dot(xh[:, gl], bdl)
        yc.append(y[:, :FNET_GW])
        ys.append(y[:, FNET_GW:])
    ych, ycl = _split2(jnp.concatenate(yc, axis=1))
    ysh, ysl = _split2(jnp.concatenate(ys, axis=1))
    clh = clh_ref[...]
    slh = slh_ref[...]
    re = _dot(clh, ych) + _dot(cll_ref[...], ych) + _dot(clh, ycl)
    im = _dot(slh, ysh) + _dot(sll_ref[...], ysh) + _dot(slh, ysl)
    o_ref[...] = ((re - im) * scale).astype(o_ref.dtype)


def _dft_consts(seq):
    def cs(n):
        k = np.arange(n, dtype=np.int64)
        ang = 2.0 * np.pi * ((k[:, None] * k[None, :]) % n).astype(np.float64) / n
        return np.cos(ang), np.sin(ang)

    bd = np.concatenate(cs(FNET_GW), axis=1)
    cl, sl_ = cs(seq)

    def hl(a):
        a32 = jnp.asarray(a, F32)
        hi = a32.astype(BF16)
        return hi, (a32 - hi.astype(F32)).astype(BF16)

    return hl(bd) + hl(cl) + hl(sl_)


def _fourier(f, nb, seq):
    t = f.shape[0]
    consts = _dft_consts(seq)
    const = lambda a: pl.BlockSpec(a.shape, lambda b: (0, 0))
    return pl.pallas_call(
        functools.partial(_fourier_kernel, scale=1.0 / math.sqrt(seq * FNET_GW)),
        grid=(nb,),
        in_specs=[pl.BlockSpec((seq, MIX_W), lambda b: (b, 0))] + [const(a) for a in consts],
        out_specs=pl.BlockSpec((seq, MIX_W), lambda b: (b, 0)),
        out_shape=jax.ShapeDtypeStruct((t, MIX_W), BF16),
        compiler_params=_params("arbitrary"),
        name="fourier",
    )(f, *consts)


def _ssd_kernel(*refs, seq, has_h0, emit_state):
    (xbc_ref, z_ref, dt_ref, cw_ref, cb_ref, bias_ref, alog_ref, dskip_ref, ng_ref, tri_ref) = refs[:10]
    rest = list(refs[10:])
    h0_ref = rest.pop(0) if has_h0 else None
    o_ref = rest.pop(0)
    sfin_ref = rest.pop(0) if emit_state else None
    xt_scr, bc_scr, tab, tabt, yt_scr, s_scr = rest
    nc = seq // SSM_CHUNK
    q = SSM_CHUNK

    lane = lax.broadcasted_iota(jnp.int32, (q, 128), 1)
    ii = lax.broadcasted_iota(jnp.int32, (q, q), 0)
    jj = lax.broadcasted_iota(jnp.int32, (q, q), 1)
    below = ii > jj
    above = ii < jj
    row = lax.broadcasted_iota(jnp.int32, (q, 1), 0)
    cw = cw_ref[...]
    amul = jnp.where(lane < 16, 1.0, jnp.where(lane < 48, -jnp.exp(alog_ref[...]), 0.0))
    tri = tri_ref[...]

    if has_h0:
        s_scr[...] = h0_ref[...]
    else:
        s_scr[...] = jnp.zeros_like(s_scr)

    def prep(c, carry):
        r0 = pl.multiple_of(c * q, q)
        cur = xbc_ref[pl.ds(r0, q), :]
        prev8 = xbc_ref[pl.ds(pl.multiple_of(jnp.maximum(r0 - 8, 0), 8), 8), :]
        next8 = xbc_ref[pl.ds(pl.multiple_of(jnp.minimum(r0 + q, seq - 8), 8), 8), :]
        prev_row = jnp.where(c > 0, prev8[7:8, :], 0.0)
        next_row = jnp.where(c < nc - 1, next8[0:1, :], 0.0)
        up = jnp.where(row == 0, prev_row, pltpu.roll(cur, 1, 0))
        dn = jnp.where(row == q - 1, next_row, pltpu.roll(cur, q - 1, 0))
        conv = up * cw[0:1, :] + cur * cw[1:2, :] + dn * cw[2:3, :] + cb_ref[...]
        act = conv * _sigmoid(conv)
        xt_scr[c] = act[:, :SSM_D_INNER].T
        bc_scr[pl.ds(r0, q), :] = act[:, SSM_D_INNER:].astype(BF16)

        raw = dt_ref[pl.ds(r0, q), :]
        rep = jnp.where(lane < 16, raw, jnp.where(lane < 32, pltpu.roll(raw, 16, 1), pltpu.roll(raw, 32, 1)))
        v = rep + bias_ref[...]
        sp = jnp.maximum(v, 0.0) + jnp.log1p(jnp.exp(-jnp.abs(v)))
        t = sp * amul
        hi = t.astype(BF16)
        r1 = t - hi.astype(F32)
        mid = r1.astype(BF16)
        lo = (r1 - mid.astype(F32)).astype(BF16)
        cum = _dot(tri, hi) + _dot(tri, mid) + _dot(tri, lo)
        tc = jnp.where(lane < 32, t, cum)
        tab[pl.ds(r0, q), :] = tc
        tabt[c] = tc.T
        return carry

    lax.fori_loop(0, nc, prep, 0)

    def pass_fwd(c, carry):
        r0 = pl.multiple_of(c * q, q)
        pc = tab[pl.ds(r0, q), :]
        pt = tabt[c]
        for g in range(2):
            bg = bc_scr[pl.ds(r0, q), 128 * g:128 * (g + 1)]
            cg = bc_scr[pl.ds(r0, q), 256 + 128 * g:256 + 128 * (g + 1)]
            cb = _dot_nt(cg, bg)
            y_in = _dot_nt(s_scr[0, 256 * g:256 * (g + 1), :].astype(BF16), cg)
            for hh in range(4):
                h = 4 * g + hh
                acf_row = pt[32 + h:33 + h, :]
                eb_col = pc[:, 40 + h:41 + h] - pc[:, 24 + h:25 + h]
                eb_row = pt[40 + h:41 + h, :] - pt[24 + h:25 + h, :]
                dtf_row = pt[h:h + 1, :]
                dtb_row = pt[8 + h:9 + h, :]
                seg = jnp.where(below, pc[:, 32 + h:33 + h] - acf_row, jnp.where(above, eb_row - eb_col, 0.0))
                wgt = jnp.where(below, dtf_row, jnp.where(above, dtb_row, dtf_row + dtb_row))
                mp = cb * (jnp.exp(seg) * wgt)
                xth = xt_scr[c, 64 * h:64 * (h + 1), :]
                yd = _dot_nt(xth.astype(BF16), mp.astype(BF16))
                yo = y_in[64 * hh:64 * (hh + 1), :] * jnp.exp(acf_row)
                yt_scr[c, 64 * h:64 * (h + 1), :] = yd + yo + dskip_ref[h] * xth
                tot = pt[32 + h:33 + h, q - 1:q]
                wf_row = dtf_row * jnp.exp(tot - acf_row)
                st = _dot((xth * wf_row).astype(BF16), bg)
                s_scr[0, 64 * h:64 * (h + 1), :] = s_scr[0, 64 * h:64 * (h + 1), :] * jnp.exp(tot) + st
        return carry

    lax.fori_loop(0, nc, pass_fwd, 0)

    def pass_bwd(ci, carry):
        c = nc - 1 - ci
        r0 = pl.multiple_of(c * q, q)
        pt = tabt[c]
        for g in range(2):
            bg = bc_scr[pl.ds(r0, q), 128 * g:128 * (g + 1)]
            cg = bc_scr[pl.ds(r0, q), 256 + 128 * g:256 + 128 * (g + 1)]
            y_in = _dot_nt(s_scr[1, 256 * g:256 * (g + 1), :].astype(BF16), cg)
            for hh in range(4):
                h = 4 * g + hh
                eb_row = pt[40 + h:41 + h, :] - pt[24 + h:25 + h, :]
                tot = pt[40 + h:41 + h, q - 1:q]
                xth = xt_scr[c, 64 * h:64 * (h + 1), :]
                yt_scr[c, 64 * h:64 * (h + 1), :] += y_in[64 * hh:64 * (hh + 1), :] * jnp.exp(tot - eb_row)
                wb_row = pt[8 + h:9 + h, :] * jnp.exp(eb_row)
                st = _dot((xth * wb_row).astype(BF16), bg)
                s_scr[1, 64 * h:64 * (h + 1), :] = s_scr[1, 64 * h:64 * (h + 1), :] * jnp.exp(tot) + st
        zc = z_ref[pl.ds(r0, q), :]
        yg = yt_scr[c].T * (zc * _sigmoid(zc))
        ms = jnp.mean(yg * yg, axis=-1, keepdims=True)
        o_ref[pl.ds(r0, q), :] = (yg * lax.rsqrt(ms + EPS) * ng_ref[...]).astype(o_ref.dtype)
        return carry

    lax.fori_loop(0, nc, pass_bwd, 0)

    if emit_state:
        sfin_ref[...] = s_scr[...]


def _ssd(xbc, z, dt, conv_w, conv_b, bias128, alog128, dskip, norm_g, tri, h0, nb, seq, emit_state):
    t = xbc.shape[0]
    has_h0 = h0 is not None
    const = lambda shape: pl.BlockSpec(shape, lambda b: (0,) * len(shape))
    in_specs = [
        pl.BlockSpec((seq, SSM_CONV_CH), lambda b: (b, 0)),
        pl.BlockSpec((seq, SSM_D_INNER), lambda b: (b, 0)),
        pl.BlockSpec((seq, 128), lambda b: (b, 0)),
        const((3, SSM_CONV_CH)), const((1, SSM_CONV_CH)), const((1, 128)), const((1, 128)),
        pl.BlockSpec(memory_space=pltpu.SMEM), const((1, SSM_D_INNER)), const((128, 128)),
    ]
    args = [xbc, z, dt, conv_w, conv_b, bias128, alog128, dskip, norm_g, tri]
    if has_h0:
        in_specs.append(pl.BlockSpec((None, 2, 512, 128), lambda b: (b, 0, 0, 0)))
        args.append(h0)
    out_specs = [pl.BlockSpec((seq, SSM_D_INNER), lambda b: (b, 0))]
    out_shape = [jax.ShapeDtypeStruct((t, SSM_D_INNER), BF16)]
    if emit_state:
        out_specs.append(pl.BlockSpec((None, 2, 512, 128), lambda b: (b, 0, 0, 0)))
        out_shape.append(jax.ShapeDtypeStruct((nb, 2, 512, 128), F32))
    return pl.pallas_call(
        functools.partial(_ssd_kernel, seq=seq, has_h0=has_h0, emit_state=emit_state),
        grid=(nb,),
        in_specs=in_specs,
        out_specs=out_specs,
        out_shape=out_shape,
        scratch_shapes=[
            pltpu.VMEM((seq // SSM_CHUNK, SSM_D_INNER, SSM_CHUNK), F32),
            pltpu.VMEM((seq, 512), BF16),
            pltpu.VMEM((seq, 128), F32),
            pltpu.VMEM((seq // SSM_CHUNK, 128, SSM_CHUNK), F32),
            pltpu.VMEM((seq // SSM_CHUNK, SSM_D_INNER, SSM_CHUNK), F32),
            pltpu.VMEM((2, 512, 128), F32),
        ],
        compiler_params=_params("arbitrary"),
        name="ssd",
    )(*args)


def _gmlp_kernel(uv_ref, g_ref, ws_ref, bs_ref, o_ref):
    rows = uv_ref.shape[0]
    c0 = math.sqrt(2.0 / math.pi)
    for c in range(rows // GMLP_CHUNK):
        sl = slice(c * GMLP_CHUNK, (c + 1) * GMLP_CHUNK)
        x = uv_ref[sl, :]
        ge =x * (0.5 * (1.0 + jnp.tanh(c0 * (x + 0.044715 * (x * x * x)))))
        u = ge[:, :MIX_W]
        v = ge[:, MIX_W:]
        v = v * lax.rsqrt(jnp.mean(v * v, axis=-1, keepdims=True) + EPS) * g_ref[...]
        for g in range(4):
            gl = slice(g * 128, (g + 1) * 128)
            mixed = _dot(ws_ref[g], v[:, gl].astype(BF16)) + bs_ref[:, g:g + 1]
            o_ref[sl, gl] = (u[:, gl] * mixed).astype(o_ref.dtype)


def _gmlp(uv, norm_g, ws, bs_t):
    t = uv.shape[0]
    tm = 256
    return pl.pallas_call(
        _gmlp_kernel,
        grid=(t // tm,),
        in_specs=[
            pl.BlockSpec((tm, 2 * MIX_W), lambda m: (m, 0)),
            pl.BlockSpec((1, MIX_W), lambda m: (0, 0)),
            pl.BlockSpec((4, 128, 128), lambda m: (0, 0, 0)),
            pl.BlockSpec((128, 4), lambda m: (0, 0)),
        ],
        out_specs=pl.BlockSpec((tm, MIX_W), lambda m: (m, 0)),
        out_shape=jax.ShapeDtypeStruct((t, MIX_W), BF16),
        compiler_params=_params("arbitrary"),
        name="gmlp",
    )(uv, norm_g, ws, bs_t)


def _merge_kernel(x_ref, sc_ref, sh_ref, g_ref, b0, b1, b2, b3, wg0, wg1, wg2, wg3, wb_ref, wo_ref,
                  o_ref, h_scr):
    n = pl.program_id(1)

    @pl.when(n == 0)
    def _():
        h_scr[...] = _rms_mod(x_ref[...], sc_ref[...], sh_ref[...]).astype(BF16)
        o_ref[...] = jnp.zeros_like(o_ref)

    h = h_scr[...]
    merged = None
    for k, (b_ref, wg_ref) in enumerate(((b0, wg0), (b1, wg1), (b2, wg2), (b3, wg3))):
        gate = _sigmoid(_dot_nt(h, wg_ref[0]))
        term = gate * _dot(b_ref[...], wb_ref[k].astype(BF16))
        merged = term if merged is None else merged + term
    o_ref[...] += _dot(merged.astype(BF16), wo_ref[...].astype(BF16))

    @pl.when(n == pl.num_programs(1) - 1)
    def _():
        o_ref[...] = x_ref[...] + g_ref[...] * o_ref[...]


def _merge(x, mod3, branches, wt, w_branch, w_out, layer, row_fn):
    t = x.shape[0]
    tm, tn = 512, 256
    assert t % tm == 0
    nn = D_MODEL // tn
    rf = lambda m: row_fn(m * tm)
    gate_spec = lambda k: pl.BlockSpec(
        (pl.Element(1), pl.Element(tn), pl.Element(D_MODEL)),
        lambda m, n: (layer, pl.multiple_of(GATE_COL0 + k * D_MODEL + n * tn, 16), 0))
    return pl.pallas_call(
        _merge_kernel,
        grid=(t // tm, nn),
        in_specs=[
            pl.BlockSpec((tm, D_MODEL), lambda m, n: (m, 0)),
            _mod_spec(1, rf), _mod_spec(0, rf), _mod_spec(2, rf),
        ] + [pl.BlockSpec((tm, MIX_W), lambda m, n: (m, 0))] * 4
          + [gate_spec(k) for k in range(4)] + [
            pl.BlockSpec((None, N_BRANCH, MIX_W, tn), lambda m, n: (layer, 0, 0, n)),
            pl.BlockSpec((None, tn, D_MODEL), lambda m, n: (layer, n, 0)),
        ],
        out_specs=pl.BlockSpec((tm, D_MODEL), lambda m, n: (m, 0)),
        out_shape=jax.ShapeDtypeStruct((t, D_MODEL), F32),
        scratch_shapes=[pltpu.VMEM((tm, D_MODEL), BF16)],
        compiler_params=_params("arbitrary", "arbitrary"),
        name="merge",
    )(x, mod3, mod3, mod3, *branches, wt, wt, wt, wt, w_branch, w_out)


def _ffn_kernel(x_ref, sc_ref, sh_ref, g_ref, w1_ref, w2_ref, o_ref, h_scr):
    n = pl.program_id(1)

    @pl.when(n == 0)
    def _():
        h_scr[...] = _rms_mod(x_ref[...], sc_ref[...], sh_ref[...]).astype(BF16)
        o_ref[...] = jnp.zeros_like(o_ref)

    a = jnp.maximum(_dot(h_scr[...], w1_ref[...].astype(BF16)), 0.0)
    o_ref[...] += _dot((a * a).astype(BF16), w2_ref[...].astype(BF16))

    @pl.when(n == pl.num_programs(1) - 1)
    def _():
        o_ref[...] = x_ref[...] + g_ref[...] * o_ref[...]


def _ffn(x, mod3, w1, w2, layer, row_fn):
    t = x.shape[0]
    tm, tf = 1024, 512
    assert t % tm == 0
    rf = lambda m: row_fn(m * tm)
    return pl.pallas_call(
        _ffn_kernel,
        grid=(t // tm, D_FF // tf),
        in_specs=[
            pl.BlockSpec((tm, D_MODEL), lambda m, n: (m, 0), pipeline_mode=pl.Buffered(1)),
            _mod_spec(4, rf), _mod_spec(3, rf), _mod_spec(5, rf),
            pl.BlockSpec((None, D_MODEL, tf), lambda m, n: (layer, 0, n)),
            pl.BlockSpec((None, tf, D_MODEL), lambda m, n: (layer, n, 0)),
        ],
        out_specs=pl.BlockSpec((tm, D_MODEL), lambda m, n: (m, 0)),
        out_shape=jax.ShapeDtypeStruct((t, D_MODEL), F32),
        scratch_shapes=[pltpu.VMEM((tm, D_MODEL), BF16)],
        compiler_params=_params("arbitrary", "arbitrary"),
        name="ffn",
    )(x, mod3, mod3, mod3, w1, w2)


def _rope_tables(seq):
    half = 16
    freqs = ROPE_THETA ** (-jnp.arange(half, dtype=F32) / half)
    n_rows = seq // GRID_W
    rows = jnp.repeat(jnp.arange(n_rows), GRID_W).astype(F32)
    cols = jnp.tile(jnp.arange(GRID_W), n_rows).astype(F32)
    ar = rows[:, None] * freqs[None, :]
    ac = cols[:, None] * freqs[None, :]
    cos64 = jnp.concatenate([jnp.cos(ar), jnp.cos(ar), jnp.cos(ac), jnp.cos(ac)], axis=-1)
    sin64 = jnp.concatenate([-jnp.sin(ar), jnp.sin(ar), -jnp.sin(ac), jnp.sin(ac)], axis=-1)
    return (jnp.tile(cos64, (1, ATT_HEADS)), jnp.tile(sin64, (1, ATT_HEADS)),
            jnp.tile(cos64, (1, ATT_KV_HEADS)), jnp.tile(sin64, (1, ATT_KV_HEADS)))


def _layer_path(x, mod3, lw, row_fn, nb, seq, lat):
    (wt, qg, kg, gsum, conv_w, conv_b, bias128, alog128, dskip, ssm_g, tri, gmlp_g, ws, bs_t,
     w_branch, w_out, w_ff1, w_ff2, layer) = lw
    qkv, f_in, xbc, z, dt, uv = _inproj(x, mod3, wt, layer, row_fn)
    if lat is None:
        o_att, k_new, v_new = _attention_ctx(qkv, qg, kg, gsum, nb, seq)
        h0 = None
    else:
        rope_tabs, cache_k, cache_v, h0 = lat
        o_att = _attention_lat(qkv, qg, kg, gsum, rope_tabs, cache_k, cache_v, nb, seq)
        k_new = v_new = None
    o_fnet = _fourier(f_in, nb, seq)
    ssd_out = _ssd(xbc, z, dt, conv_w, conv_b, bias128, alog128, dskip, ssm_g, tri, h0, nb, seq,
                   emit_state=lat is None)
    o_ssm = ssd_out[0]
    s_fin = ssd_out[1] if lat is None else None
    o_gmlp = _gmlp(uv, gmlp_g, ws, bs_t)
    x = _merge(x, mod3, (o_att, o_fnet, o_ssm, o_gmlp), wt, w_branch, w_out, layer, row_fn)
    x = _ffn(x, mod3, w_ff1, w_ff2, layer, row_fn)
    return x, k_new, v_new, s_fin


def kernel(x_prompt, x_sample, c, cache_k, cache_v, state_ssm, c_ctx, w_mod, b_mod, w_in, q_norm_g, k_norm_g,
           conv_w, conv_b, a_log, dt_bias, d_skip, ssm_norm_g, gmlp_norm_g, w_spatial, b_spatial, w_branch,
           w_out, w_ff1, w_ff2):
    nb_ctx, seq_ctx, _ = x_prompt.shape
    nb_lat, seq_lat, _ = x_sample.shape
    assert seq_lat == LAT_LEN and 1 + nb_lat <= 8

    cond8 = jnp.zeros((8, D_MODEL), F32).at[0].set(c_ctx).at[1:1 + nb_lat].set(c)
    mod = _modulation(cond8, w_mod, b_mod)

    gsum = jnp.asarray(np.kron(np.eye(ATT_HEADS), np.ones((HEAD_DIM, HEAD_DIM))), BF16)
    tri = jnp.asarray(np.tril(np.ones((SSM_CHUNK, SSM_CHUNK))), BF16)
    rope_tabs = _rope_tables(seq_lat)

    xp = x_prompt.reshape(nb_ctx * seq_ctx, D_MODEL)
    xs = x_sample.reshape(nb_lat * seq_lat, D_MODEL)
    ctx_row = lambda tok: 0
    lat_row = lambda tok: 1 + tok // LAT_LEN

    wt = _cast_w_in(jnp.swapaxes(w_in, 1, 2))

    new_k, new_v, new_s = [], [], []
    for l in range(DEPTH):
        zeros80 = jnp.zeros((128 - 48,), F32)
        bias128 = jnp.concatenate([jnp.tile(dt_bias[l].reshape(16), 3), zeros80]).reshape(1, 128)
        alog128 = jnp.concatenate([jnp.zeros((16,), F32), jnp.tile(a_log[l].reshape(16), 2), zeros80]).reshape(1, 128)
        lw = (
            wt,
            jnp.tile(q_norm_g[l], ATT_HEADS).reshape(1, 512), jnp.tile(k_norm_g[l], ATT_KV_HEADS).reshape(1, 128),
            gsum, conv_w[l], conv_b[l].reshape(1, SSM_CONV_CH), bias128, alog128,
            d_skip[l], ssm_norm_g[l].reshape(1, SSM_D_INNER), tri,
            gmlp_norm_g[l].reshape(1, MIX_W), w_spatial[l].astype(BF16), b_spatial[l].T,
            w_branch, w_out, w_ff1, w_ff2, l,
        )
        mod3 = mod[l].reshape(8, 1, 6 * D_MODEL)
        xp, k_l, v_l, s_l = _layer_path(xp, mod3, lw, ctx_row, nb_ctx, seq_ctx, None)
        new_k.append(k_l.reshape(nb_ctx, seq_ctx, ATT_KV_HEADS, HEAD_DIM))
        new_v.append(v_l.reshape(nb_ctx, seq_ctx, ATT_KV_HEADS, HEAD_DIM))
        new_s.append(s_l.reshape(nb_ctx, 2, SSM_HEADS, SSM_HEAD_DIM, SSM_STATE))
        lat = (rope_tabs,
               cache_k[:, l].reshape(nb_lat, -1, ATT_KV_HEADS * HEAD_DIM),
               cache_v[:, l].reshape(nb_lat, -1, ATT_KV_HEADS * HEAD_DIM),
               state_ssm[:, l].reshape(nb_lat, 2, SSM_HEADS * SSM_HEAD_DIM, SSM_STATE))
        xs, _, _, _ = _layer_path(xs, mod3, lw, lat_row, nb_lat, seq_lat, lat)

    return (xp.reshape(nb_ctx, seq_ctx, D_MODEL), xs.reshape(nb_lat, seq_lat, D_MODEL),
            jnp.stack(new_k, axis=1), jnp.stack(new_v, axis=1), jnp.stack(new_s, axis=1))
```

```python
import functools
import math

import numpy as np
import jax
import jax.numpy as jnp
from jax import lax
from jax.experimental import pallas as pl
from jax.experimental.pallas import tpu as pltpu

F32 = jnp.float32
BF16 = jnp.bfloat16

D_MODEL = 2048
DEPTH = 2
GRID_W = 64
MIX_W = 512
N_BRANCH = 4
ATT_HEADS = 8
ATT_KV_HEADS = 2
HEAD_DIM = 64
ROPE_THETA = 10000.0
FNET_GW = 128
SSM_HEADS = 8
SSM_HEAD_DIM = 64
SSM_STATE = 128
SSM_CHUNK = 128
SSM_D_INNER = 512
SSM_CONV_CH = 1024
GMLP_CHUNK = 128
D_FF = 4 * D_MODEL
EPS = 1e-6
LAT_LEN = 1024

QKV_W = 768
COL_F = 768
COL_XBC = 1280
COL_Z = 2304
COL_DT = 2816
COL_UV = 2832
GATE_COL0 = 3856
N_IN = GATE_COL0 + N_BRANCH * D_MODEL
_IN_PIECES = ((0, QKV_W), (COL_F, COL_XBC), (COL_XBC, COL_Z), (COL_Z, COL_DT),
              (COL_DT, COL_DT + 128),
              (COL_UV, GATE_COL0))

VMEM_LIMIT = 56 * 1024 * 1024
NEG_BIG = -1e30


def _params(*sem):
    return pltpu.CompilerParams(dimension_semantics=sem, vmem_limit_bytes=VMEM_LIMIT)


def _dot(a, b):
    return jnp.dot(a, b, preferred_element_type=F32)


def _dot_nt(a, b):
    return lax.dot_general(a, b, (((1,), (1,)), ((), ())), preferred_element_type=F32)


def _dot_tn(a, b):
    return lax.dot_general(a, b, (((0,), (0,)), ((), ())), preferred_element_type=F32)


def _sigmoid(x):
    return 1.0 / (1.0 + jnp.exp(-x))


def _split2(x):
    hi = x.astype(BF16)
    lo = (x - hi.astype(F32)).astype(BF16)
    return hi, lo


def _rms_mod(x, sc, sh):
    ms = jnp.mean(x * x, axis=-1, keepdims=True)
    return x * lax.rsqrt(ms + EPS) * (1.0 + sc) + sh


def _mod_kernel(cond_ref, w_ref, b_ref, o_ref):
    c = cond_ref[...]
    s = c * _sigmoid(c)
    o_ref[...] = _dot(s.astype(BF16), w_ref[...].astype(BF16)) + b_ref[...]


def _modulation(cond8, w_mod, b_mod):
    tn = 1024
    n6 = 6 * D_MODEL
    return pl.pallas_call(
        _mod_kernel,
        grid=(DEPTH, n6 // tn),
        in_specs=[
            pl.BlockSpec((8, D_MODEL), lambda l, n: (0, 0)),
            pl.BlockSpec((None, D_MODEL, tn), lambda l, n: (l, 0, n)),
            pl.BlockSpec((None, 1, tn), lambda l, n: (l, 0, n)),
        ],
        out_specs=pl.BlockSpec((None, 8, tn), lambda l, n: (l, 0, n)),
        out_shape=jax.ShapeDtypeStruct((DEPTH, 8, n6), F32),
        compiler_params=_params("arbitrary", "arbitrary"),
        name="modulation",
    )(cond8, w_mod, b_mod.reshape(DEPTH, 1, n6))


def _mod_spec(chunk, row_fn):
    return pl.BlockSpec((None, 1, D_MODEL), lambda m, *_: (row_fn(m), 0, chunk))


def _cast_kernel(w_ref, o_ref):
    o_ref[...] = w_ref[...].astype(o_ref.dtype)


def _cast_w_in(w_in_t):
    rows = 1024
    return pl.pallas_call(
        _cast_kernel,
        grid=(DEPTH, pl.cdiv(N_IN, rows)),
        in_specs=[pl.BlockSpec((None, rows, D_MODEL), lambda l, r: (l, r, 0))],
        out_specs=pl.BlockSpec((None, rows, D_MODEL), lambda l, r: (l, r, 0)),
        out_shape=jax.ShapeDtypeStruct((DEPTH, N_IN, D_MODEL), BF16),
        compiler_params=_params("arbitrary", "arbitrary"),
        name="cast_w_in",
    )(w_in_t)


def _inproj_kernel(x_ref, sc_ref, sh_ref, w_ref, h_ref, *out_refs):
    hb = _rms_mod(x_ref[...], sc_ref[...], sh_ref[...]).astype(BF16)
    h_ref[...] = hb
    for o_ref, (a, b) in zip(out_refs, _IN_PIECES):
        o_ref[...] = _dot_nt(hb, w_ref[a:b, :])


def _inproj(x, mod3, wt, layer, row_fn):
    t = x.shape[0]
    tm = 256
    assert t % tm == 0
    widths = [b - a for a, b in _IN_PIECES]
    return pl.pallas_call(
        _inproj_kernel,
        grid=(t // tm,),
        in_specs=[
            pl.BlockSpec((tm, D_MODEL), lambda m: (m, 0)),
            _mod_spec(1, lambda m: row_fn(m * tm)),
            _mod_spec(0, lambda m: row_fn(m * tm)),
            pl.BlockSpec((None, GATE_COL0, D_MODEL), lambda m: (layer, 0, 0), pipeline_mode=pl.Buffered(1)),
        ],
        out_specs=[pl.BlockSpec((tm, D_MODEL), lambda m: (m, 0))]
                  + [pl.BlockSpec((tm, w), lambda m: (m, 0)) for w in widths],
        out_shape=[jax.ShapeDtypeStruct((t, D_MODEL), BF16)]
                  + [jax.ShapeDtypeStruct((t, w), F32) for w in widths],
        compiler_params=_params("arbitrary"),
        name="inproj",
    )(x, mod3, mod3, wt)


def _head_rms(t, gsum, gain):
    t2 = t * t
    hi, lo = _split2(t2)
    ss = _dot(hi, gsum) + _dot(lo, gsum)
    return t * lax.rsqrt(ss * (1.0 / HEAD_DIM) + EPS) * gain


def _rope(t, cos, sin_signed):
    w = t.shape[-1]
    up = pltpu.roll(t, w - 16, 1)
    dn = pltpu.roll(t, 16, 1)
    lane = lax.broadcasted_iota(jnp.int32, t.shape, 1)
    partner = jnp.where((lane & 31) < 16, up, dn)
    return t * cos + partner * sin_signed


ATT_GROUP = ATT_HEADS // ATT_KV_HEADS
GROUP_W = ATT_GROUP * HEAD_DIM


def _store_group_keys(k, kbd_scr, row0):
    s = k.shape[0]
    k2 = jnp.concatenate([k, k], axis=1)
    k2r = pltpu.roll(k2, HEAD_DIM, 1)
    lane = lax.broadcasted_iota(jnp.int32, k2.shape, 1)
    first = (lane & (2 * HEAD_DIM - 1)) < HEAD_DIM
    for j, tiled in enumerate((jnp.where(first, k2, k2r), jnp.where(first, k2r, k2))):
        for hh in range(ATT_GROUP):
            slab = jnp.where(lax.shift_right_logical(lane, 6) == hh, tiled, 0.0)
            kbd_scr[j, hh, row0:row0 + s, :] = slab.astype(kbd_scr.dtype)


def _attend_heads(q, kbd_scr, vt_scr, o_ref):
    qb = (q * (HEAD_DIM ** -0.5 * math.log2(math.e))).astype(BF16)
    s_len = kbd_scr.shape[2]
    for j in range(ATT_KV_HEADS):
        kbd = kbd_scr[j].reshape(ATT_GROUP * s_len, GROUP_W)
        s_all = _dot_nt(kbd, qb[:, j * GROUP_W:(j + 1) * GROUP_W])
        outs = []
        for hh in range(ATT_GROUP):
            s = s_all[hh * s_len:(hh + 1) * s_len, :]
            m = jnp.max(s, axis=0, keepdims=True)
            e = jnp.exp2(s - m)
            den = jnp.sum(e, axis=0, keepdims=True)
            outs.append(_dot(vt_scr[j * HEAD_DIM:(j + 1) * HEAD_DIM, :], e.astype(BF16)) / den)
        o_ref[:, j * GROUP_W:(j + 1) * GROUP_W] = jnp.concatenate(outs, axis=0).T.astype(o_ref.dtype)


def _attn_ctx_kernel(q_ref, kv_ref, qg_ref, kg_ref, gsum_ref, o_ref, kout_ref, vout_ref, kbd_scr, vt_scr):
    gsum = gsum_ref[...]
    kv = kv_ref[...]
    k = _head_rms(kv[:, :128], gsum[:128, :128], kg_ref[...])
    v = kv[:, 128:]
    kout_ref[...] = k
    vout_ref[...] = v
    _store_group_keys(k, kbd_scr, 0)
    vt_scr[...] = v.T.astype(BF16)
    q = _head_rms(q_ref[...], gsum, qg_ref[...])
    _attend_heads(q, kbd_scr, vt_scr, o_ref)


def _attn_lat_kernel(q_ref, kv_ref, qg_ref, kg_ref, gsum_ref, cq_ref, sq_ref, ck_ref, sk_ref,
                     cache_k_ref, cache_v_ref, o_ref, kbd_scr, vt_scr):
    gsum = gsum_ref[...]
    past = cache_k_ref.shape[0]

    @pl.when(pl.program_id(1) == 0)
    def _():
        kv = kv_ref[...]
        k = _head_rms(kv[:, :128], gsum[:128, :128], kg_ref[...])
        k = _rope(k, ck_ref[...], sk_ref[...])
        _store_group_keys(cache_k_ref[...], kbd_scr, 0)
        _store_group_keys(k, kbd_scr, past)
        vt_scr[:, :past] = cache_v_ref[...].T.astype(BF16)
        vt_scr[:, past:] = kv[:, 128:].T.astype(BF16)

    q = _head_rms(q_ref[...], gsum, qg_ref[...])
    q = _rope(q, cq_ref[...], sq_ref[...])
    _attend_heads(q, kbd_scr, vt_scr, o_ref)


def _attention_ctx(qkv, qg, kg, gsum, nb, seq):
    t = qkv.shape[0]
    const = lambda shape: pl.BlockSpec(shape, lambda b: (0, 0))
    return pl.pallas_call(
        _attn_ctx_kernel,
        grid=(nb,),
        in_specs=[
            pl.BlockSpec((seq, 512), lambda b: (b, 0)),
            pl.BlockSpec((seq, 256), lambda b: (b, 2)),
            const((1, 512)), const((1, 128)), const((512, 512)),
        ],
        out_specs=[
            pl.BlockSpec((seq, 512), lambda b: (b, 0)),
            pl.BlockSpec((seq, 128), lambda b: (b, 0)),
            pl.BlockSpec((seq, 128), lambda b: (b, 0)),
        ],
        out_shape=[
            jax.ShapeDtypeStruct((t, 512), BF16),
            jax.ShapeDtypeStruct((t, 128), F32),
            jax.ShapeDtypeStruct((t, 128), F32),
        ],
        scratch_shapes=[pltpu.VMEM((ATT_KV_HEADS, ATT_GROUP, seq, GROUP_W), BF16), pltpu.VMEM((128, seq), BF16)],
        compiler_params=_params("arbitrary"),
        name="attn_ctx",
    )(qkv, qkv, qg, kg, gsum)


def _attention_lat(qkv, qg, kg, gsum, rope_tabs, cache_k, cache_v, nb, seq):
    t = qkv.shape[0]
    qblk = 256
    nq = seq // qblk
    past = cache_k.shape[1]
    cq, sq, ck, sk = rope_tabs
    const = lambda shape: pl.BlockSpec(shape, lambda b, i: (0, 0))
    return pl.pallas_call(
        _attn_lat_kernel,
        grid=(nb, nq),
        in_specs=[
            pl.BlockSpec((qblk, 512), lambda b, i: (b * nq + i, 0)),
            pl.BlockSpec((seq, 256), lambda b, i: (b, 2)),
            const((1, 512)), const((1, 128)), const((512, 512)),
            pl.BlockSpec((qblk, 512), lambda b, i: (i, 0)),
            pl.BlockSpec((qblk, 512), lambda b, i: (i, 0)),
            const((seq, 128)), const((seq, 128)),
            pl.BlockSpec((None, past, 128), lambda b, i: (b, 0, 0)),
            pl.BlockSpec((None, past, 128), lambda b, i: (b, 0, 0)),
        ],
        out_specs=pl.BlockSpec((qblk, 512), lambda b, i: (b * nq + i, 0)),
        out_shape=jax.ShapeDtypeStruct((t, 512), BF16),
        scratch_shapes=[pltpu.VMEM((ATT_KV_HEADS, ATT_GROUP, past + seq, GROUP_W), BF16),
                        pltpu.VMEM((128, past + seq), BF16)],
        compiler_params=_params("arbitrary", "arbitrary"),
        name="attn_lat",
    )(qkv, qkv, qg, kg, gsum, cq, sq, ck, sk, cache_k, cache_v)


def _fourier_kernel(f_ref, bdh_ref, bdl_ref, clh_ref, cll_ref, slh_ref, sll_ref, o_ref, *, scale):
    xh, xl = _split2(f_ref[...])
    bdh = bdh_ref[...]
    bdl = bdl_ref[...]
    yc, ys = [], []
    for g in range(MIX_W // FNET_GW):
        gl = slice(g * FNET_GW, (g + 1) * FNET_GW)
        y = _dot(xh[:, gl], bdh) + _dot(xl[:, gl], bdh) + _dot(xh[:, gl], bdl)
        yc.append(y[:, :FNET_GW])
        ys.append(y[:, FNET_GW:])
    ych, ycl = _split2(jnp.concatenate(yc, axis=1))
    ysh, ysl = _split2(jnp.concatenate(ys, axis=1))
    clh = clh_ref[...]
    slh = slh_ref[...]
    re = _dot(clh, ych) + _dot(cll_ref[...], ych) + _dot(clh, ycl)
    im = _dot(slh, ysh) + _dot(sll_ref[...], ysh) + _dot(slh, ysl)
    o_ref[...] = ((re - im) * scale).astype(o_ref.dtype)


def _dft_consts(seq):
    def cs(n):
        k = np.arange(n, dtype=np.int64)
        ang = 2.0 * np.pi * ((k[:, None] * k[None, :]) % n).astype(np.float64) / n
        return np.cos(ang), np.sin(ang)

    bd = np.concatenate(cs(FNET_GW), axis=1)
    cl, sl_ = cs(seq)

    def hl(a):
        a32 = jnp.asarray(a, F32)
        hi = a32.astype(BF16)
        return hi, (a32 - hi.astype(F32)).astype(BF16)

    return hl(bd) + hl(cl) + hl(sl_)


def _fourier(f, nb, seq):
    t = f.shape[0]
    consts = _dft_consts(seq)
    const = lambda a: pl.BlockSpec(a.shape, lambda b: (0, 0))
    return pl.pallas_call(
        functools.partial(_fourier_kernel, scale=1.0 / math.sqrt(seq * FNET_GW)),
        grid=(nb,),
        in_specs=[pl.BlockSpec((seq, MIX_W), lambda b: (b, 0))] + [const(a) for a in consts],
        out_specs=pl.BlockSpec((seq, MIX_W), lambda b: (b, 0)),
        out_shape=jax.ShapeDtypeStruct((t, MIX_W), BF16),
        compiler_params=_params("arbitrary"),
        name="fourier",
    )(f, *consts)


def _ssd_kernel(*refs, seq, has_h0, emit_state, has_prev):
    (xbc_ref, z_ref, dt_ref, cw_ref, cb_ref, bias_ref, alog_ref, dskip_ref, ng_ref, tri_ref) = refs[:10]
    rest = list(refs[10:])
    h0_ref = rest.pop(0) if has_h0 else None
    sprev_ref = rest.pop(0) if has_prev else None
    o_ref = rest.pop(0)
    sfin_ref = rest.pop(0) if emit_state else None
    xt_scr, bc_scr, tab, tabt, yt_scr, s_scr = rest
    nc = seq // SSM_CHUNK
    q = SSM_CHUNK

    lane = lax.broadcasted_iota(jnp.int32, (q, 128), 1)
    ii = lax.broadcasted_iota(jnp.int32, (q, q), 0)
    jj = lax.broadcasted_iota(jnp.int32, (q, q), 1)
    below = ii > jj
    above = ii < jj
    row = lax.broadcasted_iota(jnp.int32, (q, 1), 0)
    cw = cw_ref[...]
    amul = jnp.where(lane < 16, 1.0, jnp.where(lane < 48, -jnp.exp(alog_ref[...]), 0.0))
    tri = tri_ref[...]

    if has_h0:
        s_scr[...] = h0_ref[...]
    else:
        s_scr[...] = jnp.zeros_like(s_scr)

    def prep(c, carry):
        r0 = pl.multiple_of(c * q, q)
        cur = xbc_ref[pl.ds(r0, q), :]
        prev8 = xbc_ref[pl.ds(pl.multiple_of(jnp.maximum(r0 - 8, 0), 8), 8), :]
        next8 = xbc_ref[pl.ds(pl.multiple_of(jnp.minimum(r0 + q, seq - 8), 8), 8), :]
        prev_row = jnp.where(c > 0, prev8[7:8, :], 0.0)
        next_row = jnp.where(c < nc - 1, next8[0:1, :], 0.0)
        up = jnp.where(row == 0, prev_row, pltpu.roll(cur, 1, 0))
        dn = jnp.where(row == q - 1, next_row, pltpu.roll(cur, q - 1, 0))
        conv = up * cw[0:1, :] + cur * cw[1:2, :] + dn * cw[2:3, :] + cb_ref[...]
        act = conv * _sigmoid(conv)
        xt_scr[c] = act[:, :SSM_D_INNER].T
        bc_scr[pl.ds(r0, q), :] = act[:, SSM_D_INNER:].astype(BF16)

        raw = dt_ref[pl.ds(r0, q), :]
        rep = jnp.where(lane < 16, raw, jnp.where(lane < 32, pltpu.roll(raw, 16, 1), pltpu.roll(raw, 32, 1)))
        v = rep + bias_ref[...]
        sp = jnp.maximum(v, 0.0) + jnp.log1p(jnp.exp(-jnp.abs(v)))
        t = sp * amul
        hi = t.astype(BF16)
        r1 = t - hi.astype(F32)
        mid = r1.astype(BF16)
        lo = (r1 - mid.astype(F32)).astype(BF16)
        cum = _dot(tri, hi) + _dot(tri, mid) + _dot(tri, lo)
        tc = jnp.where(lane < 32, t, cum)
        tab[pl.ds(r0, q), :] = tc
        tabt[c] = tc.T
        return carry

    lax.fori_loop(0, nc, prep, 0)

    def pass_fwd(c, carry):
        r0 = pl.multiple_of(c * q, q)
        pc = tab[pl.ds(r0, q), :]
        pt = tabt[c]
        for g in range(2):
            bg = bc_scr[pl.ds(r0, q), 128 * g:128 * (g + 1)]
            cg = bc_scr[pl.ds(r0, q), 256 + 128 * g:256 + 128 * (g + 1)]
            cb = _dot_nt(cg, bg)
            y_in = _dot_nt(s_scr[0, 256 * g:256 * (g + 1), :].astype(BF16), cg)
            for hh in range(4):
                h = 4 * g + hh
                acf_row = pt[32 + h:33 + h, :]
                eb_col = pc[:, 40 + h:41 + h] - pc[:, 24 + h:25 + h]
                eb_row = pt[40 + h:41 + h, :] - pt[24 + h:25 + h, :]
                dtf_row = pt[h:h + 1, :]
                dtb_row = pt[8 + h:9 + h, :]
                seg = jnp.where(below, pc[:, 32 + h:33 + h] - acf_row, jnp.where(above, eb_row - eb_col, 0.0))
                wgt = jnp.where(below, dtf_row, jnp.where(above, dtb_row, dtf_row + dtb_row))
                mp = cb * (jnp.exp(seg) * wgt)
                xth = xt_scr[c, 64 * h:64 * (h + 1), :]
                yd = _dot_nt(xth.astype(BF16), mp.astype(BF16))
                yo = y_in[64 * hh:64 * (hh + 1), :] * jnp.exp(acf_row)
                yt_scr[c, 64 * h:64 * (h + 1), :] = yd + yo + dskip_ref[h] * xth
                tot = pt[32 + h:33 + h, q - 1:q]
                wf_row = dtf_row * jnp.exp(tot - acf_row)
                st = _dot((xth * wf_row).astype(BF16), bg)
                s_scr[0, 64 * h:64 * (h + 1), :] = s_scr[0, 64 * h:64 * (h + 1), :] * jnp.exp(tot) + st
        return carry

    lax.fori_loop(0, nc, pass_fwd, 0)

    def pass_bwd(ci, carry):
        c = nc - 1 - ci
        r0 = pl.multiple_of(c * q, q)
        pt = tabt[c]
        for g in range(2):
            bg = bc_scr[pl.ds(r0, q), 128 * g:128 * (g + 1)]
            cg = bc_scr[pl.ds(r0, q), 256 + 128 * g:256 + 128 * (g + 1)]
            y_in = _dot_nt(s_scr[1, 256 * g:256 * (g + 1), :].astype(BF16), cg)
            for hh in range(4):
                h = 4 * g + hh
                eb_row = pt[40 + h:41 + h, :] - pt[24 + h:25 + h, :]
                tot = pt[40 + h:41 + h, q - 1:q]
                xth = xt_scr[c, 64 * h:64 * (h + 1), :]
                yt_scr[c, 64 * h:64 * (h + 1), :] += y_in[64 * hh:64 * (hh + 1), :] * jnp.exp(tot - eb_row)
                wb_row = pt[8 + h:9 + h, :] * jnp.exp(eb_row)
                st = _dot((xth * wb_row).astype(BF16), bg)
                s_scr[1, 64 * h:64 * (h + 1), :] = s_scr[1, 64 * h:64 * (h + 1), :] * jnp.exp(tot) + st
        zc = z_ref[pl.ds(r0, q), :]
        yg = yt_scr[c].T * (zc * _sigmoid(zc))
        ms = jnp.mean(yg * yg, axis=-1, keepdims=True)
        o_ref[pl.ds(r0, q), :] = (yg * lax.rsqrt(ms + EPS) * ng_ref[...]).astype(o_ref.dtype)
        return carry

    lax.fori_loop(0, nc, pass_bwd, 0)

    if emit_state and has_prev:
        sfin_ref[0] = sprev_ref[...]
        sfin_ref[1] = s_scr[...]
    elif emit_state:
        sfin_ref[...] = s_scr[...]


def _ssd(xbc, z, dt, conv_w, conv_b, bias128, alog128, dskip, norm_g, tri, h0, s_prev, nb, seq, emit_state):
    t = xbc.shape[0]
    has_h0 = h0 is not None
    has_prev = s_prev is not None
    const = lambda shape: pl.BlockSpec(shape, lambda b: (0,) * len(shape))
    in_specs = [
        pl.BlockSpec((seq, SSM_CONV_CH), lambda b: (b, 0)),
        pl.BlockSpec((seq, SSM_D_INNER), lambda b: (b, 0)),
        pl.BlockSpec((seq, 128), lambda b: (b, 0)),
        const((3, SSM_CONV_CH)), const((1, SSM_CONV_CH)), const((1, 128)), const((1, 128)),
        pl.BlockSpec(memory_space=pltpu.SMEM), const((1, SSM_D_INNER)), const((128, 128)),
    ]
    args = [xbc, z, dt, conv_w, conv_b, bias128, alog128, dskip, norm_g, tri]
    if has_h0:
        in_specs.append(pl.BlockSpec((None, 2, 512, 128), lambda b: (b, 0, 0, 0)))
        args.append(h0)
    if has_prev:
        in_specs.append(pl.BlockSpec((None, 2, 512, 128), lambda b: (b, 0, 0, 0)))
        args.append(s_prev)
    out_specs = [pl.BlockSpec((seq, SSM_D_INNER), lambda b: (b, 0))]
    out_shape = [jax.ShapeDtypeStruct((t, SSM_D_INNER), BF16)]
    if emit_state and has_prev:
        out_specs.append(pl.BlockSpec((None, 2, 2, 512, 128), lambda b: (b, 0, 0, 0, 0)))
        out_shape.append(jax.ShapeDtypeStruct((nb, 2, 2, 512, 128), F32))
    elif emit_state:
        out_specs.append(pl.BlockSpec((None, 2, 512, 128), lambda b: (b, 0, 0, 0)))
        out_shape.append(jax.ShapeDtypeStruct((nb, 2, 512, 128), F32))
    return pl.pallas_call(
        functools.partial(_ssd_kernel, seq=seq, has_h0=has_h0, emit_state=emit_state, has_prev=has_prev),
        grid=(nb,),
        in_specs=in_specs,
        out_specs=out_specs,
        out_shape=out_shape,
        scratch_shapes=[
            pltpu.VMEM((seq // SSM_CHUNK, SSM_D_INNER, SSM_CHUNK), F32),
            pltpu.VMEM((seq, 512), BF16),
            pltpu.VMEM((seq, 128), F32),
            pltpu.VMEM((seq // SSM_CHUNK, 128, SSM_CHUNK), F32),
            pltpu.VMEM((seq // SSM_CHUNK, SSM_D_INNER, SSM_CHUNK), F32),
            pltpu.VMEM((2, 512, 128), F32),
        ],
        compiler_params=_params("arbitrary"),
        name="ssd",
    )(*args)


def _gmlp_kernel(uv_ref, g_ref, ws_ref, bs_ref, o_ref):
    rows = uv_ref.shape[0]
    c0 = math.sqrt(2.0 / math.pi)
    for c in range(rows // GMLP_CHUNK):
        sl = slice(c * GMLP_CHUNK, (c + 1) * GMLP_CHUNK)
        x = uv_ref[sl, :]
        ge =x * (0.5 * (1.0 + jnp.tanh(c0 * (x + 0.044715 * (x * x * x)))))
        u = ge[:, :MIX_W]
        v = ge[:, MIX_W:]
        v = v * lax.rsqrt(jnp.mean(v * v, axis=-1, keepdims=True) + EPS) * g_ref[...]
        for g in range(4):
            gl = slice(g * 128, (g + 1) * 128)
            mixed = _dot(ws_ref[g], v[:, gl].astype(BF16)) + bs_ref[:, g:g + 1]
            o_ref[sl, gl] = (u[:, gl] * mixed).astype(o_ref.dtype)


def _gmlp(uv, norm_g, ws, bs_t):
    t = uv.shape[0]
    tm = 256
    return pl.pallas_call(
        _gmlp_kernel,
        grid=(t // tm,),
        in_specs=[
            pl.BlockSpec((tm, 2 * MIX_W), lambda m: (m, 0)),
            pl.BlockSpec((1, MIX_W), lambda m: (0, 0)),
            pl.BlockSpec((4, 128, 128), lambda m: (0, 0, 0)),
            pl.BlockSpec((128, 4), lambda m: (0, 0)),
        ],
        out_specs=pl.BlockSpec((tm, MIX_W), lambda m: (m, 0)),
        out_shape=jax.ShapeDtypeStruct((t, MIX_W), BF16),
        compiler_params=_params("arbitrary"),
        name="gmlp",
    )(uv, norm_g, ws, bs_t)


def _merge_kernel(x_ref, h_ref, g_ref, b0, b1, b2, b3, wg0, wg1, wg2, wg3, wb_ref, wo_ref, o_ref):
    n = pl.program_id(1)

    @pl.when(n == 0)
    def _():
        o_ref[...] = jnp.zeros_like(o_ref)

    h = h_ref[...]
    merged = None
    for k, (b_ref, wg_ref) in enumerate(((b0, wg0), (b1, wg1), (b2, wg2), (b3, wg3))):
        gate = _sigmoid(_dot_nt(h, wg_ref[0]))
        term = gate * _dot(b_ref[...], wb_ref[k].astype(BF16))
        merged = term if merged is None else merged + term
    o_ref[...] += _dot(merged.astype(BF16), wo_ref[...].astype(BF16))

    @pl.when(n == pl.num_programs(1) - 1)
    def _():
        o_ref[...] = x_ref[...] + g_ref[...] * o_ref[...]


def _merge(x, h, mod3, branches, wt, w_branch, w_out, layer, row_fn):
    t = x.shape[0]
    tm, tn = 512, 256
    assert t % tm == 0
    nn = D_MODEL // tn
    rf = lambda m: row_fn(m * tm)
    gate_spec = lambda k: pl.BlockSpec(
        (pl.Element(1), pl.Element(tn), pl.Element(D_MODEL)),
        lambda m, n: (layer, pl.multiple_of(GATE_COL0 + k * D_MODEL + n * tn, 16), 0))
    return pl.pallas_call(
        _merge_kernel,
        grid=(t // tm, nn),
        in_specs=[
            pl.BlockSpec((tm, D_MODEL), lambda m, n: (m, 0)),
            pl.BlockSpec((tm, D_MODEL), lambda m, n: (m, 0)),
            _mod_spec(2, rf),
        ] + [pl.BlockSpec((tm, MIX_W), lambda m, n: (m, 0))] * 4
          + [gate_spec(k) for k in range(4)] + [
            pl.BlockSpec((None, N_BRANCH, MIX_W, tn), lambda m, n: (layer, 0, 0, n)),
            pl.BlockSpec((None, tn, D_MODEL), lambda m, n: (layer, n, 0)),
        ],
        out_specs=pl.BlockSpec((tm, D_MODEL), lambda m, n: (m, 0)),
        out_shape=jax.ShapeDtypeStruct((t, D_MODEL), F32),
        compiler_params=_params("arbitrary", "arbitrary"),
        name="merge",
    )(x, h, mod3, *branches, wt, wt, wt, wt, w_branch, w_out)


def _ffn_kernel(x_ref, sc_ref, sh_ref, g_ref, w1_ref, w2_ref, o_ref, h_scr):
    n = pl.program_id(1)

    @pl.when(n == 0)
    def _():
        h_scr[...] = _rms_mod(x_ref[...], sc_ref[...], sh_ref[...]).astype(BF16)
        o_ref[...] = jnp.zeros_like(o_ref)

    a = jnp.maximum(_dot(h_scr[...], w1_ref[...].astype(BF16)), 0.0)
    o_ref[...] += _dot((a * a).astype(BF16), w2_ref[...].astype(BF16))

    @pl.when(n == pl.num_programs(1) - 1)
    def _():
        o_ref[...] = x_ref[...] + g_ref[...] * o_ref[...]


def _ffn(x, mod3, w1, w2, layer, row_fn):
    t = x.shape[0]
    tm, tf = 1024, 512
    assert t % tm == 0
    rf = lambda m: row_fn(m * tm)
    return pl.pallas_call(
        _ffn_kernel,
        grid=(t // tm, D_FF // tf),
        in_specs=[
            pl.BlockSpec((tm, D_MODEL), lambda m, n: (m, 0), pipeline_mode=pl.Buffered(1)),
            _mod_spec(4, rf), _mod_spec(3, rf), _mod_spec(5, rf),
            pl.BlockSpec((None, D_MODEL, tf), lambda m, n: (layer, 0, n)),
            pl.BlockSpec((None, tf, D_MODEL), lambda m, n: (layer, n, 0)),
        ],
        out_specs=pl.BlockSpec((tm, D_MODEL), lambda m, n: (m, 0)),
        out_shape=jax.ShapeDtypeStruct((t, D_MODEL), F32),
        scratch_shapes=[pltpu.VMEM((tm, D_MODEL), BF16)],
        compiler_params=_params("arbitrary", "arbitrary"),
        name="ffn",
    )(x, mod3, mod3, mod3, w1, w2)


def _rope_tables(seq):
    half = 16
    freqs = ROPE_THETA ** (-jnp.arange(half, dtype=F32) / half)
    n_rows = seq // GRID_W
    rows = jnp.repeat(jnp.arange(n_rows), GRID_W).astype(F32)
    cols = jnp.tile(jnp.arange(GRID_W), n_rows).astype(F32)
    ar = rows[:, None] * freqs[None, :]
    ac = cols[:, None] * freqs[None, :]
    cos64 = jnp.concatenate([jnp.cos(ar), jnp.cos(ar), jnp.cos(ac), jnp.cos(ac)], axis=-1)
    sin64 = jnp.concatenate([-jnp.sin(ar), jnp.sin(ar), -jnp.sin(ac), jnp.sin(ac)], axis=-1)
    return (jnp.tile(cos64, (1, ATT_HEADS)), jnp.tile(sin64, (1, ATT_HEADS)),
            jnp.tile(cos64, (1, ATT_KV_HEADS)), jnp.tile(sin64, (1, ATT_KV_HEADS)))


def _layer_path(x, mod3, lw, row_fn, nb, seq, lat, s_prev=None):
    (wt, qg, kg, gsum, conv_w, conv_b, bias128, alog128, dskip, ssm_g, tri, gmlp_g, ws, bs_t,
     w_branch, w_out, w_ff1, w_ff2, layer) = lw
    h, qkv, f_in, xbc, z, dt, uv = _inproj(x, mod3, wt, layer, row_fn)
    if lat is None:
        o_att, k_new, v_new = _attention_ctx(qkv, qg, kg, gsum, nb, seq)
        h0 = None
    else:
        rope_tabs, cache_k, cache_v, h0 = lat
        o_att = _attention_lat(qkv, qg, kg, gsum, rope_tabs, cache_k, cache_v, nb, seq)
        k_new = v_new = None
    o_fnet = _fourier(f_in, nb, seq)
    ssd_out = _ssd(xbc, z, dt, conv_w, conv_b, bias128, alog128, dskip, ssm_g, tri, h0, s_prev, nb, seq,
                   emit_state=lat is None)
    o_ssm = ssd_out[0]
    s_fin = ssd_out[1] if lat is None else None
    o_gmlp = _gmlp(uv, gmlp_g, ws, bs_t)
    x = _merge(x, h, mod3, (o_att, o_fnet, o_ssm, o_gmlp), wt, w_branch, w_out, layer, row_fn)
    x = _ffn(x, mod3, w_ff1, w_ff2, layer, row_fn)
    return x, k_new, v_new, s_fin


def kernel(x_prompt, x_sample, c, cache_k, cache_v, state_ssm, c_ctx, w_mod, b_mod, w_in, q_norm_g, k_norm_g,
           conv_w, conv_b, a_log, dt_bias, d_skip, ssm_norm_g, gmlp_norm_g, w_spatial, b_spatial, w_branch,
           w_out, w_ff1, w_ff2):
    nb_ctx, seq_ctx, _ = x_prompt.shape
    nb_lat, seq_lat, _ = x_sample.shape
    assert seq_lat == LAT_LEN and 1 + nb_lat <= 8

    cond8 = jnp.zeros((8, D_MODEL), F32).at[0].set(c_ctx).at[1:1 + nb_lat].set(c)
    mod = _modulation(cond8, w_mod, b_mod)

    gsum = jnp.asarray(np.kron(np.eye(ATT_HEADS), np.ones((HEAD_DIM, HEAD_DIM))), BF16)
    tri = jnp.asarray(np.tril(np.ones((SSM_CHUNK, SSM_CHUNK))), BF16)
    rope_tabs = _rope_tables(seq_lat)

    xp = x_prompt.reshape(nb_ctx * seq_ctx, D_MODEL)
    xs = x_sample.reshape(nb_lat * seq_lat, D_MODEL)
    ctx_row = lambda tok: 0
    lat_row = lambda tok: 1 + tok // LAT_LEN

    wt = _cast_w_in(jnp.swapaxes(w_in, 1, 2))

    assert DEPTH == 2
    new_k, new_v, s_l = [], [], None
    for l in range(DEPTH):
        zeros80 = jnp.zeros((128 - 48,), F32)
        bias128 = jnp.concatenate([jnp.tile(dt_bias[l].reshape(16), 3), zeros80]).reshape(1, 128)
        alog128 = jnp.concatenate([jnp.zeros((16,), F32), jnp.tile(a_log[l].reshape(16), 2), zeros80]).reshape(1, 128)
        lw = (
            wt,
            jnp.tile(q_norm_g[l], ATT_HEADS).reshape(1, 512), jnp.tile(k_norm_g[l], ATT_KV_HEADS).reshape(1, 128),
            gsum, conv_w[l], conv_b[l].reshape(1, SSM_CONV_CH), bias128, alog128,
            d_skip[l], ssm_norm_g[l].reshape(1, SSM_D_INNER), tri,
            gmlp_norm_g[l].reshape(1, MIX_W), w_spatial[l].astype(BF16), b_spatial[l].T,
            w_branch, w_out, w_ff1, w_ff2, l,
        )
        mod3 = mod[l].reshape(8, 1, 6 * D_MODEL)
        xp, k_l, v_l, s_l = _layer_path(xp, mod3, lw, ctx_row, nb_ctx, seq_ctx, None, s_prev=s_l)
        new_k.append(k_l.reshape(nb_ctx, seq_ctx, ATT_KV_HEADS, HEAD_DIM))
        new_v.append(v_l.reshape(nb_ctx, seq_ctx, ATT_KV_HEADS, HEAD_DIM))
        lat = (rope_tabs,
               cache_k[:, l].reshape(nb_lat, -1, ATT_KV_HEADS * HEAD_DIM),
               cache_v[:, l].reshape(nb_lat, -1, ATT_KV_HEADS * HEAD_DIM),
               state_ssm[:, l].reshape(nb_lat, 2, SSM_HEADS * SSM_HEAD_DIM, SSM_STATE))
        xs, _, _, _ = _layer_path(xs, mod3, lw, lat_row, nb_lat, seq_lat, lat)

    return (xp.reshape(nb_ctx, seq_ctx, D_MODEL), xs.reshape(nb_lat, seq_lat, D_MODEL),
            jnp.stack(new_k, axis=1), jnp.stack(new_v, axis=1),
            s_l.reshape(nb_ctx, DEPTH, 2, SSM_HEADS, SSM_HEAD_DIM, SSM_STATE))
```

```python
import functools
import math

import numpy as np
import jax
import jax.numpy as jnp
from jax import lax
from jax.experimental import pallas as pl
from jax.experimental.pallas import tpu as pltpu

F32 = jnp.float32
BF16 = jnp.bfloat16

D_MODEL = 2048
DEPTH = 2
GRID_W = 64
MIX_W = 512
N_BRANCH = 4
ATT_HEADS = 8
ATT_KV_HEADS = 2
HEAD_DIM = 64
ROPE_THETA = 10000.0
FNET_GW = 128
SSM_HEADS = 8
SSM_HEAD_DIM = 64
SSM_STATE = 128
SSM_CHUNK = 128
SSM_D_INNER = 512
SSM_CONV_CH = 1024
GMLP_CHUNK = 128
D_FF = 4 * D_MODEL
EPS = 1e-6
LAT_LEN = 1024

QKV_W = 768
COL_F = 768
COL_XBC = 1280
COL_Z = 2304
COL_DT = 2816
COL_UV = 2832
GATE_COL0 = 3856
N_IN = GATE_COL0 + N_BRANCH * D_MODEL
_IN_PIECES = ((0, QKV_W), (COL_F, COL_XBC), (COL_XBC, COL_Z), (COL_Z, COL_DT),
              (COL_DT, COL_DT + 128),
              (COL_UV, GATE_COL0))

VMEM_LIMIT = 56 * 1024 * 1024
NEG_BIG = -1e30


def _params(*sem):
    return pltpu.CompilerParams(dimension_semantics=sem, vmem_limit_bytes=VMEM_LIMIT)


def _dot(a, b):
    return jnp.dot(a, b, preferred_element_type=F32)


def _dot_nt(a, b):
    return lax.dot_general(a, b, (((1,), (1,)), ((), ())), preferred_element_type=F32)


def _dot_tn(a, b):
    return lax.dot_general(a, b, (((0,), (0,)), ((), ())), preferred_element_type=F32)


def _sigmoid(x):
    return 1.0 / (1.0 + jnp.exp(-x))


def _split2(x):
    hi = x.astype(BF16)
    lo = (x - hi.astype(F32)).astype(BF16)
    return hi, lo


def _rms_mod(x, sc, sh):
    ms = jnp.mean(x * x, axis=-1, keepdims=True)
    return x * lax.rsqrt(ms + EPS) * (1.0 + sc) + sh


def _mod_kernel(cond_ref, w_ref, b_ref, o_ref):
    c = cond_ref[...]
    s = c * _sigmoid(c)
    o_ref[...] = _dot(s.astype(BF16), w_ref[...].astype(BF16)) + b_ref[...]


def _modulation(cond8, w_mod, b_mod):
    tn = 1024
    n6 = 6 * D_MODEL
    return pl.pallas_call(
        _mod_kernel,
        grid=(DEPTH, n6 // tn),
        in_specs=[
            pl.BlockSpec((8, D_MODEL), lambda l, n: (0, 0)),
            pl.BlockSpec((None, D_MODEL, tn), lambda l, n: (l, 0, n)),
            pl.BlockSpec((None, 1, tn), lambda l, n: (l, 0, n)),
        ],
        out_specs=pl.BlockSpec((None, 8, tn), lambda l, n: (l, 0, n)),
        out_shape=jax.ShapeDtypeStruct((DEPTH, 8, n6), F32),
        compiler_params=_params("arbitrary", "arbitrary"),
        name="modulation",
    )(cond8, w_mod, b_mod.reshape(DEPTH, 1, n6))


def _mod_spec(chunk, row_fn):
    return pl.BlockSpec((None, 1, D_MODEL), lambda m, *_: (row_fn(m), 0, chunk))


def _cast_kernel(w_ref, o_ref):
    o_ref[...] = w_ref[...].astype(o_ref.dtype)


def _cast_bf16(w):
    flat = w.reshape(-1, D_MODEL)
    rows = 1024
    out = pl.pallas_call(
        _cast_kernel,
        grid=(pl.cdiv(flat.shape[0], rows),),
        in_specs=[pl.BlockSpec((rows, D_MODEL), lambda r: (r, 0))],
        out_specs=pl.BlockSpec((rows, D_MODEL), lambda r: (r, 0)),
        out_shape=jax.ShapeDtypeStruct(flat.shape, BF16),
        compiler_params=_params("arbitrary"),
        name="cast_bf16",
    )(flat)
    return out.reshape(w.shape)


def _gmlp_chunk(x, gain, ws_ref, bs_ref):
    c0 = math.sqrt(2.0 / math.pi)
    ge = x * (0.5 * (1.0 + jnp.tanh(c0 * (x + 0.044715 * (x * x * x)))))
    u = ge[:, :MIX_W]
    v = ge[:, MIX_W:]
    v = v * lax.rsqrt(jnp.mean(v * v, axis=-1, keepdims=True) + EPS) * gain
    outs = []
    for g in range(MIX_W // 128):
        gl = slice(g * 128, (g + 1) * 128)
        mixed = _dot(ws_ref[g], v[:, gl].astype(BF16)) + bs_ref[:, g:g + 1]
        outs.append(u[:, gl] * mixed)
    return jnp.concatenate(outs, axis=1)


def _inproj_kernel(x_ref, sc_ref, sh_ref, w_ref, gg_ref, ws_ref, bs_ref, h_ref, *out_refs):
    hb = _rms_mod(x_ref[...], sc_ref[...], sh_ref[...]).astype(BF16)
    h_ref[...] = hb
    o_gmlp = out_refs[-1]
    a, b = _IN_PIECES[-1]
    uv = _dot_nt(hb, w_ref[a:b, :])
    for c in range(x_ref.shape[0] // GMLP_CHUNK):
        sl = slice(c * GMLP_CHUNK, (c + 1) * GMLP_CHUNK)
        o_gmlp[sl, :] = _gmlp_chunk(uv[sl, :], gg_ref[...], ws_ref, bs_ref).astype(o_gmlp.dtype)
    for o_ref, (a, b) in zip(out_refs[:-1], _IN_PIECES[:-1]):
        o_ref[...] = _dot_nt(hb, w_ref[a:b, :])


def _inproj(x, mod3, wt, layer, gmlp_g, ws, bs_t, row_fn):
    t = x.shape[0]
    tm = 256
    assert t % tm == 0
    widths = [b - a for a, b in _IN_PIECES[:-1]]
    const = lambda shape: pl.BlockSpec(shape, lambda m: (0,) * len(shape))
    return pl.pallas_call(
        _inproj_kernel,
        grid=(t // tm,),
        in_specs=[
            pl.BlockSpec((tm, D_MODEL), lambda m: (m, 0)),
            _mod_spec(1, lambda m: row_fn(m * tm)),
            _mod_spec(0, lambda m: row_fn(m * tm)),
            pl.BlockSpec((None, GATE_COL0, D_MODEL), lambda m: (layer, 0, 0), pipeline_mode=pl.Buffered(1)),
            const((1, MIX_W)), const((4, 128, 128)), const((128, 4)),
        ],
        out_specs=[pl.BlockSpec((tm, D_MODEL), lambda m: (m, 0))]
                  + [pl.BlockSpec((tm, w), lambda m: (m, 0)) for w in widths]
                  + [pl.BlockSpec((tm, MIX_W), lambda m: (m, 0))],
        out_shape=[jax.ShapeDtypeStruct((t, D_MODEL), BF16)]
                  + [jax.ShapeDtypeStruct((t, w), F32) for w in widths]
                  + [jax.ShapeDtypeStruct((t, MIX_W), BF16)],
        compiler_params=_params("arbitrary"),
        name="inproj",
    )(x, mod3, mod3, wt, gmlp_g, ws, bs_t)


def _head_rms(t, gsum, gain):
    t2 = t * t
    hi, lo = _split2(t2)
    ss = _dot(hi, gsum) + _dot(lo, gsum)
    return t * lax.rsqrt(ss * (1.0 / HEAD_DIM) + EPS) * gain


def _rope(t, cos, sin_signed):
    w = t.shape[-1]
    up = pltpu.roll(t, w - 16, 1)
    dn = pltpu.roll(t, 16, 1)
    lane = lax.broadcasted_iota(jnp.int32, t.shape, 1)
    partner = jnp.where((lane & 31) < 16, up, dn)
    return t * cos + partner * sin_signed


ATT_GROUP = ATT_HEADS // ATT_KV_HEADS
GROUP_W = ATT_GROUP * HEAD_DIM


def _store_group_keys(k, kbd_scr, row0):
    s = k.shape[0]
    k2 = jnp.concatenate([k, k], axis=1)
    k2r = pltpu.roll(k2, HEAD_DIM, 1)
    lane = lax.broadcasted_iota(jnp.int32, k2.shape, 1)
    first = (lane & (2 * HEAD_DIM - 1)) < HEAD_DIM
    for j, tiled in enumerate((jnp.where(first, k2, k2r), jnp.where(first, k2r, k2))):
        for hh in range(ATT_GROUP):
            slab = jnp.where(lax.shift_right_logical(lane, 6) == hh, tiled, 0.0)
            kbd_scr[j, hh, row0:row0 + s, :] = slab.astype(kbd_scr.dtype)


def _attend_heads(q, kbd_scr, vt_scr, o_ref):
    qb = (q * (HEAD_DIM ** -0.5 * math.log2(math.e))).astype(BF16)
    s_len = kbd_scr.shape[2]
    for j in range(ATT_KV_HEADS):
        kbd = kbd_scr[j].reshape(ATT_GROUP * s_len, GROUP_W)
        s_all = _dot_nt(kbd, qb[:, j * GROUP_W:(j + 1) * GROUP_W])
        outs = []
        for hh in range(ATT_GROUP):
            s = s_all[hh * s_len:(hh + 1) * s_len, :]
            m = jnp.max(s, axis=0, keepdims=True)
            e = jnp.exp2(s - m)
            den = jnp.sum(e, axis=0, keepdims=True)
            outs.append(_dot(vt_scr[j * HEAD_DIM:(j + 1) * HEAD_DIM, :], e.astype(BF16)) / den)
        o_ref[:, j * GROUP_W:(j + 1) * GROUP_W] = jnp.concatenate(outs, axis=0).T.astype(o_ref.dtype)


def _attn_ctx_kernel(q_ref, kv_ref, qg_ref, kg_ref, gsum_ref, o_ref, kout_ref, vout_ref, kbd_scr, vt_scr):
    gsum = gsum_ref[...]
    kv = kv_ref[...]
    k = _head_rms(kv[:, :128], gsum[:128, :128], kg_ref[...])
    v = kv[:, 128:]
    kout_ref[...] = k
    vout_ref[...] = v
    _store_group_keys(k, kbd_scr, 0)
    vt_scr[...] = v.T.astype(BF16)
    q = _head_rms(q_ref[...], gsum, qg_ref[...])
    _attend_heads(q, kbd_scr, vt_scr, o_ref)


def _attn_lat_kernel(q_ref, kv_ref, qg_ref, kg_ref, gsum_ref, cq_ref, sq_ref, ck_ref, sk_ref,
                     cache_k_ref, cache_v_ref, o_ref, kbd_scr, vt_scr):
    gsum = gsum_ref[...]
    past = cache_k_ref.shape[0]

    @pl.when(pl.program_id(1) == 0)
    def _():
        kv = kv_ref[...]
        k = _head_rms(kv[:, :128], gsum[:128, :128], kg_ref[...])
        k = _rope(k, ck_ref[...], sk_ref[...])
        _store_group_keys(cache_k_ref[...], kbd_scr, 0)
        _store_group_keys(k, kbd_scr, past)
        vt_scr[:, :past] = cache_v_ref[...].T.astype(BF16)
        vt_scr[:, past:] = kv[:, 128:].T.astype(BF16)

    q = _head_rms(q_ref[...], gsum, qg_ref[...])
    q = _rope(q, cq_ref[...], sq_ref[...])
    _attend_heads(q, kbd_scr, vt_scr, o_ref)


def _attention_ctx(qkv, qg, kg, gsum, nb, seq):
    t = qkv.shape[0]
    const = lambda shape: pl.BlockSpec(shape, lambda b: (0, 0))
    return pl.pallas_call(
        _attn_ctx_kernel,
        grid=(nb,),
        in_specs=[
            pl.BlockSpec((seq, 512), lambda b: (b, 0)),
            pl.BlockSpec((seq, 256), lambda b: (b, 2)),
            const((1, 512)), const((1, 128)), const((512, 512)),
        ],
        out_specs=[
            pl.BlockSpec((seq, 512), lambda b: (b, 0)),
            pl.BlockSpec((seq, 128), lambda b: (b, 0)),
            pl.BlockSpec((seq, 128), lambda b: (b, 0)),
        ],
        out_shape=[
            jax.ShapeDtypeStruct((t, 512), BF16),
            jax.ShapeDtypeStruct((t, 128), F32),
            jax.ShapeDtypeStruct((t, 128), F32),
        ],
        scratch_shapes=[pltpu.VMEM((ATT_KV_HEADS, ATT_GROUP, seq, GROUP_W), BF16), pltpu.VMEM((128, seq), BF16)],
        compiler_params=_params("arbitrary"),
        name="attn_ctx",
    )(qkv, qkv, qg, kg, gsum)


def _attention_lat(qkv, qg, kg, gsum, rope_tabs, cache_k, cache_v, nb, seq):
    t = qkv.shape[0]
    qblk = 256
    nq = seq // qblk
    past = cache_k.shape[1]
    cq, sq, ck, sk = rope_tabs
    const = lambda shape: pl.BlockSpec(shape, lambda b, i: (0, 0))
    return pl.pallas_call(
        _attn_lat_kernel,
        grid=(nb, nq),
        in_specs=[
            pl.BlockSpec((qblk, 512), lambda b, i: (b * nq + i, 0)),
            pl.BlockSpec((seq, 256), lambda b, i: (b, 2)),
            const((1, 512)), const((1, 128)), const((512, 512)),
            pl.BlockSpec((qblk, 512), lambda b, i: (i, 0)),
            pl.BlockSpec((qblk, 512), lambda b, i: (i, 0)),
            const((seq, 128)), const((seq, 128)),
            pl.BlockSpec((None, past, 128), lambda b, i: (b, 0, 0)),
            pl.BlockSpec((None, past, 128), lambda b, i: (b, 0, 0)),
        ],
        out_specs=pl.BlockSpec((qblk, 512), lambda b, i: (b * nq + i, 0)),
        out_shape=jax.ShapeDtypeStruct((t, 512), BF16),
        scratch_shapes=[pltpu.VMEM((ATT_KV_HEADS, ATT_GROUP, past + seq, GROUP_W), BF16),
                        pltpu.VMEM((128, past + seq), BF16)],
        compiler_params=_params("arbitrary", "arbitrary"),
        name="attn_lat",
    )(qkv, qkv, qg, kg, gsum, cq, sq, ck, sk, cache_k, cache_v)


def _fourier_kernel(f_ref, bdh_ref, bdl_ref, clh_ref, cll_ref, slh_ref, sll_ref, o_ref, *, scale):
    xh, xl = _split2(f_ref[...])
    bdh = bdh_ref[...]
    bdl = bdl_ref[...]
    yc, ys = [], []
    for g in range(MIX_W // FNET_GW):
        gl = slice(g * FNET_GW, (g + 1) * FNET_GW)
        y = _dot(xh[:, gl], bdh) + _dot(xl[:, gl], bdh) + _dot(xh[:, gl], bdl)
        yc.append(y[:, :FNET_GW])
        ys.append(y[:, FNET_GW:])
    ych, ycl = _split2(jnp.concatenate(yc, axis=1))
    ysh, ysl = _split2(jnp.concatenate(ys, axis=1))
    clh = clh_ref[...]
    slh = slh_ref[...]
    re = _dot(clh, ych) + _dot(cll_ref[...], ych) + _dot(clh, ycl)
    im = _dot(slh, ysh) + _dot(sll_ref[...], ysh) + _dot(slh, ysl)
    o_ref[...] = ((re - im) * scale).astype(o_ref.dtype)


def _dft_consts(seq):
    def cs(n):
        k = np.arange(n, dtype=np.int64)
        ang = 2.0 * np.pi * ((k[:, None] * k[None, :]) % n).astype(np.float64) / n
        return np.cos(ang), np.sin(ang)

    bd = np.concatenate(cs(FNET_GW), axis=1)
    cl, sl_ = cs(seq)

    def hl(a):
        a32 = jnp.asarray(a, F32)
        hi = a32.astype(BF16)
        return hi, (a32 - hi.astype(F32)).astype(BF16)

    return hl(bd) + hl(cl) + hl(sl_)


def _fourier(f, nb, seq):
    t = f.shape[0]
    consts = _dft_consts(seq)
    const = lambda a: pl.BlockSpec(a.shape, lambda b: (0, 0))
    return pl.pallas_call(
        functools.partial(_fourier_kernel, scale=1.0 / math.sqrt(seq * FNET_GW)),
        grid=(nb,),
        in_specs=[pl.BlockSpec((seq, MIX_W), lambda b: (b, 0))] + [const(a) for a in consts],
        out_specs=pl.BlockSpec((seq, MIX_W), lambda b: (b, 0)),
        out_shape=jax.ShapeDtypeStruct((t, MIX_W), BF16),
        compiler_params=_params("arbitrary"),
        name="fourier",
    )(f, *consts)


def _ssd_kernel(*refs, seq, has_h0, emit_state, has_prev):
    (xbc_ref, z_ref, dt_ref, cw_ref, cb_ref, bias_ref, alog_ref, dskip_ref, ng_ref, tri_ref) = refs[:10]
    rest = list(refs[10:])
    h0_ref = rest.pop(0) if has_h0 else None
    sprev_ref = rest.pop(0) if has_prev else None
    o_ref = rest.pop(0)
    sfin_ref = rest.pop(0) if emit_state else None
    xt_scr, bc_scr, tab, tabt, yt_scr, s_scr = rest
    nc = seq // SSM_CHUNK
    q = SSM_CHUNK

    lane = lax.broadcasted_iota(jnp.int32, (q, 128), 1)
    ii = lax.broadcasted_iota(jnp.int32, (q, q), 0)
    jj = lax.broadcasted_iota(jnp.int32, (q, q), 1)
    below = ii > jj
    above = ii < jj
    row = lax.broadcasted_iota(jnp.int32, (q, 1), 0)
    cw = cw_ref[...]
    amul = jnp.where(lane < 16, 1.0, jnp.where(lane < 48, -jnp.exp(alog_ref[...]), 0.0))
    tri = tri_ref[...]

    if has_h0:
        s_scr[...] = h0_ref[...]
    else:
        s_scr[...] = jnp.zeros_like(s_scr)

    def prep(c, carry):
        r0 = pl.multiple_of(c * q, q)
        cur = xbc_ref[pl.ds(r0, q), :]
        prev8 = xbc_ref[pl.ds(pl.multiple_of(jnp.maximum(r0 - 8, 0), 8), 8), :]
        next8 = xbc_ref[pl.ds(pl.multiple_of(jnp.minimum(r0 + q, seq - 8), 8), 8), :]
        prev_row = jnp.where(c > 0, prev8[7:8, :], 0.0)
        next_row = jnp.where(c < nc - 1, next8[0:1, :], 0.0)
        up = jnp.where(row == 0, prev_row, pltpu.roll(cur, 1, 0))
        dn = jnp.where(row == q - 1, next_row, pltpu.roll(cur, q - 1, 0))
        conv = up * cw[0:1, :] + cur * cw[1:2, :] + dn * cw[2:3, :] + cb_ref[...]
        act = conv * _sigmoid(conv)
        xt_scr[c] = act[:, :SSM_D_INNER].T
        bc_scr[pl.ds(r0, q), :] = act[:, SSM_D_INNER:].astype(BF16)

        raw = dt_ref[pl.ds(r0, q), :]
        rep = jnp.where(lane < 16, raw, jnp.where(lane < 32, pltpu.roll(raw, 16, 1), pltpu.roll(raw, 32, 1)))
        v = rep + bias_ref[...]
        sp = jnp.maximum(v, 0.0) + jnp.log1p(jnp.exp(-jnp.abs(v)))
        t = sp * amul
        hi = t.astype(BF16)
        r1 = t - hi.astype(F32)
        mid = r1.astype(BF16)
        lo = (r1 - mid.astype(F32)).astype(BF16)
        cum = _dot(tri, hi) + _dot(tri, mid) + _dot(tri, lo)
        tc = jnp.where(lane < 32, t, cum)
        tab[pl.ds(r0, q), :] = tc
        tabt[c] = tc.T
        return carry

    lax.fori_loop(0, nc, prep, 0)

    def pass_fwd(c, carry):
        r0 = pl.multiple_of(c * q, q)
        pc = tab[pl.ds(r0, q), :]
        pt = tabt[c]
        for g in range(2):
            bg = bc_scr[pl.ds(r0, q), 128 * g:128 * (g + 1)]
            cg = bc_scr[pl.ds(r0, q), 256 + 128 * g:256 + 128 * (g + 1)]
            cb = _dot_nt(cg, bg)
            y_in = _dot_nt(s_scr[0, 256 * g:256 * (g + 1), :].astype(BF16), cg)
            for hh in range(4):
                h = 4 * g + hh
                acf_row = pt[32 + h:33 + h, :]
                eb_col = pc[:, 40 + h:41 + h] - pc[:, 24 + h:25 + h]
                eb_row = pt[40 + h:41 + h, :] - pt[24 + h:25 + h, :]
                dtf_row = pt[h:h + 1, :]
                dtb_row = pt[8 + h:9 + h, :]
                seg = jnp.where(below, pc[:, 32 + h:33 + h] - acf_row, jnp.where(above, eb_row - eb_col, 0.0))
                wgt = jnp.where(below, dtf_row, jnp.where(above, dtb_row, dtf_row + dtb_row))
                mp = cb * (jnp.exp(seg) * wgt)
                xth = xt_scr[c, 64 * h:64 * (h + 1), :]
                yd = _dot_nt(xth.astype(BF16), mp.astype(BF16))
                yo = y_in[64 * hh:64 * (hh + 1), :] * jnp.exp(acf_row)
                yt_scr[c, 64 * h:64 * (h + 1), :] = yd + yo + dskip_ref[h] * xth
                tot = pt[32 + h:33 + h, q - 1:q]
                wf_row = dtf_row * jnp.exp(tot - acf_row)
                st = _dot((xth * wf_row).astype(BF16), bg)
                s_scr[0, 64 * h:64 * (h + 1), :] = s_scr[0, 64 * h:64 * (h + 1), :] * jnp.exp(tot) + st
        return carry

    lax.fori_loop(0, nc, pass_fwd, 0)

    def pass_bwd(ci, carry):
        c = nc - 1 - ci
        r0 = pl.multiple_of(c * q, q)
        pt = tabt[c]
        for g in range(2):
            bg = bc_scr[pl.ds(r0, q), 128 * g:128 * (g + 1)]
            cg = bc_scr[pl.ds(r0, q), 256 + 128 * g:256 + 128 * (g + 1)]
            y_in = _dot_nt(s_scr[1, 256 * g:256 * (g + 1), :].astype(BF16), cg)
            for hh in range(4):
                h = 4 * g + hh
                eb_row = pt[40 + h:41 + h, :] - pt[24 + h:25 + h, :]
                tot = pt[40 + h:41 + h, q - 1:q]
                xth = xt_scr[c, 64 * h:64 * (h + 1), :]
                yt_scr[c, 64 * h:64 * (h + 1), :] += y_in[64 * hh:64 * (hh + 1), :] * jnp.exp(tot - eb_row)
                wb_row = pt[8 + h:9 + h, :] * jnp.exp(eb_row)
                st = _dot((xth * wb_row).astype(BF16), bg)
                s_scr[1, 64 * h:64 * (h + 1), :] = s_scr[1, 64 * h:64 * (h + 1), :] * jnp.exp(tot) + st
        zc = z_ref[pl.ds(r0, q), :]
        yg = yt_scr[c].T * (zc * _sigmoid(zc))
        ms = jnp.mean(yg * yg, axis=-1, keepdims=True)
        o_ref[pl.ds(r0, q), :] = (yg * lax.rsqrt(ms + EPS) * ng_ref[...]).astype(o_ref.dtype)
        return carry

    lax.fori_loop(0, nc, pass_bwd, 0)

    if emit_state and has_prev:
        sfin_ref[0] = sprev_ref[...]
        sfin_ref[1] = s_scr[...]
    elif emit_state:
        sfin_ref[...] = s_scr[...]


def _ssd(xbc, z, dt, conv_w, conv_b, bias128, alog128, dskip, norm_g, tri, h0, s_prev, nb, seq, emit_state):
    t = xbc.shape[0]
    has_h0 = h0 is not None
    has_prev = s_prev is not None
    const = lambda shape: pl.BlockSpec(shape, lambda b: (0,) * len(shape))
    in_specs = [
        pl.BlockSpec((seq, SSM_CONV_CH), lambda b: (b, 0)),
        pl.BlockSpec((seq, SSM_D_INNER), lambda b: (b, 0)),
        pl.BlockSpec((seq, 128), lambda b: (b, 0)),
        const((3, SSM_CONV_CH)), const((1, SSM_CONV_CH)), const((1, 128)), const((1, 128)),
        pl.BlockSpec(memory_space=pltpu.SMEM), const((1, SSM_D_INNER)), const((128, 128)),
    ]
    args = [xbc, z, dt, conv_w, conv_b, bias128, alog128, dskip, norm_g, tri]
    if has_h0:
        in_specs.append(pl.BlockSpec((None, 2, 512, 128), lambda b: (b, 0, 0, 0)))
        args.append(h0)
    if has_prev:
        in_specs.append(pl.BlockSpec((None, 2, 512, 128), lambda b: (b, 0, 0, 0)))
        args.append(s_prev)
    out_specs = [pl.BlockSpec((seq, SSM_D_INNER), lambda b: (b, 0))]
    out_shape = [jax.ShapeDtypeStruct((t, SSM_D_INNER), BF16)]
    if emit_state and has_prev:
        out_specs.append(pl.BlockSpec((None, 2, 2, 512, 128), lambda b: (b, 0, 0, 0, 0)))
        out_shape.append(jax.ShapeDtypeStruct((nb, 2, 2, 512, 128), F32))
    elif emit_state:
        out_specs.append(pl.BlockSpec((None, 2, 512, 128), lambda b: (b, 0, 0, 0)))
        out_shape.append(jax.ShapeDtypeStruct((nb, 2, 512, 128), F32))
    return pl.pallas_call(
        functools.partial(_ssd_kernel, seq=seq, has_h0=has_h0, emit_state=emit_state, has_prev=has_prev),
        grid=(nb,),
        in_specs=in_specs,
        out_specs=out_specs,
        out_shape=out_shape,
        scratch_shapes=[
            pltpu.VMEM((seq // SSM_CHUNK, SSM_D_INNER, SSM_CHUNK), F32),
            pltpu.VMEM((seq, 512), BF16),
            pltpu.VMEM((seq, 128), F32),
            pltpu.VMEM((seq // SSM_CHUNK, 128, SSM_CHUNK), F32),
            pltpu.VMEM((seq // SSM_CHUNK, SSM_D_INNER, SSM_CHUNK), F32),
            pltpu.VMEM((2, 512, 128), F32),
        ],
        compiler_params=_params("arbitrary"),
        name="ssd",
    )(*args)


def _merge_kernel(x_ref, h_ref, g_ref, b0, b1, b2, b3, wg0, wg1, wg2, wg3, wb_ref, wo_ref, o_ref):
    n = pl.program_id(1)

    @pl.when(n == 0)
    def _():
        o_ref[...] = jnp.zeros_like(o_ref)

    h = h_ref[...]
    merged = None
    for k, (b_ref, wg_ref) in enumerate(((b0, wg0), (b1, wg1), (b2, wg2), (b3, wg3))):
        gate = _sigmoid(_dot_nt(h, wg_ref[0]))
        term = gate * _dot(b_ref[...], wb_ref[k])
        merged = term if merged is None else merged + term
    o_ref[...] += _dot(merged.astype(BF16), wo_ref[...])

    @pl.when(n == pl.num_programs(1) - 1)
    def _():
        o_ref[...] = x_ref[...] + g_ref[...] * o_ref[...]


def _merge(x, h, mod3, branches, wt, w_branch, w_out, layer, row_fn):
    t = x.shape[0]
    tm, tn = 512, 256
    assert t % tm == 0
    nn = D_MODEL // tn
    rf = lambda m: row_fn(m * tm)
    gate_spec = lambda k: pl.BlockSpec(
        (pl.Element(1), pl.Element(tn), pl.Element(D_MODEL)),
        lambda m, n: (layer, pl.multiple_of(GATE_COL0 + k * D_MODEL + n * tn, 16), 0))
    return pl.pallas_call(
        _merge_kernel,
        grid=(t // tm, nn),
        in_specs=[
            pl.BlockSpec((tm, D_MODEL), lambda m, n: (m, 0)),
            pl.BlockSpec((tm, D_MODEL), lambda m, n: (m, 0)),
            _mod_spec(2, rf),
        ] + [pl.BlockSpec((tm, MIX_W), lambda m, n: (m, 0))] * 4
          + [gate_spec(k) for k in range(4)] + [
            pl.BlockSpec((None, N_BRANCH, MIX_W, tn), lambda m, n: (layer, 0, 0, n)),
            pl.BlockSpec((None, tn, D_MODEL), lambda m, n: (layer, n, 0)),
        ],
        out_specs=pl.BlockSpec((tm, D_MODEL), lambda m, n: (m, 0)),
        out_shape=jax.ShapeDtypeStruct((t, D_MODEL), F32),
        compiler_params=_params("arbitrary", "arbitrary"),
        name="merge",
    )(x, h, mod3, *branches, wt, wt, wt, wt, w_branch, w_out)


def _ffn_kernel(x_ref, sc_ref, sh_ref, g_ref, w1_ref, w2_ref, o_ref, h_scr):
    n = pl.program_id(1)

    @pl.when(n == 0)
    def _():
        h_scr[...] = _rms_mod(x_ref[...], sc_ref[...], sh_ref[...]).astype(BF16)
        o_ref[...] = jnp.zeros_like(o_ref)

    a = jnp.maximum(_dot(h_scr[...], w1_ref[...].astype(BF16)), 0.0)
    o_ref[...] += _dot((a * a).astype(BF16), w2_ref[...].astype(BF16))

    @pl.when(n == pl.num_programs(1) - 1)
    def _():
        o_ref[...] = x_ref[...] + g_ref[...] * o_ref[...]


def _ffn(x, mod3, w1, w2, layer, row_fn):
    t = x.shape[0]
    tm, tf = 1024, 512
    assert t % tm == 0
    rf = lambda m: row_fn(m * tm)
    return pl.pallas_call(
        _ffn_kernel,
        grid=(t // tm, D_FF // tf),
        in_specs=[
            pl.BlockSpec((tm, D_MODEL), lambda m, n: (m, 0), pipeline_mode=pl.Buffered(1)),
            _mod_spec(4, rf), _mod_spec(3, rf), _mod_spec(5, rf),
            pl.BlockSpec((None, D_MODEL, tf), lambda m, n: (layer, 0, n)),
            pl.BlockSpec((None, tf, D_MODEL), lambda m, n: (layer, n, 0)),
        ],
        out_specs=pl.BlockSpec((tm, D_MODEL), lambda m, n: (m, 0)),
        out_shape=jax.ShapeDtypeStruct((t, D_MODEL), F32),
        scratch_shapes=[pltpu.VMEM((tm, D_MODEL), BF16)],
        compiler_params=_params("arbitrary", "arbitrary"),
        name="ffn",
    )(x, mod3, mod3, mod3, w1, w2)


def _rope_tables(seq):
    half = 16
    freqs = ROPE_THETA ** (-jnp.arange(half, dtype=F32) / half)
    n_rows = seq // GRID_W
    rows = jnp.repeat(jnp.arange(n_rows), GRID_W).astype(F32)
    cols = jnp.tile(jnp.arange(GRID_W), n_rows).astype(F32)
    ar = rows[:, None] * freqs[None, :]
    ac = cols[:, None] * freqs[None, :]
    cos64 = jnp.concatenate([jnp.cos(ar), jnp.cos(ar), jnp.cos(ac), jnp.cos(ac)], axis=-1)
    sin64 = jnp.concatenate([-jnp.sin(ar), jnp.sin(ar), -jnp.sin(ac), jnp.sin(ac)], axis=-1)
    return (jnp.tile(cos64, (1, ATT_HEADS)), jnp.tile(sin64, (1, ATT_HEADS)),
            jnp.tile(cos64, (1, ATT_KV_HEADS)), jnp.tile(sin64, (1, ATT_KV_HEADS)))


def _layer_path(x, mod3, lw, row_fn, nb, seq, lat, s_prev=None):
    (wt, qg, kg, gsum, conv_w, conv_b, bias128, alog128, dskip, ssm_g, tri, gmlp_g, ws, bs_t,
     w_branch, w_out, w_ff1, w_ff2, layer) = lw
    h, qkv, f_in, xbc, z, dt, o_gmlp = _inproj(x, mod3, wt, layer, gmlp_g, ws, bs_t, row_fn)
    if lat is None:
        o_att, k_new, v_new = _attention_ctx(qkv, qg, kg, gsum, nb, seq)
        h0 = None
    else:
        rope_tabs, cache_k, cache_v, h0 = lat
        o_att = _attention_lat(qkv, qg, kg, gsum, rope_tabs, cache_k, cache_v, nb, seq)
        k_new = v_new = None
    o_fnet = _fourier(f_in, nb, seq)
    ssd_out = _ssd(xbc, z, dt, conv_w, conv_b, bias128, alog128, dskip, ssm_g, tri, h0, s_prev, nb, seq,
                   emit_state=lat is None)
    o_ssm = ssd_out[0]
    s_fin = ssd_out[1] if lat is None else None
    x = _merge(x, h, mod3, (o_att, o_fnet, o_ssm, o_gmlp), wt, w_branch, w_out, layer, row_fn)
    x = _ffn(x, mod3, w_ff1, w_ff2, layer, row_fn)
    return x, k_new, v_new, s_fin


def kernel(x_prompt, x_sample, c, cache_k, cache_v, state_ssm, c_ctx, w_mod, b_mod, w_in, q_norm_g, k_norm_g,
           conv_w, conv_b, a_log, dt_bias, d_skip, ssm_norm_g, gmlp_norm_g, w_spatial, b_spatial, w_branch,
           w_out, w_ff1, w_ff2):
    nb_ctx, seq_ctx, _ = x_prompt.shape
    nb_lat, seq_lat, _ = x_sample.shape
    assert seq_lat == LAT_LEN and 1 + nb_lat <= 8

    cond8 = jnp.zeros((8, D_MODEL), F32).at[0].set(c_ctx).at[1:1 + nb_lat].set(c)
    mod = _modulation(cond8, w_mod, b_mod)

    gsum = jnp.asarray(np.kron(np.eye(ATT_HEADS), np.ones((HEAD_DIM, HEAD_DIM))), BF16)
    tri = jnp.asarray(np.tril(np.ones((SSM_CHUNK, SSM_CHUNK))), BF16)
    rope_tabs = _rope_tables(seq_lat)

    xp = x_prompt.reshape(nb_ctx * seq_ctx, D_MODEL)
    xs = x_sample.reshape(nb_lat * seq_lat, D_MODEL)
    ctx_row = lambda tok: 0
    lat_row = lambda tok: 1 + tok // LAT_LEN

    wt = _cast_bf16(jnp.swapaxes(w_in, 1, 2))
    w_branch = _cast_bf16(w_branch)
    w_out = _cast_bf16(w_out)

    assert DEPTH == 2
    new_k, new_v, s_l = [], [], None
    for l in range(DEPTH):
        zeros80 = jnp.zeros((128 - 48,), F32)
        bias128 = jnp.concatenate([jnp.tile(dt_bias[l].reshape(16), 3), zeros80]).reshape(1, 128)
        alog128 = jnp.concatenate([jnp.zeros((16,), F32), jnp.tile(a_log[l].reshape(16), 2), zeros80]).reshape(1, 128)
        lw = (
            wt,
            jnp.tile(q_norm_g[l], ATT_HEADS).reshape(1, 512), jnp.tile(k_norm_g[l], ATT_KV_HEADS).reshape(1, 128),
            gsum, conv_w[l], conv_b[l].reshape(1, SSM_CONV_CH), bias128, alog128,
            d_skip[l], ssm_norm_g[l].reshape(1, SSM_D_INNER), tri,
            gmlp_norm_g[l].reshape(1, MIX_W), w_spatial[l].astype(BF16), b_spatial[l].T,
            w_branch, w_out, w_ff1, w_ff2, l,
        )
        mod3 = mod[l].reshape(8, 1, 6 * D_MODEL)
        xp, k_l, v_l, s_l = _layer_path(xp, mod3, lw, ctx_row, nb_ctx, seq_ctx, None, s_prev=s_l)
        new_k.append(k_l.reshape(nb_ctx, seq_ctx, ATT_KV_HEADS, HEAD_DIM))
        new_v.append(v_l.reshape(nb_ctx, seq_ctx, ATT_KV_HEADS, HEAD_DIM))
        lat = (rope_tabs,
               cache_k[:, l].reshape(nb_lat, -1, ATT_KV_HEADS * HEAD_DIM),
               cache_v[:, l].reshape(nb_lat, -1, ATT_KV_HEADS * HEAD_DIM),
               state_ssm[:, l].reshape(nb_lat, 2, SSM_HEADS * SSM_HEAD_DIM, SSM_STATE))
        xs, _, _, _ = _layer_path(xs, mod3, lw, lat_row, nb_lat, seq_lat, lat)

    return (xp.reshape(nb_ctx, seq_ctx, D_MODEL), xs.reshape(nb_lat, seq_lat, D_MODEL),
            jnp.stack(new_k, axis=1), jnp.stack(new_v, axis=1),
            s_l.reshape(nb_ctx, DEPTH, 2, SSM_HEADS, SSM_HEAD_DIM, SSM_STATE))
```

```python
import functools
import math

import numpy as np
import jax
import jax.numpy as jnp
from jax import lax
from jax.experimental import pallas as pl
from jax.experimental.pallas import tpu as pltpu

F32 = jnp.float32
BF16 = jnp.bfloat16

D_MODEL = 2048
DEPTH = 2
GRID_W = 64
MIX_W = 512
N_BRANCH = 4
ATT_HEADS = 8
ATT_KV_HEADS = 2
HEAD_DIM = 64
ROPE_THETA = 10000.0
FNET_GW = 128
SSM_HEADS = 8
SSM_HEAD_DIM = 64
SSM_STATE = 128
SSM_CHUNK = 128
SSM_D_INNER = 512
SSM_CONV_CH = 1024
GMLP_CHUNK = 128
D_FF = 4 * D_MODEL
EPS = 1e-6
LAT_LEN = 1024

QKV_W = 768
COL_F = 768
COL_XBC = 1280
COL_Z = 2304
COL_DT = 2816
COL_UV = 2832
GATE_COL0 = 3856
N_IN = GATE_COL0 + N_BRANCH * D_MODEL
_IN_PIECES = ((0, QKV_W), (COL_F, COL_XBC), (COL_XBC, COL_Z), (COL_Z, COL_DT),
              (COL_DT, COL_DT + 128),
              (COL_UV, GATE_COL0))

VMEM_LIMIT = 56 * 1024 * 1024
NEG_BIG = -1e30


def _params(*sem):
    return pltpu.CompilerParams(dimension_semantics=sem, vmem_limit_bytes=VMEM_LIMIT)


def _dot(a, b):
    return jnp.dot(a, b, preferred_element_type=F32)


def _dot_nt(a, b):
    return lax.dot_general(a, b, (((1,), (1,)), ((), ())), preferred_element_type=F32)


def _dot_tn(a, b):
    return lax.dot_general(a, b, (((0,), (0,)), ((), ())), preferred_element_type=F32)


def _sigmoid(x):
    return 1.0 / (1.0 + jnp.exp(-x))


def _split2(x):
    hi = x.astype(BF16)
    lo = (x - hi.astype(F32)).astype(BF16)
    return hi, lo


def _rms_mod(x, sc, sh):
    ms = jnp.mean(x * x, axis=-1, keepdims=True)
    return x * lax.rsqrt(ms + EPS) * (1.0 + sc) + sh


def _rms_mod_rows(x_ref, sc_ref, sh_ref, dst_ref):
    slab = 16

    def body(i, carry):
        rows = pl.ds(pl.multiple_of(i * slab, slab), slab)
        dst_ref[rows, :] = _rms_mod(x_ref[rows, :], sc_ref[...], sh_ref[...]).astype(dst_ref.dtype)
        return carry

    lax.fori_loop(0, x_ref.shape[0] // slab, body, 0, unroll=8)


def _mod_kernel(cond_ref, w_ref, b_ref, o_ref):
    c = cond_ref[...]
    s = c * _sigmoid(c)
    o_ref[...] = _dot(s.astype(BF16), w_ref[...].astype(BF16)) + b_ref[...]


def _modulation(cond8, w_mod, b_mod):
    tn = 1024
    n6 = 6 * D_MODEL
    return pl.pallas_call(
        _mod_kernel,
        grid=(DEPTH, n6 // tn),
        in_specs=[
            pl.BlockSpec((8, D_MODEL), lambda l, n: (0, 0)),
            pl.BlockSpec((None, D_MODEL, tn), lambda l, n: (l, 0, n)),
            pl.BlockSpec((None, 1, tn), lambda l, n: (l, 0, n)),
        ],
        out_specs=pl.BlockSpec((None, 8, tn), lambda l, n: (l, 0, n)),
        out_shape=jax.ShapeDtypeStruct((DEPTH, 8, n6), F32),
        compiler_params=_params("arbitrary", "arbitrary"),
        name="modulation",
    )(cond8, w_mod, b_mod.reshape(DEPTH, 1, n6))


def _mod_spec(chunk, row_fn):
    return pl.BlockSpec((None, 1, D_MODEL), lambda m, *_: (row_fn(m), 0, chunk))


def _cast_kernel(w_ref, o_ref):
    o_ref[...] = w_ref[...].astype(o_ref.dtype)


def _cast_bf16(w):
    flat = w.reshape(-1, D_MODEL)
    rows = 1024
    out = pl.pallas_call(
        _cast_kernel,
        grid=(pl.cdiv(flat.shape[0], rows),),
        in_specs=[pl.BlockSpec((rows, D_MODEL), lambda r: (r, 0))],
        out_specs=pl.BlockSpec((rows, D_MODEL), lambda r: (r, 0)),
        out_shape=jax.ShapeDtypeStruct(flat.shape, BF16),
        compiler_params=_params("arbitrary"),
        name="cast_bf16",
    )(flat)
    return out.reshape(w.shape)


def _gmlp_chunk(x, gain, ws_ref, bs_ref):
    c0 = math.sqrt(2.0 / math.pi)
    ge = x * (0.5 * (1.0 + jnp.tanh(c0 * (x + 0.044715 * (x * x * x)))))
    u = ge[:, :MIX_W]
    v = ge[:, MIX_W:]
    v = v * lax.rsqrt(jnp.mean(v * v, axis=-1, keepdims=True) + EPS) * gain
    outs = []
    for g in range(MIX_W // 128):
        gl = slice(g * 128, (g + 1) * 128)
        mixed = _dot(ws_ref[g], v[:, gl].astype(BF16)) + bs_ref[:, g:g + 1]
        outs.append(u[:, gl] * mixed)
    return jnp.concatenate(outs, axis=1)


def _inproj_kernel(x_ref, sc_ref, sh_ref, w_ref, gg_ref, ws_ref, bs_ref, h_ref, *out_refs):
    hb = _rms_mod(x_ref[...], sc_ref[...], sh_ref[...]).astype(BF16)
    h_ref[...] = hb
    o_gmlp = out_refs[-1]
    a, b = _IN_PIECES[-1]
    uv = _dot_nt(hb, w_ref[a:b, :])
    for c in range(x_ref.shape[0] // GMLP_CHUNK):
        sl = slice(c * GMLP_CHUNK, (c + 1) * GMLP_CHUNK)
        o_gmlp[sl, :] = _gmlp_chunk(uv[sl, :], gg_ref[...], ws_ref, bs_ref).astype(o_gmlp.dtype)
    for o_ref, (a, b) in zip(out_refs[:-1], _IN_PIECES[:-1]):
        o_ref[...] = _dot_nt(hb, w_ref[a:b, :])


def _inproj(x, mod3, wt, layer, gmlp_g, ws, bs_t, row_fn):
    t = x.shape[0]
    tm = 256
    assert t % tm == 0
    widths = [b - a for a, b in _IN_PIECES[:-1]]
    const = lambda shape: pl.BlockSpec(shape, lambda m: (0,) * len(shape))
    return pl.pallas_call(
        _inproj_kernel,
        grid=(t // tm,),
        in_specs=[
            pl.BlockSpec((tm, D_MODEL), lambda m: (m, 0)),
            _mod_spec(1, lambda m: row_fn(m * tm)),
            _mod_spec(0, lambda m: row_fn(m * tm)),
            pl.BlockSpec((None, GATE_COL0, D_MODEL), lambda m: (layer, 0, 0), pipeline_mode=pl.Buffered(1)),
            const((1, MIX_W)), const((4, 128, 128)), const((128, 4)),
        ],
        out_specs=[pl.BlockSpec((tm, D_MODEL), lambda m: (m, 0))]
                  + [pl.BlockSpec((tm, w), lambda m: (m, 0)) for w in widths]
                  + [pl.BlockSpec((tm, MIX_W), lambda m: (m, 0))],
        out_shape=[jax.ShapeDtypeStruct((t, D_MODEL), BF16)]
                  + [jax.ShapeDtypeStruct((t, w), F32) for w in widths]
                  + [jax.ShapeDtypeStruct((t, MIX_W), BF16)],
        compiler_params=_params("arbitrary"),
        name="inproj",
    )(x, mod3, mod3, wt, gmlp_g, ws, bs_t)


def _head_rms(t, gsum, gain):
    t2 = t * t
    hi, lo = _split2(t2)
    ss = _dot(hi, gsum) + _dot(lo, gsum)
    return t * lax.rsqrt(ss * (1.0 / HEAD_DIM) + EPS) * gain


def _rope(t, cos, sin_signed):
    w = t.shape[-1]
    up = pltpu.roll(t, w - 16, 1)
    dn = pltpu.roll(t, 16, 1)
    lane = lax.broadcasted_iota(jnp.int32, t.shape, 1)
    partner = jnp.where((lane & 31) < 16, up, dn)
    return t * cos + partner * sin_signed


ATT_GROUP = ATT_HEADS // ATT_KV_HEADS
GROUP_W = ATT_GROUP * HEAD_DIM


def _store_group_keys(k, kbd_scr, row0):
    s = k.shape[0]
    k2 = jnp.concatenate([k, k], axis=1)
    k2r = pltpu.roll(k2, HEAD_DIM, 1)
    lane = lax.broadcasted_iota(jnp.int32, k2.shape, 1)
    first = (lane & (2 * HEAD_DIM - 1)) < HEAD_DIM
    for j, tiled in enumerate((jnp.where(first, k2, k2r), jnp.where(first, k2r, k2))):
        for hh in range(ATT_GROUP):
            slab = jnp.where(lax.shift_right_logical(lane, 6) == hh, tiled, 0.0)
            kbd_scr[j, hh, row0:row0 + s, :] = slab.astype(kbd_scr.dtype)


def _attend_heads(q, kbd_scr, vt_scr, o_ref):
    qb = (q * (HEAD_DIM ** -0.5 * math.log2(math.e))).astype(BF16)
    s_len = kbd_scr.shape[2]
    for j in range(ATT_KV_HEADS):
        kbd = kbd_scr[j].reshape(ATT_GROUP * s_len, GROUP_W)
        s_all = _dot_nt(kbd, qb[:, j * GROUP_W:(j + 1) * GROUP_W])
        outs = []
        for hh in range(ATT_GROUP):
            s = s_all[hh * s_len:(hh + 1) * s_len, :]
            m = jnp.max(s, axis=0, keepdims=True)
            e = jnp.exp2(s - m)
            den = jnp.sum(e, axis=0, keepdims=True)
            outs.append(_dot(vt_scr[j * HEAD_DIM:(j + 1) * HEAD_DIM, :], e.astype(BF16)) / den)
        o_ref[:, j * GROUP_W:(j + 1) * GROUP_W] = jnp.concatenate(outs, axis=0).T.astype(o_ref.dtype)


def _attn_ctx_kernel(q_ref, kv_ref, qg_ref, kg_ref, gsum_ref, o_ref, kout_ref, vout_ref, kbd_scr, vt_scr):
    gsum = gsum_ref[...]
    kv = kv_ref[...]
    k = _head_rms(kv[:, :128], gsum[:128, :128], kg_ref[...])
    v = kv[:, 128:]
    kout_ref[...] = k
    vout_ref[...] = v
    _store_group_keys(k, kbd_scr, 0)
    vt_scr[...] = v.T.astype(BF16)
    q = _head_rms(q_ref[...], gsum, qg_ref[...])
    _attend_heads(q, kbd_scr, vt_scr, o_ref)


def _attn_lat_kernel(q_ref, kv_ref, qg_ref, kg_ref, gsum_ref, cq_ref, sq_ref, ck_ref, sk_ref,
                     cache_k_ref, cache_v_ref, o_ref, kbd_scr, vt_scr):
    gsum = gsum_ref[...]
    past = cache_k_ref.shape[0]

    @pl.when(pl.program_id(1) == 0)
    def _():
        kv = kv_ref[...]
        k = _head_rms(kv[:, :128], gsum[:128, :128], kg_ref[...])
        k = _rope(k, ck_ref[...], sk_ref[...])
        _store_group_keys(cache_k_ref[...], kbd_scr, 0)
        _store_group_keys(k, kbd_scr, past)
        vt_scr[:, :past] = cache_v_ref[...].T.astype(BF16)
        vt_scr[:, past:] = kv[:, 128:].T.astype(BF16)

    q = _head_rms(q_ref[...], gsum, qg_ref[...])
    q = _rope(q, cq_ref[...], sq_ref[...])
    _attend_heads(q, kbd_scr, vt_scr, o_ref)


def _attention_ctx(qkv, qg, kg, gsum, nb, seq):
    t = qkv.shape[0]
    const = lambda shape: pl.BlockSpec(shape, lambda b: (0, 0))
    return pl.pallas_call(
        _attn_ctx_kernel,
        grid=(nb,),
        in_specs=[
            pl.BlockSpec((seq, 512), lambda b: (b, 0)),
            pl.BlockSpec((seq, 256), lambda b: (b, 2)),
            const((1, 512)), const((1, 128)), const((512, 512)),
        ],
        out_specs=[
            pl.BlockSpec((seq, 512), lambda b: (b, 0)),
            pl.BlockSpec((seq, 128), lambda b: (b, 0)),
            pl.BlockSpec((seq, 128), lambda b: (b, 0)),
        ],
        out_shape=[
            jax.ShapeDtypeStruct((t, 512), BF16),
            jax.ShapeDtypeStruct((t, 128), F32),
            jax.ShapeDtypeStruct((t, 128), F32),
        ],
        scratch_shapes=[pltpu.VMEM((ATT_KV_HEADS, ATT_GROUP, seq, GROUP_W), BF16), pltpu.VMEM((128, seq), BF16)],
        compiler_params=_params("arbitrary"),
        name="attn_ctx",
    )(qkv, qkv, qg, kg, gsum)


def _attention_lat(qkv, qg, kg, gsum, rope_tabs, cache_k, cache_v, nb, seq):
    t = qkv.shape[0]
    qblk = 256
    nq = seq // qblk
    past = cache_k.shape[1]
    cq, sq, ck, sk = rope_tabs
    const = lambda shape: pl.BlockSpec(shape, lambda b, i: (0, 0))
    return pl.pallas_call(
        _attn_lat_kernel,
        grid=(nb, nq),
        in_specs=[
            pl.BlockSpec((qblk, 512), lambda b, i: (b * nq + i, 0)),
            pl.BlockSpec((seq, 256), lambda b, i: (b, 2)),
            const((1, 512)), const((1, 128)), const((512, 512)),
            pl.BlockSpec((qblk, 512), lambda b, i: (i, 0)),
            pl.BlockSpec((qblk, 512), lambda b, i: (i, 0)),
            const((seq, 128)), const((seq, 128)),
            pl.BlockSpec((None, past, 128), lambda b, i: (b, 0, 0)),
            pl.BlockSpec((None, past, 128), lambda b, i: (b, 0, 0)),
        ],
        out_specs=pl.BlockSpec((qblk, 512), lambda b, i: (b * nq + i, 0)),
        out_shape=jax.ShapeDtypeStruct((t, 512), BF16),
        scratch_shapes=[pltpu.VMEM((ATT_KV_HEADS, ATT_GROUP, past + seq, GROUP_W), BF16),
                        pltpu.VMEM((128, past + seq), BF16)],
        compiler_params=_params("arbitrary", "arbitrary"),
        name="attn_lat",
    )(qkv, qkv, qg, kg, gsum, cq, sq, ck, sk, cache_k, cache_v)


def _fourier_kernel(f_ref, bdh_ref, bdl_ref, clh_ref, cll_ref, slh_ref, sll_ref, o_ref, *, scale):
    xh, xl = _split2(f_ref[...])
    bdh = bdh_ref[...]
    bdl = bdl_ref[...]
    yc, ys = [], []
    for g in range(MIX_W // FNET_GW):
        gl = slice(g * FNET_GW, (g + 1) * FNET_GW)
        y = _dot(xh[:, gl], bdh) + _dot(xl[:, gl], bdh) + _dot(xh[:, gl], bdl)
        yc.append(y[:, :FNET_GW])
        ys.append(y[:, FNET_GW:])
    ych, ycl = _split2(jnp.concatenate(yc, axis=1))
    ysh, ysl = _split2(jnp.concatenate(ys, axis=1))
    clh = clh_ref[...]
    slh = slh_ref[...]
    re = _dot(clh, ych) + _dot(cll_ref[...], ych) + _dot(clh, ycl)
    im = _dot(slh, ysh) + _dot(sll_ref[...], ysh) + _dot(slh, ysl)
    o_ref[...] = ((re - im) * scale).astype(o_ref.dtype)


def _dft_consts(seq):
    def cs(n):
        k = np.arange(n, dtype=np.int64)
        ang = 2.0 * np.pi * ((k[:, None] * k[None, :]) % n).astype(np.float64) / n
        return np.cos(ang), np.sin(ang)

    bd = np.concatenate(cs(FNET_GW), axis=1)
    cl, sl_ = cs(seq)

    def hl(a):
        a32 = jnp.asarray(a, F32)
        hi = a32.astype(BF16)
        return hi, (a32 - hi.astype(F32)).astype(BF16)

    return hl(bd) + hl(cl) + hl(sl_)


def _fourier(f, nb, seq):
    t = f.shape[0]
    consts = _dft_consts(seq)
    const = lambda a: pl.BlockSpec(a.shape, lambda b: (0, 0))
    return pl.pallas_call(
        functools.partial(_fourier_kernel, scale=1.0 / math.sqrt(seq * FNET_GW)),
        grid=(nb,),
        in_specs=[pl.BlockSpec((seq, MIX_W), lambda b: (b, 0))] + [const(a) for a in consts],
        out_specs=pl.BlockSpec((seq, MIX_W), lambda b: (b, 0)),
        out_shape=jax.ShapeDtypeStruct((t, MIX_W), BF16),
        compiler_params=_params("arbitrary"),
        name="fourier",
    )(f, *consts)


def _ssd_kernel(*refs, seq, has_h0, emit_state, has_prev):
    (xbc_ref, z_ref, dt_ref, cw_ref, cb_ref, bias_ref, alog_ref, dskip_ref, ng_ref, tri_ref) = refs[:10]
    rest = list(refs[10:])
    h0_ref = rest.pop(0) if has_h0 else None
    sprev_ref = rest.pop(0) if has_prev else None
    o_ref = rest.pop(0)
    sfin_ref = rest.pop(0) if emit_state else None
    xt_scr, bc_scr, tab, tabt, yt_scr, s_scr = rest
    nc = seq // SSM_CHUNK
    q = SSM_CHUNK

    lane = lax.broadcasted_iota(jnp.int32, (q, 128), 1)
    ii = lax.broadcasted_iota(jnp.int32, (q, q), 0)
    jj = lax.broadcasted_iota(jnp.int32, (q, q), 1)
    below = ii > jj
    above = ii < jj
    row = lax.broadcasted_iota(jnp.int32, (q, 1), 0)
    cw = cw_ref[...]
    amul = jnp.where(lane < 16, 1.0, jnp.where(lane < 48, -jnp.exp(alog_ref[...]), 0.0))
    tri = tri_ref[...]

    if has_h0:
        s_scr[...] = h0_ref[...]
    else:
        s_scr[...] = jnp.zeros_like(s_scr)

    def prep(c, carry):
        r0 = pl.multiple_of(c * q, q)
        cur = xbc_ref[pl.ds(r0, q), :]
        prev8 = xbc_ref[pl.ds(pl.multiple_of(jnp.maximum(r0 - 8, 0), 8), 8), :]
        next8 = xbc_ref[pl.ds(pl.multiple_of(jnp.minimum(r0 + q, seq - 8), 8), 8), :]
        prev_row = jnp.where(c > 0, prev8[7:8, :], 0.0)
        next_row = jnp.where(c < nc - 1, next8[0:1, :], 0.0)
        up = jnp.where(row == 0, prev_row, pltpu.roll(cur, 1, 0))
        dn = jnp.where(row == q - 1, next_row, pltpu.roll(cur, q - 1, 0))
        conv = up * cw[0:1, :] + cur * cw[1:2, :] + dn * cw[2:3, :] + cb_ref[...]
        act = conv * _sigmoid(conv)
        xt_scr[c] = act[:, :SSM_D_INNER].T
        bc_scr[pl.ds(r0, q), :] = act[:, SSM_D_INNER:].astype(BF16)

        raw = dt_ref[pl.ds(r0, q), :]
        rep = jnp.where(lane < 16, raw, jnp.where(lane < 32, pltpu.roll(raw, 16, 1), pltpu.roll(raw, 32, 1)))
        v = rep + bias_ref[...]
        sp = jnp.maximum(v, 0.0) + jnp.log1p(jnp.exp(-jnp.abs(v)))
        t = sp * amul
        hi = t.astype(BF16)
        r1 = t - hi.astype(F32)
        mid = r1.astype(BF16)
        lo = (r1 - mid.astype(F32)).astype(BF16)
        cum = _dot(tri, hi) + _dot(tri, mid) + _dot(tri, lo)
        tc = jnp.where(lane < 32, t, cum)
        tab[pl.ds(r0, q), :] = tc
        tabt[c] = tc.T
        return carry

    lax.fori_loop(0, nc, prep, 0)

    def pass_fwd(c, carry):
        r0 = pl.multiple_of(c * q, q)
        pc = tab[pl.ds(r0, q), :]
        pt = tabt[c]
        for g in range(2):
            bg = bc_scr[pl.ds(r0, q), 128 * g:128 * (g + 1)]
            cg = bc_scr[pl.ds(r0, q), 256 + 128 * g:256 + 128 * (g + 1)]
            cb = _dot_nt(cg, bg)
            y_in = _dot_nt(s_scr[0, 256 * g:256 * (g + 1), :].astype(BF16), cg)
            for hh in range(4):
                h = 4 * g + hh
                acf_row = pt[32 + h:33 + h, :]
                eb_col = pc[:, 40 + h:41 + h] - pc[:, 24 + h:25 + h]
                eb_row = pt[40 + h:41 + h, :] - pt[24 + h:25 + h, :]
                dtf_row = pt[h:h + 1, :]
                dtb_row = pt[8 + h:9 + h, :]
                seg = jnp.where(below, pc[:, 32 + h:33 + h] - acf_row, jnp.where(above, eb_row - eb_col, 0.0))
                wgt = jnp.where(below, dtf_row, jnp.where(above, dtb_row, dtf_row + dtb_row))
                mp = cb * (jnp.exp(seg) * wgt)
                xth = xt_scr[c, 64 * h:64 * (h + 1), :]
                yd = _dot_nt(xth.astype(BF16), mp.astype(BF16))
                yo = y_in[64 * hh:64 * (hh + 1), :] * jnp.exp(acf_row)
                yt_scr[c, 64 * h:64 * (h + 1), :] = yd + yo + dskip_ref[h] * xth
                tot = pt[32 + h:33 + h, q - 1:q]
                wf_row = dtf_row * jnp.exp(tot - acf_row)
                st = _dot((xth * wf_row).astype(BF16), bg)
                s_scr[0, 64 * h:64 * (h + 1), :] = s_scr[0, 64 * h:64 * (h + 1), :] * jnp.exp(tot) + st
        return carry

    lax.fori_loop(0, nc, pass_fwd, 0)

    def pass_bwd(ci, carry):
        c = nc - 1 - ci
        r0 = pl.multiple_of(c * q, q)
        pt = tabt[c]
        for g in range(2):
            bg = bc_scr[pl.ds(r0, q), 128 * g:128 * (g + 1)]
            cg = bc_scr[pl.ds(r0, q), 256 + 128 * g:256 + 128 * (g + 1)]
            y_in = _dot_nt(s_scr[1, 256 * g:256 * (g + 1), :].astype(BF16), cg)
            for hh in range(4):
                h = 4 * g + hh
                eb_row = pt[40 + h:41 + h, :] - pt[24 + h:25 + h, :]
                tot = pt[40 + h:41 + h, q - 1:q]
                xth = xt_scr[c, 64 * h:64 * (h + 1), :]
                yt_scr[c, 64 * h:64 * (h + 1), :] += y_in[64 * hh:64 * (hh + 1), :] * jnp.exp(tot - eb_row)
                wb_row = pt[8 + h:9 + h, :] * jnp.exp(eb_row)
                st = _dot((xth * wb_row).astype(BF16), bg)
                s_scr[1, 64 * h:64 * (h + 1), :] = s_scr[1, 64 * h:64 * (h + 1), :] * jnp.exp(tot) + st
        zc = z_ref[pl.ds(r0, q), :]
        yg = yt_scr[c].T * (zc * _sigmoid(zc))
        ms = jnp.mean(yg * yg, axis=-1, keepdims=True)
        o_ref[pl.ds(r0, q), :] = (yg * lax.rsqrt(ms + EPS) * ng_ref[...]).astype(o_ref.dtype)
        return carry

    lax.fori_loop(0, nc, pass_bwd, 0)

    if emit_state and has_prev:
        sfin_ref[0] = sprev_ref[...]
        sfin_ref[1] = s_scr[...]
    elif emit_state:
        sfin_ref[...] = s_scr[...]


def _ssd(xbc, z, dt, conv_w, conv_b, bias128, alog128, dskip, norm_g, tri, h0, s_prev, nb, seq, emit_state):
    t = xbc.shape[0]
    has_h0 = h0 is not None
    has_prev = s_prev is not None
    const = lambda shape: pl.BlockSpec(shape, lambda b: (0,) * len(shape))
    in_specs = [
        pl.BlockSpec((seq, SSM_CONV_CH), lambda b: (b, 0)),
        pl.BlockSpec((seq, SSM_D_INNER), lambda b: (b, 0)),
        pl.BlockSpec((seq, 128), lambda b: (b, 0)),
        const((3, SSM_CONV_CH)), const((1, SSM_CONV_CH)), const((1, 128)), const((1, 128)),
        pl.BlockSpec(memory_space=pltpu.SMEM), const((1, SSM_D_INNER)), const((128, 128)),
    ]
    args = [xbc, z, dt, conv_w, conv_b, bias128, alog128, dskip, norm_g, tri]
    if has_h0:
        in_specs.append(pl.BlockSpec((None, 2, 512, 128), lambda b: (b, 0, 0, 0)))
        args.append(h0)
    if has_prev:
        in_specs.append(pl.BlockSpec((None, 2, 512, 128), lambda b: (b, 0, 0, 0)))
        args.append(s_prev)
    out_specs = [pl.BlockSpec((seq, SSM_D_INNER), lambda b: (b, 0))]
    out_shape = [jax.ShapeDtypeStruct((t, SSM_D_INNER), BF16)]
    if emit_state and has_prev:
        out_specs.append(pl.BlockSpec((None, 2, 2, 512, 128), lambda b: (b, 0, 0, 0, 0)))
        out_shape.append(jax.ShapeDtypeStruct((nb, 2, 2, 512, 128), F32))
    elif emit_state:
        out_specs.append(pl.BlockSpec((None, 2, 512, 128), lambda b: (b, 0, 0, 0)))
        out_shape.append(jax.ShapeDtypeStruct((nb, 2, 512, 128), F32))
    return pl.pallas_call(
        functools.partial(_ssd_kernel, seq=seq, has_h0=has_h0, emit_state=emit_state, has_prev=has_prev),
        grid=(nb,),
        in_specs=in_specs,
        out_specs=out_specs,
        out_shape=out_shape,
        scratch_shapes=[
            pltpu.VMEM((seq // SSM_CHUNK, SSM_D_INNER, SSM_CHUNK), F32),
            pltpu.VMEM((seq, 512), BF16),
            pltpu.VMEM((seq, 128), F32),
            pltpu.VMEM((seq // SSM_CHUNK, 128, SSM_CHUNK), F32),
            pltpu.VMEM((seq // SSM_CHUNK, SSM_D_INNER, SSM_CHUNK), F32),
            pltpu.VMEM((2, 512, 128), F32),
        ],
        compiler_params=_params("arbitrary"),
        name="ssd",
    )(*args)


def _merge_kernel(x_ref, h_ref, g_ref, b0, b1, b2, b3, wg0, wg1, wg2, wg3, wb_ref, wo_ref, o_ref):
    n = pl.program_id(1)

    @pl.when(n == 0)
    def _():
        o_ref[...] = jnp.zeros_like(o_ref)

    h = h_ref[...]
    merged = None
    for k, (b_ref, wg_ref) in enumerate(((b0, wg0), (b1, wg1), (b2, wg2), (b3, wg3))):
        gate = _sigmoid(_dot_nt(h, wg_ref[0]))
        term = gate * _dot(b_ref[...], wb_ref[k])
        merged = term if merged is None else merged + term
    o_ref[...] += _dot(merged.astype(BF16), wo_ref[...])

    @pl.when(n == pl.num_programs(1) - 1)
    def _():
        o_ref[...] = x_ref[...] + g_ref[...] * o_ref[...]


def _merge(x, h, mod3, branches, wt, w_branch, w_out, layer, row_fn):
    t = x.shape[0]
    tm, tn = 512, 512
    assert t % tm == 0
    nn = D_MODEL // tn
    rf = lambda m: row_fn(m * tm)
    gate_spec = lambda k: pl.BlockSpec(
        (pl.Element(1), pl.Element(tn), pl.Element(D_MODEL)),
        lambda m, n: (layer, pl.multiple_of(GATE_COL0 + k * D_MODEL + n * tn, 16), 0))
    return pl.pallas_call(
        _merge_kernel,
        grid=(t // tm, nn),
        in_specs=[
            pl.BlockSpec((tm, D_MODEL), lambda m, n: (m, 0)),
            pl.BlockSpec((tm, D_MODEL), lambda m, n: (m, 0)),
            _mod_spec(2, rf),
        ] + [pl.BlockSpec((tm, MIX_W), lambda m, n: (m, 0))] * 4
          + [gate_spec(k) for k in range(4)] + [
            pl.BlockSpec((None, N_BRANCH, MIX_W, tn), lambda m, n: (layer, 0, 0, n)),
            pl.BlockSpec((None, tn, D_MODEL), lambda m, n: (layer, n, 0)),
        ],
        out_specs=pl.BlockSpec((tm, D_MODEL), lambda m, n: (m, 0)),
        out_shape=jax.ShapeDtypeStruct((t, D_MODEL), F32),
        compiler_params=_params("arbitrary", "arbitrary"),
        name="merge",
    )(x, h, mod3, *branches, wt, wt, wt, wt, w_branch, w_out)


def _ffn_kernel(x_ref, sc_ref, sh_ref, g_ref, w1_ref, w2_ref, o_ref, h_scr):
    n = pl.program_id(1)

    @pl.when(n == 0)
    def _():
        _rms_mod_rows(x_ref, sc_ref, sh_ref, h_scr)
        o_ref[...] = jnp.zeros_like(o_ref)

    a = jnp.maximum(_dot(h_scr[...], w1_ref[...].astype(BF16)), 0.0)
    o_ref[...] += _dot((a * a).astype(BF16), w2_ref[...].astype(BF16))

    @pl.when(n == pl.num_programs(1) - 1)
    def _():
        o_ref[...] = x_ref[...] + g_ref[...] * o_ref[...]


def _ffn(x, mod3, w1, w2, layer, row_fn):
    t = x.shape[0]
    tm, tf = 1024, 512
    assert t % tm == 0
    rf = lambda m: row_fn(m * tm)
    return pl.pallas_call(
        _ffn_kernel,
        grid=(t // tm, D_FF // tf),
        in_specs=[
            pl.BlockSpec((tm, D_MODEL), lambda m, n: (m, 0), pipeline_mode=pl.Buffered(1)),
            _mod_spec(4, rf), _mod_spec(3, rf), _mod_spec(5, rf),
            pl.BlockSpec((None, D_MODEL, tf), lambda m, n: (layer, 0, n)),
            pl.BlockSpec((None, tf, D_MODEL), lambda m, n: (layer, n, 0)),
        ],
        out_specs=pl.BlockSpec((tm, D_MODEL), lambda m, n: (m, 0)),
        out_shape=jax.ShapeDtypeStruct((t, D_MODEL), F32),
        scratch_shapes=[pltpu.VMEM((tm, D_MODEL), BF16)],
        compiler_params=_params("arbitrary", "arbitrary"),
        name="ffn",
    )(x, mod3, mod3, mod3, w1, w2)


def _rope_tables(seq):
    half = 16
    freqs = ROPE_THETA ** (-jnp.arange(half, dtype=F32) / half)
    n_rows = seq // GRID_W
    rows = jnp.repeat(jnp.arange(n_rows), GRID_W).astype(F32)
    cols = jnp.tile(jnp.arange(GRID_W), n_rows).astype(F32)
    ar = rows[:, None] * freqs[None, :]
    ac = cols[:, None] * freqs[None, :]
    cos64 = jnp.concatenate([jnp.cos(ar), jnp.cos(ar), jnp.cos(ac), jnp.cos(ac)], axis=-1)
    sin64 = jnp.concatenate([-jnp.sin(ar), jnp.sin(ar), -jnp.sin(ac), jnp.sin(ac)], axis=-1)
    return (jnp.tile(cos64, (1, ATT_HEADS)), jnp.tile(sin64, (1, ATT_HEADS)),
            jnp.tile(cos64, (1, ATT_KV_HEADS)), jnp.tile(sin64, (1, ATT_KV_HEADS)))


def _layer_path(x, mod3, lw, row_fn, nb, seq, lat, s_prev=None):
    (wt, qg, kg, gsum, conv_w, conv_b, bias128, alog128, dskip, ssm_g, tri, gmlp_g, ws, bs_t,
     w_branch, w_out, w_ff1, w_ff2, layer) = lw
    h, qkv, f_in, xbc, z, dt, o_gmlp = _inproj(x, mod3, wt, layer, gmlp_g, ws, bs_t, row_fn)
    if lat is None:
        o_att, k_new, v_new = _attention_ctx(qkv, qg, kg, gsum, nb, seq)
        h0 = None
    else:
        rope_tabs, cache_k, cache_v, h0 = lat
        o_att = _attention_lat(qkv, qg, kg, gsum, rope_tabs, cache_k, cache_v, nb, seq)
        k_new = v_new = None
    o_fnet = _fourier(f_in, nb, seq)
    ssd_out = _ssd(xbc, z, dt, conv_w, conv_b, bias128, alog128, dskip, ssm_g, tri, h0, s_prev, nb, seq,
                   emit_state=lat is None)
    o_ssm = ssd_out[0]
    s_fin = ssd_out[1] if lat is None else None
    x = _merge(x, h, mod3, (o_att, o_fnet, o_ssm, o_gmlp), wt, w_branch, w_out, layer, row_fn)
    x = _ffn(x, mod3, w_ff1, w_ff2, layer, row_fn)
    return x, k_new, v_new, s_fin


def kernel(x_prompt, x_sample, c, cache_k, cache_v, state_ssm, c_ctx, w_mod, b_mod, w_in, q_norm_g, k_norm_g,
           conv_w, conv_b, a_log, dt_bias, d_skip, ssm_norm_g, gmlp_norm_g, w_spatial, b_spatial, w_branch,
           w_out, w_ff1, w_ff2):
    nb_ctx, seq_ctx, _ = x_prompt.shape
    nb_lat, seq_lat, _ = x_sample.shape
    assert seq_lat == LAT_LEN and 1 + nb_lat <= 8

    cond8 = jnp.zeros((8, D_MODEL), F32).at[0].set(c_ctx).at[1:1 + nb_lat].set(c)
    mod = _modulation(cond8, w_mod, b_mod)

    gsum = jnp.asarray(np.kron(np.eye(ATT_HEADS), np.ones((HEAD_DIM, HEAD_DIM))), BF16)
    tri = jnp.asarray(np.tril(np.ones((SSM_CHUNK, SSM_CHUNK))), BF16)
    rope_tabs = _rope_tables(seq_lat)

    xp = x_prompt.reshape(nb_ctx * seq_ctx, D_MODEL)
    xs = x_sample.reshape(nb_lat * seq_lat, D_MODEL)
    ctx_row = lambda tok: 0
    lat_row = lambda tok: 1 + tok // LAT_LEN

    wt = _cast_bf16(jnp.swapaxes(w_in, 1, 2))
    w_branch = _cast_bf16(w_branch)
    w_out = _cast_bf16(w_out)

    assert DEPTH == 2
    new_k, new_v, s_l = [], [], None
    for l in range(DEPTH):
        zeros80 = jnp.zeros((128 - 48,), F32)
        bias128 = jnp.concatenate([jnp.tile(dt_bias[l].reshape(16), 3), zeros80]).reshape(1, 128)
        alog128 = jnp.concatenate([jnp.zeros((16,), F32), jnp.tile(a_log[l].reshape(16), 2), zeros80]).reshape(1, 128)
        lw = (
            wt,
            jnp.tile(q_norm_g[l], ATT_HEADS).reshape(1, 512), jnp.tile(k_norm_g[l], ATT_KV_HEADS).reshape(1, 128),
            gsum, conv_w[l], conv_b[l].reshape(1, SSM_CONV_CH), bias128, alog128,
            d_skip[l], ssm_norm_g[l].reshape(1, SSM_D_INNER), tri,
            gmlp_norm_g[l].reshape(1, MIX_W), w_spatial[l].astype(BF16), b_spatial[l].T,
            w_branch, w_out, w_ff1, w_ff2, l,
        )
        mod3 = mod[l].reshape(8, 1, 6 * D_MODEL)
        xp, k_l, v_l, s_l = _layer_path(xp, mod3, lw, ctx_row, nb_ctx, seq_ctx, None, s_prev=s_l)
        new_k.append(k_l.reshape(nb_ctx, seq_ctx, ATT_KV_HEADS, HEAD_DIM))
        new_v.append(v_l.reshape(nb_ctx, seq_ctx, ATT_KV_HEADS, HEAD_DIM))
        lat = (rope_tabs,
               cache_k[:, l].reshape(nb_lat, -1, ATT_KV_HEADS * HEAD_DIM),
               cache_v[:, l].reshape(nb_lat, -1, ATT_KV_HEADS * HEAD_DIM),
               state_ssm[:, l].reshape(nb_lat, 2, SSM_HEADS * SSM_HEAD_DIM, SSM_STATE))
        xs, _, _, _ = _layer_path(xs, mod3, lw, lat_row, nb_lat, seq_lat, lat)

    return (xp.reshape(nb_ctx, seq_ctx, D_MODEL), xs.reshape(nb_lat, seq_lat, D_MODEL),
            jnp.stack(new_k, axis=1), jnp.stack(new_v, axis=1),
            s_l.reshape(nb_ctx, DEPTH, 2, SSM_HEADS, SSM_HEAD_DIM, SSM_STATE))
```

```python
import functools
import math

import numpy as np
import jax
import jax.numpy as jnp
from jax import lax
from jax.experimental import pallas as pl
from jax.experimental.pallas import tpu as pltpu

F32 = jnp.float32
BF16 = jnp.bfloat16

D_MODEL = 2048
DEPTH = 2
GRID_W = 64
MIX_W = 512
N_BRANCH = 4
ATT_HEADS = 8
ATT_KV_HEADS = 2
HEAD_DIM = 64
ROPE_THETA = 10000.0
FNET_GW = 128
SSM_HEADS = 8
SSM_HEAD_DIM = 64
SSM_STATE = 128
SSM_CHUNK = 128
SSM_D_INNER = 512
SSM_CONV_CH = 1024
GMLP_CHUNK = 128
D_FF = 4 * D_MODEL
EPS = 1e-6
LAT_LEN = 1024

QKV_W = 768
COL_F = 768
COL_XBC = 1280
COL_Z = 2304
COL_DT = 2816
COL_UV = 2832
GATE_COL0 = 3856
N_IN = GATE_COL0 + N_BRANCH * D_MODEL
_IN_PIECES = ((0, QKV_W), (COL_F, COL_XBC), (COL_XBC, COL_Z), (COL_Z, COL_DT),
              (COL_DT, COL_DT + 128),
              (COL_UV, GATE_COL0))

VMEM_LIMIT = 56 * 1024 * 1024
NEG_BIG = -1e30


def _params(*sem):
    return pltpu.CompilerParams(dimension_semantics=sem, vmem_limit_bytes=VMEM_LIMIT)


def _dot(a, b):
    return jnp.dot(a, b, preferred_element_type=F32)


def _dot_nt(a, b):
    return lax.dot_general(a, b, (((1,), (1,)), ((), ())), preferred_element_type=F32)


def _dot_tn(a, b):
    return lax.dot_general(a, b, (((0,), (0,)), ((), ())), preferred_element_type=F32)


def _sigmoid(x):
    return 1.0 / (1.0 + jnp.exp(-x))


def _split2(x):
    hi = x.astype(BF16)
    lo = (x - hi.astype(F32)).astype(BF16)
    return hi, lo


def _rms_mod(x, sc, sh):
    ms = jnp.mean(x * x, axis=-1, keepdims=True)
    return x * lax.rsqrt(ms + EPS) * (1.0 + sc) + sh


def _rms_mod_rows(x_ref, sc_ref, sh_ref, dst_ref):
    slab = 16

    def body(i, carry):
        rows = pl.ds(pl.multiple_of(i * slab, slab), slab)
        dst_ref[rows, :] = _rms_mod(x_ref[rows, :], sc_ref[...], sh_ref[...]).astype(dst_ref.dtype)
        return carry

    lax.fori_loop(0, x_ref.shape[0] // slab, body, 0, unroll=8)


def _mod_kernel(cond_ref, w_ref, b_ref, o_ref):
    c = cond_ref[...]
    s = c * _sigmoid(c)
    o_ref[...] = _dot(s.astype(BF16), w_ref[...].astype(BF16)) + b_ref[...]


def _modulation(cond8, w_mod, b_mod):
    tn = 1024
    n6 = 6 * D_MODEL
    return pl.pallas_call(
        _mod_kernel,
        grid=(DEPTH, n6 // tn),
        in_specs=[
            pl.BlockSpec((8, D_MODEL), lambda l, n: (0, 0)),
            pl.BlockSpec((None, D_MODEL, tn), lambda l, n: (l, 0, n)),
            pl.BlockSpec((None, 1, tn), lambda l, n: (l, 0, n)),
        ],
        out_specs=pl.BlockSpec((None, 8, tn), lambda l, n: (l, 0, n)),
        out_shape=jax.ShapeDtypeStruct((DEPTH, 8, n6), F32),
        compiler_params=_params("arbitrary", "arbitrary"),
        name="modulation",
    )(cond8, w_mod, b_mod.reshape(DEPTH, 1, n6))


def _mod_spec(chunk, row_fn):
    return pl.BlockSpec((None, 1, D_MODEL), lambda m, *_: (row_fn(m), 0, chunk))


def _cast_kernel(w_ref, o_ref):
    o_ref[...] = w_ref[...].astype(o_ref.dtype)


def _cast_bf16(w):
    flat = w.reshape(-1, D_MODEL)
    rows = 1024
    out = pl.pallas_call(
        _cast_kernel,
        grid=(pl.cdiv(flat.shape[0], rows),),
        in_specs=[pl.BlockSpec((rows, D_MODEL), lambda r: (r, 0))],
        out_specs=pl.BlockSpec((rows, D_MODEL), lambda r: (r, 0)),
        out_shape=jax.ShapeDtypeStruct(flat.shape, BF16),
        compiler_params=_params("arbitrary"),
        name="cast_bf16",
    )(flat)
    return out.reshape(w.shape)


def _gmlp_chunk(x, gain, ws_ref, bs_ref):
    c0 = math.sqrt(2.0 / math.pi)
    ge = x * (0.5 * (1.0 + jnp.tanh(c0 * (x + 0.044715 * (x * x * x)))))
    u = ge[:, :MIX_W]
    v = ge[:, MIX_W:]
    v = v * lax.rsqrt(jnp.mean(v * v, axis=-1, keepdims=True) + EPS) * gain
    outs = []
    for g in range(MIX_W // 128):
        gl = slice(g * 128, (g + 1) * 128)
        mixed = _dot(ws_ref[g], v[:, gl].astype(BF16)) + bs_ref[:, g:g + 1]
        outs.append(u[:, gl] * mixed)
    return jnp.concatenate(outs, axis=1)


def _inproj_kernel(x_ref, sc_ref, sh_ref, w_ref, gg_ref, ws_ref, bs_ref, h_ref, *out_refs):
    hb = _rms_mod(x_ref[...], sc_ref[...], sh_ref[...]).astype(BF16)
    h_ref[...] = hb
    o_gmlp = out_refs[-1]
    a, b = _IN_PIECES[-1]
    uv = _dot_nt(hb, w_ref[a:b, :])
    for c in range(x_ref.shape[0] // GMLP_CHUNK):
        sl = slice(c * GMLP_CHUNK, (c + 1) * GMLP_CHUNK)
        o_gmlp[sl, :] = _gmlp_chunk(uv[sl, :], gg_ref[...], ws_ref, bs_ref).astype(o_gmlp.dtype)
    for o_ref, (a, b) in zip(out_refs[:-1], _IN_PIECES[:-1]):
        o_ref[...] = _dot_nt(hb, w_ref[a:b, :])


def _inproj(x, mod3, wt, layer, gmlp_g, ws, bs_t, row_fn):
    t = x.shape[0]
    tm = 256
    assert t % tm == 0
    widths = [b - a for a, b in _IN_PIECES[:-1]]
    const = lambda shape: pl.BlockSpec(shape, lambda m: (0,) * len(shape))
    return pl.pallas_call(
        _inproj_kernel,
        grid=(t // tm,),
        in_specs=[
            pl.BlockSpec((tm, D_MODEL), lambda m: (m, 0)),
            _mod_spec(1, lambda m: row_fn(m * tm)),
            _mod_spec(0, lambda m: row_fn(m * tm)),
            pl.BlockSpec((None, GATE_COL0, D_MODEL), lambda m: (layer, 0, 0), pipeline_mode=pl.Buffered(1)),
            const((1, MIX_W)), const((4, 128, 128)), const((128, 4)),
        ],
        out_specs=[pl.BlockSpec((tm, D_MODEL), lambda m: (m, 0))]
                  + [pl.BlockSpec((tm, w), lambda m: (m, 0)) for w in widths]
                  + [pl.BlockSpec((tm, MIX_W), lambda m: (m, 0))],
        out_shape=[jax.ShapeDtypeStruct((t, D_MODEL), BF16)]
                  + [jax.ShapeDtypeStruct((t, w), F32) for w in widths]
                  + [jax.ShapeDtypeStruct((t, MIX_W), BF16)],
        compiler_params=_params("arbitrary"),
        name="inproj",
    )(x, mod3, mod3, wt, gmlp_g, ws, bs_t)


def _head_rms(t, gsum, gain):
    t2 = t * t
    hi, lo = _split2(t2)
    ss = _dot(hi, gsum) + _dot(lo, gsum)
    return t * lax.rsqrt(ss * (1.0 / HEAD_DIM) + EPS) * gain


def _rope(t, cos, sin_signed):
    w = t.shape[-1]
    up = pltpu.roll(t, w - 16, 1)
    dn = pltpu.roll(t, 16, 1)
    lane = lax.broadcasted_iota(jnp.int32, t.shape, 1)
    partner = jnp.where((lane & 31) < 16, up, dn)
    return t * cos + partner * sin_signed


ATT_GROUP = ATT_HEADS // ATT_KV_HEADS
GROUP_W = ATT_GROUP * HEAD_DIM


def _store_group_keys(k, kbd_scr, row0):
    s = k.shape[0]
    k2 = jnp.concatenate([k, k], axis=1)
    k2r = pltpu.roll(k2, HEAD_DIM, 1)
    lane = lax.broadcasted_iota(jnp.int32, k2.shape, 1)
    first = (lane & (2 * HEAD_DIM - 1)) < HEAD_DIM
    for j, tiled in enumerate((jnp.where(first, k2, k2r), jnp.where(first, k2r, k2))):
        for hh in range(ATT_GROUP):
            slab = jnp.where(lax.shift_right_logical(lane, 6) == hh, tiled, 0.0)
            kbd_scr[j, hh, row0:row0 + s, :] = slab.astype(kbd_scr.dtype)


def _attend_heads(q, kbd_scr, vt_scr, o_ref):
    qb = (q * (HEAD_DIM ** -0.5 * math.log2(math.e))).astype(BF16)
    s_len = kbd_scr.shape[2]
    for j in range(ATT_KV_HEADS):
        kbd = kbd_scr[j].reshape(ATT_GROUP * s_len, GROUP_W)
        s_all = _dot_nt(kbd, qb[:, j * GROUP_W:(j + 1) * GROUP_W])
        outs = []
        for hh in range(ATT_GROUP):
            s = s_all[hh * s_len:(hh + 1) * s_len, :]
            m = jnp.max(s, axis=0, keepdims=True)
            e = jnp.exp2(s - m)
            den = jnp.sum(e, axis=0, keepdims=True)
            outs.append(_dot(vt_scr[j * HEAD_DIM:(j + 1) * HEAD_DIM, :], e.astype(BF16)) / den)
        o_ref[:, j * GROUP_W:(j + 1) * GROUP_W] = jnp.concatenate(outs, axis=0).T.astype(o_ref.dtype)


def _attn_ctx_kernel(q_ref, kv_ref, qg_ref, kg_ref, gsum_ref, o_ref, kout_ref, vout_ref, kbd_scr, vt_scr):
    gsum = gsum_ref[...]
    seq = vt_scr.shape[-1]
    for i in range(vt_scr.shape[0]):
        rows = pl.ds(i * seq, seq)
        kv = kv_ref[rows, :]
        k = _head_rms(kv[:, :128], gsum[:128, :128], kg_ref[...])
        v = kv[:, 128:]
        kout_ref[rows, :] = k
        vout_ref[rows, :] = v
        _store_group_keys(k, kbd_scr.at[i], 0)
        vt_scr[i] = v.T.astype(BF16)
        q = _head_rms(q_ref[rows, :], gsum, qg_ref[...])
        _attend_heads(q, kbd_scr.at[i], vt_scr.at[i], o_ref.at[rows])


def _attn_lat_kernel(q_ref, kv_ref, qg_ref, kg_ref, gsum_ref, cq_ref, sq_ref, ck_ref, sk_ref,
                     cache_k_ref, cache_v_ref, o_ref, kbd_scr, vt_scr):
    gsum = gsum_ref[...]
    past = cache_k_ref.shape[0]

    @pl.when(pl.program_id(1) == 0)
    def _():
        kv = kv_ref[...]
        k = _head_rms(kv[:, :128], gsum[:128, :128], kg_ref[...])
        k = _rope(k, ck_ref[...], sk_ref[...])
        _store_group_keys(cache_k_ref[...], kbd_scr, 0)
        _store_group_keys(k, kbd_scr, past)
        vt_scr[:, :past] = cache_v_ref[...].T.astype(BF16)
        vt_scr[:, past:] = kv[:, 128:].T.astype(BF16)

    q = _head_rms(q_ref[...], gsum, qg_ref[...])
    q = _rope(q, cq_ref[...], sq_ref[...])
    _attend_heads(q, kbd_scr, vt_scr, o_ref)


def _attention_ctx(qkv, qg, kg, gsum, nb, seq):
    t = qkv.shape[0]
    per = 2 if nb % 2 == 0 else 1
    rows = per * seq
    const = lambda shape: pl.BlockSpec(shape, lambda b: (0, 0))
    return pl.pallas_call(
        _attn_ctx_kernel,
        grid=(nb // per,),
        in_specs=[
            pl.BlockSpec((rows, 512), lambda b: (b, 0)),
            pl.BlockSpec((rows, 256), lambda b: (b, 2)),
            const((1, 512)), const((1, 128)), const((512, 512)),
        ],
        out_specs=[
            pl.BlockSpec((rows, 512), lambda b: (b, 0)),
            pl.BlockSpec((rows, 128), lambda b: (b, 0)),
            pl.BlockSpec((rows, 128), lambda b: (b, 0)),
        ],
        out_shape=[
            jax.ShapeDtypeStruct((t, 512), BF16),
            jax.ShapeDtypeStruct((t, 128), F32),
            jax.ShapeDtypeStruct((t, 128), F32),
        ],
        scratch_shapes=[pltpu.VMEM((per, ATT_KV_HEADS, ATT_GROUP, seq, GROUP_W), BF16),
                        pltpu.VMEM((per, 128, seq), BF16)],
        compiler_params=_params("arbitrary"),
        name="attn_ctx",
    )(qkv, qkv, qg, kg, gsum)


def _attention_lat(qkv, qg, kg, gsum, rope_tabs, cache_k, cache_v, nb, seq):
    t = qkv.shape[0]
    qblk = 256
    nq = seq // qblk
    past = cache_k.shape[1]
    cq, sq, ck, sk = rope_tabs
    const = lambda shape: pl.BlockSpec(shape, lambda b, i: (0, 0))
    return pl.pallas_call(
        _attn_lat_kernel,
        grid=(nb, nq),
        in_specs=[
            pl.BlockSpec((qblk, 512), lambda b, i: (b * nq + i, 0)),
            pl.BlockSpec((seq, 256), lambda b, i: (b, 2)),
            const((1, 512)), const((1, 128)), const((512, 512)),
            pl.BlockSpec((qblk, 512), lambda b, i: (i, 0)),
            pl.BlockSpec((qblk, 512), lambda b, i: (i, 0)),
            const((seq, 128)), const((seq, 128)),
            pl.BlockSpec((None, past, 128), lambda b, i: (b, 0, 0)),
            pl.BlockSpec((None, past, 128), lambda b, i: (b, 0, 0)),
        ],
        out_specs=pl.BlockSpec((qblk, 512), lambda b, i: (b * nq + i, 0)),
        out_shape=jax.ShapeDtypeStruct((t, 512), BF16),
        scratch_shapes=[pltpu.VMEM((ATT_KV_HEADS, ATT_GROUP, past + seq, GROUP_W), BF16),
                        pltpu.VMEM((128, past + seq), BF16)],
        compiler_params=_params("arbitrary", "arbitrary"),
        name="attn_lat",
    )(qkv, qkv, qg, kg, gsum, cq, sq, ck, sk, cache_k, cache_v)


def _fourier_kernel(f_ref, bdh_ref, bdl_ref, cl_ref, sl_ref, o_ref, *, scale):
    bdh = bdh_ref[...]
    bdl = bdl_ref[...]
    cl = cl_ref[...].astype(BF16)
    sl = sl_ref[...].astype(BF16)
    seq = cl_ref.shape[0]
    for i in range(f_ref.shape[0] // seq):
        rows = pl.ds(i * seq, seq)
        xh, xl = _split2(f_ref[rows, :])
        yc, ys = [], []
        for g in range(MIX_W // FNET_GW):
            gl = slice(g * FNET_GW, (g + 1) * FNET_GW)
            y = _dot(xh[:, gl], bdh) + _dot(xl[:, gl], bdh) + _dot(xh[:, gl], bdl)
            yc.append(y[:, :FNET_GW])
            ys.append(y[:, FNET_GW:])
        re = _dot(cl, jnp.concatenate(yc, axis=1).astype(BF16))
        im = _dot(sl, jnp.concatenate(ys, axis=1).astype(BF16))
        o_ref[rows, :] = ((re - im) * scale).astype(o_ref.dtype)


def _dft_consts(seq):
    def cs(n):
        k = np.arange(n, dtype=np.int64)
        ang = 2.0 * np.pi * ((k[:, None] * k[None, :]) % n).astype(np.float64) / n
        return np.cos(ang), np.sin(ang)

    bd = np.concatenate(cs(FNET_GW), axis=1)
    cl, sl_ = cs(seq)

    def hl(a):
        a32 = jnp.asarray(a, F32)
        hi = a32.astype(BF16)
        return hi, (a32 - hi.astype(F32)).astype(BF16)

    return hl(bd) + (jnp.asarray(cl, F32), jnp.asarray(sl_, F32))


def _fourier(f, nb, seq):
    t = f.shape[0]
    consts = _dft_consts(seq)
    const = lambda a: pl.BlockSpec(a.shape, lambda b: (0, 0))
    per = 4 if (seq <= 256 and nb % 4 == 0) else 1
    return pl.pallas_call(
        functools.partial(_fourier_kernel, scale=1.0 / math.sqrt(seq * FNET_GW)),
        grid=(nb // per,),
        in_specs=[pl.BlockSpec((per * seq, MIX_W), lambda b: (b, 0))] + [const(a) for a in consts],
        out_specs=pl.BlockSpec((per * seq, MIX_W), lambda b: (b, 0)),
        out_shape=jax.ShapeDtypeStruct((t, MIX_W), BF16),
        compiler_params=_params("arbitrary"),
        name="fourier",
    )(f, *consts)


def _ssd_kernel(*refs, seq, has_h0, emit_state, has_prev):
    (xbc_ref, z_ref, dt_ref, cw_ref, cb_ref, bias_ref, alog_ref, dskip_ref, ng_ref, tri_ref) = refs[:10]
    rest = list(refs[10:])
    h0_ref = rest.pop(0) if has_h0 else None
    sprev_ref = rest.pop(0) if has_prev else None
    o_ref = rest.pop(0)
    sfin_ref = rest.pop(0) if emit_state else None
    xt_scr, bc_scr, tab, tabt, yt_scr, s_scr = rest
    nc = seq // SSM_CHUNK
    q = SSM_CHUNK

    lane = lax.broadcasted_iota(jnp.int32, (q, 128), 1)
    ii = lax.broadcasted_iota(jnp.int32, (q, q), 0)
    jj = lax.broadcasted_iota(jnp.int32, (q, q), 1)
    below = ii > jj
    above = ii < jj
    row = lax.broadcasted_iota(jnp.int32, (q, 1), 0)
    cw = cw_ref[...]
    amul = jnp.where(lane < 16, 1.0, jnp.where(lane < 48, -jnp.exp(alog_ref[...]), 0.0))
    tri = tri_ref[...]

    if has_h0:
        s_scr[...] = h0_ref[...]
    else:
        s_scr[...] = jnp.zeros_like(s_scr)

    def prep(c, carry):
        r0 = pl.multiple_of(c * q, q)
        cur = xbc_ref[pl.ds(r0, q), :]
        prev8 = xbc_ref[pl.ds(pl.multiple_of(jnp.maximum(r0 - 8, 0), 8), 8), :]
        next8 = xbc_ref[pl.ds(pl.multiple_of(jnp.minimum(r0 + q, seq - 8), 8), 8), :]
        prev_row = jnp.where(c > 0, prev8[7:8, :], 0.0)
        next_row = jnp.where(c < nc - 1, next8[0:1, :], 0.0)
        up = jnp.where(row == 0, prev_row, pltpu.roll(cur, 1, 0))
        dn = jnp.where(row == q - 1, next_row, pltpu.roll(cur, q - 1, 0))
        conv = up * cw[0:1, :] + cur * cw[1:2, :] + dn * cw[2:3, :] + cb_ref[...]
        act = conv * _sigmoid(conv)
        xt_scr[c] = act[:, :SSM_D_INNER].T
        bc_scr[pl.ds(r0, q), :] = act[:, SSM_D_INNER:].astype(BF16)

        raw = dt_ref[pl.ds(r0, q), :]
        rep = jnp.where(lane < 16, raw, jnp.where(lane < 32, pltpu.roll(raw, 16, 1), pltpu.roll(raw, 32, 1)))
        v = rep + bias_ref[...]
        sp = jnp.maximum(v, 0.0) + jnp.log1p(jnp.exp(-jnp.abs(v)))
        t = sp * amul
        hi = t.astype(BF16)
        r1 = t - hi.astype(F32)
        mid = r1.astype(BF16)
        lo = (r1 - mid.astype(F32)).astype(BF16)
        cum = _dot(tri, hi) + _dot(tri, mid) + _dot(tri, lo)
        tc = jnp.where(lane < 32, t, cum)
        tab[pl.ds(r0, q), :] = tc
        tabt[c] = tc.T
        return carry

    lax.fori_loop(0, nc, prep, 0, unroll=nc)

    def pass_fwd(c, carry):
        r0 = pl.multiple_of(c * q, q)
        pc = tab[pl.ds(r0, q), :]
        pt = tabt[c]
        for g in range(2):
            bg = bc_scr[pl.ds(r0, q), 128 * g:128 * (g + 1)]
            cg = bc_scr[pl.ds(r0, q), 256 + 128 * g:256 + 128 * (g + 1)]
            cb = _dot_nt(cg, bg)
            y_in = _dot_nt(s_scr[0, 256 * g:256 * (g + 1), :].astype(BF16), cg)
            for hh in range(4):
                h = 4 * g + hh
                acf_row = pt[32 + h:33 + h, :]
                eb_col = pc[:, 40 + h:41 + h] - pc[:, 24 + h:25 + h]
                eb_row = pt[40 + h:41 + h, :] - pt[24 + h:25 + h, :]
                dtf_row = pt[h:h + 1, :]
                dtb_row = pt[8 + h:9 + h, :]
                seg = jnp.where(below, pc[:, 32 + h:33 + h] - acf_row, jnp.where(above, eb_row - eb_col, 0.0))
                wgt = jnp.where(below, dtf_row, jnp.where(above, dtb_row, dtf_row + dtb_row))
                mp = cb * (jnp.exp(seg) * wgt)
                xth = xt_scr[c, 64 * h:64 * (h + 1), :]
                yd = _dot_nt(xth.astype(BF16), mp.astype(BF16))
                yo = y_in[64 * hh:64 * (hh + 1), :] * jnp.exp(acf_row)
                yt_scr[c, 64 * h:64 * (h + 1), :] = yd + yo + dskip_ref[h] * xth
                tot = pt[32 + h:33 + h, q - 1:q]
                wf_row = dtf_row * jnp.exp(tot - acf_row)
                st = _dot((xth * wf_row).astype(BF16), bg)
                s_scr[0, 64 * h:64 * (h + 1), :] = s_scr[0, 64 * h:64 * (h + 1), :] * jnp.exp(tot) + st
        return carry

    lax.fori_loop(0, nc, pass_fwd, 0, unroll=nc)

    def pass_bwd(ci, carry):
        c = nc - 1 - ci
        r0 = pl.multiple_of(c * q, q)
        pt = tabt[c]
        for g in range(2):
            bg = bc_scr[pl.ds(r0, q), 128 * g:128 * (g + 1)]
            cg = bc_scr[pl.ds(r0, q), 256 + 128 * g:256 + 128 * (g + 1)]
            y_in = _dot_nt(s_scr[1, 256 * g:256 * (g + 1), :].astype(BF16), cg)
            for hh in range(4):
                h = 4 * g + hh
                eb_row = pt[40 + h:41 + h, :] - pt[24 + h:25 + h, :]
                tot = pt[40 + h:41 + h, q - 1:q]
                xth = xt_scr[c, 64 * h:64 * (h + 1), :]
                yt_scr[c, 64 * h:64 * (h + 1), :] += y_in[64 * hh:64 * (hh + 1), :] * jnp.exp(tot - eb_row)
                wb_row = pt[8 + h:9 + h, :] * jnp.exp(eb_row)
                st = _dot((xth * wb_row).astype(BF16), bg)
                s_scr[1, 64 * h:64 * (h + 1), :] = s_scr[1, 64 * h:64 * (h + 1), :] * jnp.exp(tot) + st
        zc = z_ref[pl.ds(r0, q), :]
        yg = yt_scr[c].T * (zc * _sigmoid(zc))
        ms = jnp.mean(yg * yg, axis=-1, keepdims=True)
        o_ref[pl.ds(r0, q), :] = (yg * lax.rsqrt(ms + EPS) * ng_ref[...]).astype(o_ref.dtype)
        return carry

    lax.fori_loop(0, nc, pass_bwd, 0, unroll=nc)

    if emit_state and has_prev:
        sfin_ref[0] = sprev_ref[...]
        sfin_ref[1] = s_scr[...]
    elif emit_state:
        sfin_ref[...] = s_scr[...]


def _ssd(xbc, z, dt, conv_w, conv_b, bias128, alog128, dskip, norm_g, tri, h0, s_prev, nb, seq, emit_state):
    t = xbc.shape[0]
    has_h0 = h0 is not None
    has_prev = s_prev is not None
    const = lambda shape: pl.BlockSpec(shape, lambda b: (0,) * len(shape))
    in_specs = [
        pl.BlockSpec((seq, SSM_CONV_CH), lambda b: (b, 0)),
        pl.BlockSpec((seq, SSM_D_INNER), lambda b: (b, 0)),
        pl.BlockSpec((seq, 128), lambda b: (b, 0)),
        const((3, SSM_CONV_CH)), const((1, SSM_CONV_CH)), const((1, 128)), const((1, 128)),
        pl.BlockSpec(memory_space=pltpu.SMEM), const((1, SSM_D_INNER)), const((128, 128)),
    ]
    args = [xbc, z, dt, conv_w, conv_b, bias128, alog128, dskip, norm_g, tri]
    if has_h0:
        in_specs.append(pl.BlockSpec((None, 2, 512, 128), lambda b: (b, 0, 0, 0)))
        args.append(h0)
    if has_prev:
        in_specs.append(pl.BlockSpec((None, 2, 512, 128), lambda b: (b, 0, 0, 0)))
        args.append(s_prev)
    out_specs = [pl.BlockSpec((seq, SSM_D_INNER), lambda b: (b, 0))]
    out_shape = [jax.ShapeDtypeStruct((t, SSM_D_INNER), BF16)]
    if emit_state and has_prev:
        out_specs.append(pl.BlockSpec((None, 2, 2, 512, 128), lambda b: (b, 0, 0, 0, 0)))
        out_shape.append(jax.ShapeDtypeStruct((nb, 2, 2, 512, 128), F32))
    elif emit_state:
        out_specs.append(pl.BlockSpec((None, 2, 512, 128), lambda b: (b, 0, 0, 0)))
        out_shape.append(jax.ShapeDtypeStruct((nb, 2, 512, 128), F32))
    return pl.pallas_call(
        functools.partial(_ssd_kernel, seq=seq, has_h0=has_h0, emit_state=emit_state, has_prev=has_prev),
        grid=(nb,),
        in_specs=in_specs,
        out_specs=out_specs,
        out_shape=out_shape,
        scratch_shapes=[
            pltpu.VMEM((seq // SSM_CHUNK, SSM_D_INNER, SSM_CHUNK), F32),
            pltpu.VMEM((seq, 512), BF16),
            pltpu.VMEM((seq, 128), F32),
            pltpu.VMEM((seq // SSM_CHUNK, 128, SSM_CHUNK), F32),
            pltpu.VMEM((seq // SSM_CHUNK, SSM_D_INNER, SSM_CHUNK), F32),
            pltpu.VMEM((2, 512, 128), F32),
        ],
        compiler_params=_params("arbitrary"),
        name="ssd",
    )(*args)


def _merge_kernel(x_ref, h_ref, g_ref, b0, b1, b2, b3, wg0, wg1, wg2, wg3, wb_ref, wo_ref, o_ref):
    n = pl.program_id(1)

    @pl.when(n == 0)
    def _():
        o_ref[...] = jnp.zeros_like(o_ref)

    h = h_ref[...]
    merged = None
    for k, (b_ref, wg_ref) in enumerate(((b0, wg0), (b1, wg1), (b2, wg2), (b3, wg3))):
        gate = _sigmoid(_dot_nt(h, wg_ref[0]))
        term = gate * _dot(b_ref[...], wb_ref[k])
        merged = term if merged is None else merged + term
    o_ref[...] += _dot(merged.astype(BF16), wo_ref[...])

    @pl.when(n == pl.num_programs(1) - 1)
    def _():
        o_ref[...] = x_ref[...] + g_ref[...] * o_ref[...]


def _merge(x, h, mod3, branches, wt, w_branch, w_out, layer, row_fn):
    t = x.shape[0]
    tm, tn = 512, 512
    assert t % tm == 0
    nn = D_MODEL // tn
    rf = lambda m: row_fn(m * tm)
    gate_spec = lambda k: pl.BlockSpec(
        (pl.Element(1), pl.Element(tn), pl.Element(D_MODEL)),
        lambda m, n: (layer, pl.multiple_of(GATE_COL0 + k * D_MODEL + n * tn, 16), 0))
    return pl.pallas_call(
        _merge_kernel,
        grid=(t // tm, nn),
        in_specs=[
            pl.BlockSpec((tm, D_MODEL), lambda m, n: (m, 0)),
            pl.BlockSpec((tm, D_MODEL), lambda m, n: (m, 0)),
            _mod_spec(2, rf),
        ] + [pl.BlockSpec((tm, MIX_W), lambda m, n: (m, 0))] * 4
          + [gate_spec(k) for k in range(4)] + [
            pl.BlockSpec((None, N_BRANCH, MIX_W, tn), lambda m, n: (layer, 0, 0, n)),
            pl.BlockSpec((None, tn, D_MODEL), lambda m, n: (layer, n, 0)),
        ],
        out_specs=pl.BlockSpec((tm, D_MODEL), lambda m, n: (m, 0)),
        out_shape=jax.ShapeDtypeStruct((t, D_MODEL), F32),
        compiler_params=_params("arbitrary", "arbitrary"),
        name="merge",
    )(x, h, mod3, *branches, wt, wt, wt, wt, w_branch, w_out)


def _ffn_kernel(x_ref, sc_ref, sh_ref, g_ref, w1_ref, w2_ref, o_ref, h_scr):
    n = pl.program_id(1)

    @pl.when(n == 0)
    def _():
        _rms_mod_rows(x_ref, sc_ref, sh_ref, h_scr)
        o_ref[...] = jnp.zeros_like(o_ref)

    a = jnp.maximum(_dot(h_scr[...], w1_ref[...].astype(BF16)), 0.0)
    o_ref[...] += _dot((a * a).astype(BF16), w2_ref[...].astype(BF16))

    @pl.when(n == pl.num_programs(1) - 1)
    def _():
        o_ref[...] = x_ref[...] + g_ref[...] * o_ref[...]


def _ffn(x, mod3, w1, w2, layer, row_fn):
    t = x.shape[0]
    tm, tf = 1024, 512
    assert t % tm == 0
    rf = lambda m: row_fn(m * tm)
    return pl.pallas_call(
        _ffn_kernel,
        grid=(t // tm, D_FF // tf),
        in_specs=[
            pl.BlockSpec((tm, D_MODEL), lambda m, n: (m, 0), pipeline_mode=pl.Buffered(1)),
            _mod_spec(4, rf), _mod_spec(3, rf), _mod_spec(5, rf),
            pl.BlockSpec((None, D_MODEL, tf), lambda m, n: (layer, 0, n)),
            pl.BlockSpec((None, tf, D_MODEL), lambda m, n: (layer, n, 0)),
        ],
        out_specs=pl.BlockSpec((tm, D_MODEL), lambda m, n: (m, 0)),
        out_shape=jax.ShapeDtypeStruct((t, D_MODEL), F32),
        scratch_shapes=[pltpu.VMEM((tm, D_MODEL), BF16)],
        compiler_params=_params("arbitrary", "arbitrary"),
        name="ffn",
    )(x, mod3, mod3, mod3, w1, w2)


def _rope_tables(seq):
    half = 16
    freqs = ROPE_THETA ** (-jnp.arange(half, dtype=F32) / half)
    n_rows = seq // GRID_W
    rows = jnp.repeat(jnp.arange(n_rows), GRID_W).astype(F32)
    cols = jnp.tile(jnp.arange(GRID_W), n_rows).astype(F32)
    ar = rows[:, None] * freqs[None, :]
    ac = cols[:, None] * freqs[None, :]
    cos64 = jnp.concatenate([jnp.cos(ar), jnp.cos(ar), jnp.cos(ac), jnp.cos(ac)], axis=-1)
    sin64 = jnp.concatenate([-jnp.sin(ar), jnp.sin(ar), -jnp.sin(ac), jnp.sin(ac)], axis=-1)
    return (jnp.tile(cos64, (1, ATT_HEADS)), jnp.tile(sin64, (1, ATT_HEADS)),
            jnp.tile(cos64, (1, ATT_KV_HEADS)), jnp.tile(sin64, (1, ATT_KV_HEADS)))


def _layer_path(x, mod3, lw, row_fn, nb, seq, lat, s_prev=None):
    (wt, qg, kg, gsum, conv_w, conv_b, bias128, alog128, dskip, ssm_g, tri, gmlp_g, ws, bs_t,
     w_branch, w_out, w_ff1, w_ff2, layer) = lw
    h, qkv, f_in, xbc, z, dt, o_gmlp = _inproj(x, mod3, wt, layer, gmlp_g, ws, bs_t, row_fn)
    if lat is None:
        o_att, k_new, v_new = _attention_ctx(qkv, qg, kg, gsum, nb, seq)
        h0 = None
    else:
        rope_tabs, cache_k, cache_v, h0 = lat
        o_att = _attention_lat(qkv, qg, kg, gsum, rope_tabs, cache_k, cache_v, nb, seq)
        k_new = v_new = None
    o_fnet = _fourier(f_in, nb, seq)
    ssd_out = _ssd(xbc, z, dt, conv_w, conv_b, bias128, alog128, dskip, ssm_g, tri, h0, s_prev, nb, seq,
                   emit_state=lat is None)
    o_ssm = ssd_out[0]
    s_fin = ssd_out[1] if lat is None else None
    x = _merge(x, h, mod3, (o_att, o_fnet, o_ssm, o_gmlp), wt, w_branch, w_out, layer, row_fn)
    x = _ffn(x, mod3, w_ff1, w_ff2, layer, row_fn)
    return x, k_new, v_new, s_fin


def kernel(x_prompt, x_sample, c, cache_k, cache_v, state_ssm, c_ctx, w_mod, b_mod, w_in, q_norm_g, k_norm_g,
           conv_w, conv_b, a_log, dt_bias, d_skip, ssm_norm_g, gmlp_norm_g, w_spatial, b_spatial, w_branch,
           w_out, w_ff1, w_ff2):
    nb_ctx, seq_ctx, _ = x_prompt.shape
    nb_lat, seq_lat, _ = x_sample.shape
    assert seq_lat == LAT_LEN and 1 + nb_lat <= 8

    cond8 = jnp.zeros((8, D_MODEL), F32).at[0].set(c_ctx).at[1:1 + nb_lat].set(c)
    mod = _modulation(cond8, w_mod, b_mod)

    gsum = jnp.asarray(np.kron(np.eye(ATT_HEADS), np.ones((HEAD_DIM, HEAD_DIM))), BF16)
    tri = jnp.asarray(np.tril(np.ones((SSM_CHUNK, SSM_CHUNK))), BF16)
    rope_tabs = _rope_tables(seq_lat)

    xp = x_prompt.reshape(nb_ctx * seq_ctx, D_MODEL)
    xs = x_sample.reshape(nb_lat * seq_lat, D_MODEL)
    ctx_row = lambda tok: 0
    lat_row = lambda tok: 1 + tok // LAT_LEN

    wt = _cast_bf16(jnp.swapaxes(w_in, 1, 2))
    w_branch = _cast_bf16(w_branch)
    w_out = _cast_bf16(w_out)

    assert DEPTH == 2
    new_k, new_v, s_l = [], [], None
    for l in range(DEPTH):
        zeros80 = jnp.zeros((128 - 48,), F32)
        bias128 = jnp.concatenate([jnp.tile(dt_bias[l].reshape(16), 3), zeros80]).reshape(1, 128)
        alog128 = jnp.concatenate([jnp.zeros((16,), F32), jnp.tile(a_log[l].reshape(16), 2), zeros80]).reshape(1, 128)
        lw = (
            wt,
            jnp.tile(q_norm_g[l], ATT_HEADS).reshape(1, 512), jnp.tile(k_norm_g[l], ATT_KV_HEADS).reshape(1, 128),
            gsum, conv_w[l], conv_b[l].reshape(1, SSM_CONV_CH), bias128, alog128,
            d_skip[l], ssm_norm_g[l].reshape(1, SSM_D_INNER), tri,
            gmlp_norm_g[l].reshape(1, MIX_W), w_spatial[l].astype(BF16), b_spatial[l].T,
            w_branch, w_out, w_ff1, w_ff2, l,
        )
        mod3 = mod[l].reshape(8, 1, 6 * D_MODEL)
        xp, k_l, v_l, s_l = _layer_path(xp, mod3, lw, ctx_row, nb_ctx, seq_ctx, None, s_prev=s_l)
        new_k.append(k_l.reshape(nb_ctx, seq_ctx, ATT_KV_HEADS, HEAD_DIM))
        new_v.append(v_l.reshape(nb_ctx, seq_ctx, ATT_KV_HEADS, HEAD_DIM))
        lat = (rope_tabs,
               cache_k[:, l].reshape(nb_lat, -1, ATT_KV_HEADS * HEAD_DIM),
               cache_v[:, l].reshape(nb_lat, -1, ATT_KV_HEADS * HEAD_DIM),
               state_ssm[:, l].reshape(nb_lat, 2, SSM_HEADS * SSM_HEAD_DIM, SSM_STATE))
        xs, _, _, _ = _layer_path(xs, mod3, lw, lat_row, nb_lat, seq_lat, lat)

    return (xp.reshape(nb_ctx, seq_ctx, D_MODEL), xs.reshape(nb_lat, seq_lat, D_MODEL),
            jnp.stack(new_k, axis=1), jnp.stack(new_v, axis=1),
            s_l.reshape(nb_ctx, DEPTH, 2, SSM_HEADS, SSM_HEAD_DIM, SSM_STATE))
```

```python
import functools
import math

import numpy as np
import jax
import jax.numpy as jnp
from jax import lax
from jax.experimental import pallas as pl
from jax.experimental.pallas import tpu as pltpu

F32 = jnp.float32
BF16 = jnp.bfloat16

D_MODEL = 2048
DEPTH = 2
GRID_W = 64
MIX_W = 512
N_BRANCH = 4
ATT_HEADS = 8
ATT_KV_HEADS = 2
HEAD_DIM = 64
ROPE_THETA = 10000.0
FNET_GW = 128
SSM_HEADS = 8
SSM_HEAD_DIM = 64
SSM_STATE = 128
SSM_CHUNK = 128
SSM_D_INNER = 512
SSM_CONV_CH = 1024
GMLP_CHUNK = 128
D_FF = 4 * D_MODEL
EPS = 1e-6
LAT_LEN = 1024

QKV_W = 768
COL_F = 768
COL_XBC = 1280
COL_Z = 2304
COL_DT = 2816
COL_UV = 2832
GATE_COL0 = 3856
N_IN = GATE_COL0 + N_BRANCH * D_MODEL
_IN_PIECES = ((0, QKV_W), (COL_F, COL_XBC), (COL_XBC, COL_Z), (COL_Z, COL_DT),
              (COL_DT, COL_DT + 128),
              (COL_UV, GATE_COL0))

VMEM_LIMIT = 56 * 1024 * 1024
NEG_BIG = -1e30


def _params(*sem):
    return pltpu.CompilerParams(dimension_semantics=sem, vmem_limit_bytes=VMEM_LIMIT)


def _dot(a, b):
    return jnp.dot(a, b, preferred_element_type=F32)


def _dot_nt(a, b):
    return lax.dot_general(a, b, (((1,), (1,)), ((), ())), preferred_element_type=F32)


def _dot_tn(a, b):
    return lax.dot_general(a, b, (((0,), (0,)), ((), ())), preferred_element_type=F32)


def _sigmoid(x):
    return 1.0 / (1.0 + jnp.exp(-x))


def _split2(x):
    hi = x.astype(BF16)
    lo = (x - hi.astype(F32)).astype(BF16)
    return hi, lo


def _rms_mod(x, sc, sh):
    ms = jnp.mean(x * x, axis=-1, keepdims=True)
    return x * lax.rsqrt(ms + EPS) * (1.0 + sc) + sh


def _rms_mod_rows(x_ref, sc_ref, sh_ref, dst_ref):
    slab = 16

    def body(i, carry):
        rows = pl.ds(pl.multiple_of(i * slab, slab), slab)
        dst_ref[rows, :] = _rms_mod(x_ref[rows, :], sc_ref[...], sh_ref[...]).astype(dst_ref.dtype)
        return carry

    lax.fori_loop(0, x_ref.shape[0] // slab, body, 0, unroll=8)


def _mod_kernel(cond_ref, w_ref, b_ref, o_ref):
    c = cond_ref[...]
    s = c * _sigmoid(c)
    o_ref[...] = _dot(s.astype(BF16), w_ref[...].astype(BF16)) + b_ref[...]


def _modulation(cond8, w_mod, b_mod):
    tn = 1024
    n6 = 6 * D_MODEL
    return pl.pallas_call(
        _mod_kernel,
        grid=(DEPTH, n6 // tn),
        in_specs=[
            pl.BlockSpec((8, D_MODEL), lambda l, n: (0, 0)),
            pl.BlockSpec((None, D_MODEL, tn), lambda l, n: (l, 0, n)),
            pl.BlockSpec((None, 1, tn), lambda l, n: (l, 0, n)),
        ],
        out_specs=pl.BlockSpec((None, 8, tn), lambda l, n: (l, 0, n)),
        out_shape=jax.ShapeDtypeStruct((DEPTH, 8, n6), F32),
        compiler_params=_params("arbitrary", "arbitrary"),
        name="modulation",
    )(cond8, w_mod, b_mod.reshape(DEPTH, 1, n6))


def _mod_spec(chunk, row_fn):
    return pl.BlockSpec((None, 1, D_MODEL), lambda m, *_: (row_fn(m), 0, chunk))


def _cast_kernel(w_ref, o_ref):
    o_ref[...] = w_ref[...].astype(o_ref.dtype)


def _cast_bf16(w):
    flat = w.reshape(-1, D_MODEL)
    rows = 1024
    out = pl.pallas_call(
        _cast_kernel,
        grid=(pl.cdiv(flat.shape[0], rows),),
        in_specs=[pl.BlockSpec((rows, D_MODEL), lambda r: (r, 0))],
        out_specs=pl.BlockSpec((rows, D_MODEL), lambda r: (r, 0)),
        out_shape=jax.ShapeDtypeStruct(flat.shape, BF16),
        compiler_params=_params("arbitrary"),
        name="cast_bf16",
    )(flat)
    return out.reshape(w.shape)


def _gmlp_chunk(x, gain, ws_ref, bs_ref):
    c0 = math.sqrt(2.0 / math.pi)
    ge = x * (0.5 * (1.0 + jnp.tanh(c0 * (x + 0.044715 * (x * x * x)))))
    u = ge[:, :MIX_W]
    v = ge[:, MIX_W:]
    v = v * lax.rsqrt(jnp.mean(v * v, axis=-1, keepdims=True) + EPS) * gain
    outs = []
    for g in range(MIX_W // 128):
        gl = slice(g * 128, (g + 1) * 128)
        mixed = _dot(ws_ref[g], v[:, gl].astype(BF16)) + bs_ref[:, g:g + 1]
        outs.append(u[:, gl] * mixed)
    return jnp.concatenate(outs, axis=1)


def _inproj_kernel(x_ref, sc_ref, sh_ref, w_ref, gg_ref, ws_ref, bs_ref, h_ref, *out_refs):
    hb = _rms_mod(x_ref[...], sc_ref[...], sh_ref[...]).astype(BF16)
    h_ref[...] = hb
    o_gmlp = out_refs[-1]
    a, b = _IN_PIECES[-1]
    uv = _dot_nt(hb, w_ref[a:b, :])
    for c in range(x_ref.shape[0] // GMLP_CHUNK):
        sl = slice(c * GMLP_CHUNK, (c + 1) * GMLP_CHUNK)
        o_gmlp[sl, :] = _gmlp_chunk(uv[sl, :], gg_ref[...], ws_ref, bs_ref).astype(o_gmlp.dtype)
    for o_ref, (a, b) in zip(out_refs[:-1], _IN_PIECES[:-1]):
        o_ref[...] = _dot_nt(hb, w_ref[a:b, :])


def _inproj(x, mod3, wt, layer, gmlp_g, ws, bs_t, row_fn):
    t = x.shape[0]
    tm = 256
    assert t % tm == 0
    widths = [b - a for a, b in _IN_PIECES[:-1]]
    const = lambda shape: pl.BlockSpec(shape, lambda m: (0,) * len(shape))
    return pl.pallas_call(
        _inproj_kernel,
        grid=(t // tm,),
        in_specs=[
            pl.BlockSpec((tm, D_MODEL), lambda m: (m, 0)),
            _mod_spec(1, lambda m: row_fn(m * tm)),
            _mod_spec(0, lambda m: row_fn(m * tm)),
            pl.BlockSpec((None, GATE_COL0, D_MODEL), lambda m: (layer, 0, 0), pipeline_mode=pl.Buffered(1)),
            const((1, MIX_W)), const((4, 128, 128)), const((128, 4)),
        ],
        out_specs=[pl.BlockSpec((tm, D_MODEL), lambda m: (m, 0))]
                  + [pl.BlockSpec((tm, w), lambda m: (m, 0)) for w in widths]
                  + [pl.BlockSpec((tm, MIX_W), lambda m: (m, 0))],
        out_shape=[jax.ShapeDtypeStruct((t, D_MODEL), BF16)]
                  + [jax.ShapeDtypeStruct((t, w), F32) for w in widths]
                  + [jax.ShapeDtypeStruct((t, MIX_W), BF16)],
        compiler_params=_params("arbitrary"),
        name="inproj",
    )(x, mod3, mod3, wt, gmlp_g, ws, bs_t)


def _head_rms(t, gsum, gain):
    t2 = t * t
    hi, lo = _split2(t2)
    ss = _dot(hi, gsum) + _dot(lo, gsum)
    return t * lax.rsqrt(ss * (1.0 / HEAD_DIM) + EPS) * gain


def _rope(t, cos, sin_signed):
    w = t.shape[-1]
    up = pltpu.roll(t, w - 16, 1)
    dn = pltpu.roll(t, 16, 1)
    lane = lax.broadcasted_iota(jnp.int32, t.shape, 1)
    partner = jnp.where((lane & 31) < 16, up, dn)
    return t * cos + partner * sin_signed


ATT_GROUP = ATT_HEADS // ATT_KV_HEADS
GROUP_W = ATT_GROUP * HEAD_DIM


def _store_group_keys(k, kbd_scr, row0):
    s = k.shape[0]
    k2 = jnp.concatenate([k, k], axis=1)
    k2r = pltpu.roll(k2, HEAD_DIM, 1)
    lane = lax.broadcasted_iota(jnp.int32, k2.shape, 1)
    first = (lane & (2 * HEAD_DIM - 1)) < HEAD_DIM
    for j, tiled in enumerate((jnp.where(first, k2, k2r), jnp.where(first, k2r, k2))):
        for hh in range(ATT_GROUP):
            slab = jnp.where(lax.shift_right_logical(lane, 6) == hh, tiled, 0.0)
            kbd_scr[j, hh, row0:row0 + s, :] = slab.astype(kbd_scr.dtype)


def _attend_heads(q, kbd_scr, vt_scr, o_ref):
    qb = (q * (HEAD_DIM ** -0.5 * math.log2(math.e))).astype(BF16)
    s_len = kbd_scr.shape[2]
    for j in range(ATT_KV_HEADS):
        kbd = kbd_scr[j].reshape(ATT_GROUP * s_len, GROUP_W)
        s_all = _dot_nt(kbd, qb[:, j * GROUP_W:(j + 1) * GROUP_W])
        outs = []
        for hh in range(ATT_GROUP):
            s = s_all[hh * s_len:(hh + 1) * s_len, :]
            m = jnp.max(s, axis=0, keepdims=True)
            e = jnp.exp2(s - m)
            den = jnp.sum(e, axis=0, keepdims=True)
            outs.append(_dot(vt_scr[j * HEAD_DIM:(j + 1) * HEAD_DIM, :], e.astype(BF16)) / den)
        o_ref[:, j * GROUP_W:(j + 1) * GROUP_W] = jnp.concatenate(outs, axis=0).T.astype(o_ref.dtype)


def _attn_ctx_kernel(q_ref, kv_ref, qg_ref, kg_ref, gsum_ref, o_ref, kout_ref, vout_ref, kbd_scr, vt_scr):
    gsum = gsum_ref[...]
    seq = vt_scr.shape[-1]
    for i in range(vt_scr.shape[0]):
        rows = pl.ds(i * seq, seq)
        kv = kv_ref[rows, :]
        k = _head_rms(kv[:, :128], gsum[:128, :128], kg_ref[...])
        v = kv[:, 128:]
        kout_ref[rows, :] = k
        vout_ref[rows, :] = v
        _store_group_keys(k, kbd_scr.at[i], 0)
        vt_scr[i] = v.T.astype(BF16)
        q = _head_rms(q_ref[rows, :], gsum, qg_ref[...])
        _attend_heads(q, kbd_scr.at[i], vt_scr.at[i], o_ref.at[rows])


def _attn_lat_kernel(q_ref, kv_ref, qg_ref, kg_ref, gsum_ref, cq_ref, sq_ref, ck_ref, sk_ref,
                     cache_k_ref, cache_v_ref, o_ref, kbd_scr, vt_scr):
    gsum = gsum_ref[...]
    past = cache_k_ref.shape[0]

    @pl.when(pl.program_id(1) == 0)
    def _():
        kv = kv_ref[...]
        k = _head_rms(kv[:, :128], gsum[:128, :128], kg_ref[...])
        k = _rope(k, ck_ref[...], sk_ref[...])
        _store_group_keys(cache_k_ref[...], kbd_scr, 0)
        _store_group_keys(k, kbd_scr, past)
        vt_scr[:, :past] = cache_v_ref[...].T.astype(BF16)
        vt_scr[:, past:] = kv[:, 128:].T.astype(BF16)

    q = _head_rms(q_ref[...], gsum, qg_ref[...])
    q = _rope(q, cq_ref[...], sq_ref[...])
    _attend_heads(q, kbd_scr, vt_scr, o_ref)


def _attention_ctx(qkv, qg, kg, gsum, nb, seq):
    t = qkv.shape[0]
    per = 4 if nb % 4 == 0 else 1
    rows = per * seq
    const = lambda shape: pl.BlockSpec(shape, lambda b: (0, 0))
    return pl.pallas_call(
        _attn_ctx_kernel,
        grid=(nb // per,),
        in_specs=[
            pl.BlockSpec((rows, 512), lambda b: (b, 0)),
            pl.BlockSpec((rows, 256), lambda b: (b, 2)),
            const((1, 512)), const((1, 128)), const((512, 512)),
        ],
        out_specs=[
            pl.BlockSpec((rows, 512), lambda b: (b, 0)),
            pl.BlockSpec((rows, 128), lambda b: (b, 0)),
            pl.BlockSpec((rows, 128), lambda b: (b, 0)),
        ],
        out_shape=[
            jax.ShapeDtypeStruct((t, 512), BF16),
            jax.ShapeDtypeStruct((t, 128), F32),
            jax.ShapeDtypeStruct((t, 128), F32),
        ],
        scratch_shapes=[pltpu.VMEM((per, ATT_KV_HEADS, ATT_GROUP, seq, GROUP_W), BF16),
                        pltpu.VMEM((per, 128, seq), BF16)],
        compiler_params=_params("arbitrary"),
        name="attn_ctx",
    )(qkv, qkv, qg, kg, gsum)


def _attention_lat(qkv, qg, kg, gsum, rope_tabs, cache_k, cache_v, nb, seq):
    t = qkv.shape[0]
    qblk = 256
    nq = seq // qblk
    past = cache_k.shape[1]
    cq, sq, ck, sk = rope_tabs
    const = lambda shape: pl.BlockSpec(shape, lambda b, i: (0, 0))
    return pl.pallas_call(
        _attn_lat_kernel,
        grid=(nb, nq),
        in_specs=[
            pl.BlockSpec((qblk, 512), lambda b, i: (b * nq + i, 0)),
            pl.BlockSpec((seq, 256), lambda b, i: (b, 2)),
            const((1, 512)), const((1, 128)), const((512, 512)),
            pl.BlockSpec((qblk, 512), lambda b, i: (i, 0)),
            pl.BlockSpec((qblk, 512), lambda b, i: (i, 0)),
            const((seq, 128)), const((seq, 128)),
            pl.BlockSpec((None, past, 128), lambda b, i: (b, 0, 0)),
            pl.BlockSpec((None, past, 128), lambda b, i: (b, 0, 0)),
        ],
        out_specs=pl.BlockSpec((qblk, 512), lambda b, i: (b * nq + i, 0)),
        out_shape=jax.ShapeDtypeStruct((t, 512), BF16),
        scratch_shapes=[pltpu.VMEM((ATT_KV_HEADS, ATT_GROUP, past + seq, GROUP_W), BF16),
                        pltpu.VMEM((128, past + seq), BF16)],
        compiler_params=_params("arbitrary", "arbitrary"),
        name="attn_lat",
    )(qkv, qkv, qg, kg, gsum, cq, sq, ck, sk, cache_k, cache_v)


def _fourier_kernel(f_ref, bdh_ref, bdl_ref, cl_ref, sl_ref, o_ref, *, scale):
    bdh = bdh_ref[...]
    bdl = bdl_ref[...]
    cl = cl_ref[...].astype(BF16)
    sl = sl_ref[...].astype(BF16)
    seq = cl_ref.shape[0]
    for i in range(f_ref.shape[0] // seq):
        rows = pl.ds(i * seq, seq)
        xh, xl = _split2(f_ref[rows, :])
        yc, ys = [], []
        for g in range(MIX_W // FNET_GW):
            gl = slice(g * FNET_GW, (g + 1) * FNET_GW)
            y = _dot(xh[:, gl], bdh) + _dot(xl[:, gl], bdh) + _dot(xh[:, gl], bdl)
            yc.append(y[:, :FNET_GW])
            ys.append(y[:, FNET_GW:])
        re = _dot(cl, jnp.concatenate(yc, axis=1).astype(BF16))
        im = _dot(sl, jnp.concatenate(ys, axis=1).astype(BF16))
        o_ref[rows, :] = ((re - im) * scale).astype(o_ref.dtype)


def _dft_consts(seq):
    def cs(n):
        k = np.arange(n, dtype=np.int64)
        ang = 2.0 * np.pi * ((k[:, None] * k[None, :]) % n).astype(np.float64) / n
        return np.cos(ang), np.sin(ang)

    bd = np.concatenate(cs(FNET_GW), axis=1)
    cl, sl_ = cs(seq)

    def hl(a):
        a32 = jnp.asarray(a, F32)
        hi = a32.astype(BF16)
        return hi, (a32 - hi.astype(F32)).astype(BF16)

    return hl(bd) + (jnp.asarray(cl, F32), jnp.asarray(sl_, F32))


def _fourier(f, nb, seq):
    t = f.shape[0]
    consts = _dft_consts(seq)
    const = lambda a: pl.BlockSpec(a.shape, lambda b: (0, 0))
    per = 4 if (seq <= 256 and nb % 4 == 0) else 1
    return pl.pallas_call(
        functools.partial(_fourier_kernel, scale=1.0 / math.sqrt(seq * FNET_GW)),
        grid=(nb // per,),
        in_specs=[pl.BlockSpec((per * seq, MIX_W), lambda b: (b, 0))] + [const(a) for a in consts],
        out_specs=pl.BlockSpec((per * seq, MIX_W), lambda b: (b, 0)),
        out_shape=jax.ShapeDtypeStruct((t, MIX_W), BF16),
        compiler_params=_params("arbitrary"),
        name="fourier",
    )(f, *consts)


def _ssd_kernel(*refs, seq, has_h0, emit_state, has_prev):
    (xbc_ref, z_ref, dt_ref, cw_ref, cb_ref, bias_ref, alog_ref, dskip_ref, ng_ref, tri_ref) = refs[:10]
    rest = list(refs[10:])
    h0_ref = rest.pop(0) if has_h0 else None
    sprev_ref = rest.pop(0) if has_prev else None
    o_ref = rest.pop(0)
    sfin_ref = rest.pop(0) if emit_state else None
    xt_scr, bc_scr, tab, tabt, yt_scr, s_scr = rest
    nc = seq // SSM_CHUNK
    q = SSM_CHUNK

    lane = lax.broadcasted_iota(jnp.int32, (q, 128), 1)
    ii = lax.broadcasted_iota(jnp.int32, (q, q), 0)
    jj = lax.broadcasted_iota(jnp.int32, (q, q), 1)
    below = ii > jj
    above = ii < jj
    row = lax.broadcasted_iota(jnp.int32, (q, 1), 0)
    cw = cw_ref[...]
    amul = jnp.where(lane < 16, 1.0, jnp.where(lane < 48, -jnp.exp(alog_ref[...]), 0.0))
    tri = tri_ref[...]

    if has_h0:
        s_scr[...] = h0_ref[...]
    else:
        s_scr[...] = jnp.zeros_like(s_scr)

    def prep(c, carry):
        r0 = pl.multiple_of(c * q, q)
        cur = xbc_ref[pl.ds(r0, q), :]
        prev8 = xbc_ref[pl.ds(pl.multiple_of(jnp.maximum(r0 - 8, 0), 8), 8), :]
        next8 = xbc_ref[pl.ds(pl.multiple_of(jnp.minimum(r0 + q, seq - 8), 8), 8), :]
        prev_row = jnp.where(c > 0, prev8[7:8, :], 0.0)
        next_row = jnp.where(c < nc - 1, next8[0:1, :], 0.0)
        up = jnp.where(row == 0, prev_row, pltpu.roll(cur, 1, 0))
        dn = jnp.where(row == q - 1, next_row, pltpu.roll(cur, q - 1, 0))
        conv = up * cw[0:1, :] + cur * cw[1:2, :] + dn * cw[2:3, :] + cb_ref[...]
        act = conv * _sigmoid(conv)
        xt_scr[c] = act[:, :SSM_D_INNER].T
        bc_scr[pl.ds(r0, q), :] = act[:, SSM_D_INNER:].astype(BF16)

        raw = dt_ref[pl.ds(r0, q), :]
        rep = jnp.where(lane < 16, raw, jnp.where(lane < 32, pltpu.roll(raw, 16, 1), pltpu.roll(raw, 32, 1)))
        v = rep + bias_ref[...]
        sp = jnp.maximum(v, 0.0) + jnp.log1p(jnp.exp(-jnp.abs(v)))
        t = sp * amul
        hi = t.astype(BF16)
        r1 = t - hi.astype(F32)
        mid = r1.astype(BF16)
        lo = (r1 - mid.astype(F32)).astype(BF16)
        cum = _dot(tri, hi) + _dot(tri, mid) + _dot(tri, lo)
        tc = jnp.where(lane < 32, t, cum)
        tab[pl.ds(r0, q), :] = tc
        tabt[c] = tc.T
        return carry

    lax.fori_loop(0, nc, prep, 0, unroll=nc)

    def pass_fwd(c, carry):
        r0 = pl.multiple_of(c * q, q)
        pc = tab[pl.ds(r0, q), :]
        pt = tabt[c]
        for g in range(2):
            bg = bc_scr[pl.ds(r0, q), 128 * g:128 * (g + 1)]
            cg = bc_scr[pl.ds(r0, q), 256 + 128 * g:256 + 128 * (g + 1)]
            cb = _dot_nt(cg, bg)
            y_in = _dot_nt(s_scr[0, 256 * g:256 * (g + 1), :].astype(BF16), cg)
            for hh in range(4):
                h = 4 * g + hh
                acf_row = pt[32 + h:33 + h, :]
                eb_col = pc[:, 40 + h:41 + h] - pc[:, 24 + h:25 + h]
                eb_row = pt[40 + h:41 + h, :] - pt[24 + h:25 + h, :]
                dtf_row = pt[h:h + 1, :]
                dtb_row = pt[8 + h:9 + h, :]
                seg = jnp.where(below, pc[:, 32 + h:33 + h] - acf_row, jnp.where(above, eb_row - eb_col, 0.0))
                wgt = jnp.where(below, dtf_row, jnp.where(above, dtb_row, dtf_row + dtb_row))
                mp = cb * (jnp.exp(seg) * wgt)
                xth = xt_scr[c, 64 * h:64 * (h + 1), :]
                yd = _dot_nt(xth.astype(BF16), mp.astype(BF16))
                yo = y_in[64 * hh:64 * (hh + 1), :] * jnp.exp(acf_row)
                yt_scr[c, 64 * h:64 * (h + 1), :] = yd + yo + dskip_ref[h] * xth
                tot = pt[32 + h:33 + h, q - 1:q]
                wf_row = dtf_row * jnp.exp(tot - acf_row)
                st = _dot((xth * wf_row).astype(BF16), bg)
                s_scr[0, 64 * h:64 * (h + 1), :] = s_scr[0, 64 * h:64 * (h + 1), :] * jnp.exp(tot) + st
        return carry

    lax.fori_loop(0, nc, pass_fwd, 0, unroll=nc)

    def pass_bwd(ci, carry):
        c = nc - 1 - ci
        r0 = pl.multiple_of(c * q, q)
        pt = tabt[c]
        for g in range(2):
            bg = bc_scr[pl.ds(r0, q), 128 * g:128 * (g + 1)]
            cg = bc_scr[pl.ds(r0, q), 256 + 128 * g:256 + 128 * (g + 1)]
            y_in = _dot_nt(s_scr[1, 256 * g:256 * (g + 1), :].astype(BF16), cg)
            for hh in range(4):
                h = 4 * g + hh
                eb_row = pt[40 + h:41 + h, :] - pt[24 + h:25 + h, :]
                tot = pt[40 + h:41 + h, q - 1:q]
                xth = xt_scr[c, 64 * h:64 * (h + 1), :]
                yt_scr[c, 64 * h:64 * (h + 1), :] += y_in[64 * hh:64 * (hh + 1), :] * jnp.exp(tot - eb_row)
                wb_row = pt[8 + h:9 + h, :] * jnp.exp(eb_row)
                st = _dot((xth * wb_row).astype(BF16), bg)
                s_scr[1, 64 * h:64 * (h + 1), :] = s_scr[1, 64 * h:64 * (h + 1), :] * jnp.exp(tot) + st
        zc = z_ref[pl.ds(r0, q), :]
        yg = yt_scr[c].T * (zc * _sigmoid(zc))
        ms = jnp.mean(yg * yg, axis=-1, keepdims=True)
        o_ref[pl.ds(r0, q), :] = (yg * lax.rsqrt(ms + EPS) * ng_ref[...]).astype(o_ref.dtype)
        return carry

    lax.fori_loop(0, nc, pass_bwd, 0, unroll=nc)

    if emit_state and has_prev:
        sfin_ref[0] = sprev_ref[...]
        sfin_ref[1] = s_scr[...]
    elif emit_state:
        sfin_ref[...] = s_scr[...]


def _ssd(xbc, z, dt, conv_w, conv_b, bias128, alog128, dskip, norm_g, tri, h0, s_prev, nb, seq, emit_state):
    t = xbc.shape[0]
    has_h0 = h0 is not None
    has_prev = s_prev is not None
    const = lambda shape: pl.BlockSpec(shape, lambda b: (0,) * len(shape))
    in_specs = [
        pl.BlockSpec((seq, SSM_CONV_CH), lambda b: (b, 0)),
        pl.BlockSpec((seq, SSM_D_INNER), lambda b: (b, 0)),
        pl.BlockSpec((seq, 128), lambda b: (b, 0)),
        const((3, SSM_CONV_CH)), const((1, SSM_CONV_CH)), const((1, 128)), const((1, 128)),
        pl.BlockSpec(memory_space=pltpu.SMEM), const((1, SSM_D_INNER)), const((128, 128)),
    ]
    args = [xbc, z, dt, conv_w, conv_b, bias128, alog128, dskip, norm_g, tri]
    if has_h0:
        in_specs.append(pl.BlockSpec((None, 2, 512, 128), lambda b: (b, 0, 0, 0)))
        args.append(h0)
    if has_prev:
        in_specs.append(pl.BlockSpec((None, 2, 512, 128), lambda b: (b, 0, 0, 0)))
        args.append(s_prev)
    out_specs = [pl.BlockSpec((seq, SSM_D_INNER), lambda b: (b, 0))]
    out_shape = [jax.ShapeDtypeStruct((t, SSM_D_INNER), BF16)]
    if emit_state and has_prev:
        out_specs.append(pl.BlockSpec((None, 2, 2, 512, 128), lambda b: (b, 0, 0, 0, 0)))
        out_shape.append(jax.ShapeDtypeStruct((nb, 2, 2, 512, 128), F32))
    elif emit_state:
        out_specs.append(pl.BlockSpec((None, 2, 512, 128), lambda b: (b, 0, 0, 0)))
        out_shape.append(jax.ShapeDtypeStruct((nb, 2, 512, 128), F32))
    return pl.pallas_call(
        functools.partial(_ssd_kernel, seq=seq, has_h0=has_h0, emit_state=emit_state, has_prev=has_prev),
        grid=(nb,),
        in_specs=in_specs,
        out_specs=out_specs,
        out_shape=out_shape,
        scratch_shapes=[
            pltpu.VMEM((seq // SSM_CHUNK, SSM_D_INNER, SSM_CHUNK), F32),
            pltpu.VMEM((seq, 512), BF16),
            pltpu.VMEM((seq, 128), F32),
            pltpu.VMEM((seq // SSM_CHUNK, 128, SSM_CHUNK), F32),
            pltpu.VMEM((seq // SSM_CHUNK, SSM_D_INNER, SSM_CHUNK), F32),
            pltpu.VMEM((2, 512, 128), F32),
        ],
        compiler_params=_params("arbitrary"),
        name="ssd",
    )(*args)


def _merge_kernel(x_ref, h_ref, g_ref, b0, b1, b2, b3, wg0, wg1, wg2, wg3, wb_ref, wo_ref, o_ref):
    n = pl.program_id(1)

    @pl.when(n == 0)
    def _():
        o_ref[...] = x_ref[...]

    h = h_ref[...]
    merged = None
    for k, (b_ref, wg_ref) in enumerate(((b0, wg0), (b1, wg1), (b2, wg2), (b3, wg3))):
        gate = _sigmoid(_dot_nt(h, wg_ref[0]))
        term = gate * _dot(b_ref[...], wb_ref[k])
        merged = term if merged is None else merged + term
    o_ref[...] += g_ref[...] * _dot(merged.astype(BF16), wo_ref[...])


def _merge(x, h, mod3, branches, wt, w_branch, w_out, layer, row_fn):
    t = x.shape[0]
    tm, tn = 512, 512
    assert t % tm == 0
    nn = D_MODEL // tn
    rf = lambda m: row_fn(m * tm)
    gate_spec = lambda k: pl.BlockSpec(
        (pl.Element(1), pl.Element(tn), pl.Element(D_MODEL)),
        lambda m, n: (layer, pl.multiple_of(GATE_COL0 + k * D_MODEL + n * tn, 16), 0))
    return pl.pallas_call(
        _merge_kernel,
        grid=(t // tm, nn),
        in_specs=[
            pl.BlockSpec((tm, D_MODEL), lambda m, n: (m, 0)),
            pl.BlockSpec((tm, D_MODEL), lambda m, n: (m, 0)),
            _mod_spec(2, rf),
        ] + [pl.BlockSpec((tm, MIX_W), lambda m, n: (m, 0))] * 4
          + [gate_spec(k) for k in range(4)] + [
            pl.BlockSpec((None, N_BRANCH, MIX_W, tn), lambda m, n: (layer, 0, 0, n)),
            pl.BlockSpec((None, tn, D_MODEL), lambda m, n: (layer, n, 0)),
        ],
        out_specs=pl.BlockSpec((tm, D_MODEL), lambda m, n: (m, 0)),
        out_shape=jax.ShapeDtypeStruct((t, D_MODEL), F32),
        compiler_params=_params("arbitrary", "arbitrary"),
        name="merge",
    )(x, h, mod3, *branches, wt, wt, wt, wt, w_branch, w_out)


def _ffn_kernel(x_ref, sc_ref, sh_ref, g_ref, w1_ref, w2_ref, o_ref, h_scr):
    n = pl.program_id(1)

    @pl.when(n == 0)
    def _():
        _rms_mod_rows(x_ref, sc_ref, sh_ref, h_scr)
        o_ref[...] = x_ref[...]

    a = jnp.maximum(_dot(h_scr[...], w1_ref[...].astype(BF16)), 0.0)
    o_ref[...] += g_ref[...] * _dot((a * a).astype(BF16), w2_ref[...].astype(BF16))


def _ffn(x, mod3, w1, w2, layer, row_fn):
    t = x.shape[0]
    tm, tf = 1024, 512
    assert t % tm == 0
    rf = lambda m: row_fn(m * tm)
    return pl.pallas_call(
        _ffn_kernel,
        grid=(t // tm, D_FF // tf),
        in_specs=[
            pl.BlockSpec((tm, D_MODEL), lambda m, n: (m, 0), pipeline_mode=pl.Buffered(1)),
            _mod_spec(4, rf), _mod_spec(3, rf), _mod_spec(5, rf),
            pl.BlockSpec((None, D_MODEL, tf), lambda m, n: (layer, 0, n)),
            pl.BlockSpec((None, tf, D_MODEL), lambda m, n: (layer, n, 0)),
        ],
        out_specs=pl.BlockSpec((tm, D_MODEL), lambda m, n: (m, 0)),
        out_shape=jax.ShapeDtypeStruct((t, D_MODEL), F32),
        scratch_shapes=[pltpu.VMEM((tm, D_MODEL), BF16)],
        compiler_params=_params("arbitrary", "arbitrary"),
        name="ffn",
    )(x, mod3, mod3, mod3, w1, w2)


def _rope_tables(seq):
    half = 16
    freqs = ROPE_THETA ** (-jnp.arange(half, dtype=F32) / half)
    n_rows = seq // GRID_W
    rows = jnp.repeat(jnp.arange(n_rows), GRID_W).astype(F32)
    cols = jnp.tile(jnp.arange(GRID_W), n_rows).astype(F32)
    ar = rows[:, None] * freqs[None, :]
    ac = cols[:, None] * freqs[None, :]
    cos64 = jnp.concatenate([jnp.cos(ar), jnp.cos(ar), jnp.cos(ac), jnp.cos(ac)], axis=-1)
    sin64 = jnp.concatenate([-jnp.sin(ar), jnp.sin(ar), -jnp.sin(ac), jnp.sin(ac)], axis=-1)
    return (jnp.tile(cos64, (1, ATT_HEADS)), jnp.tile(sin64, (1, ATT_HEADS)),
            jnp.tile(cos64, (1, ATT_KV_HEADS)), jnp.tile(sin64, (1, ATT_KV_HEADS)))


def _layer_path(x, mod3, lw, row_fn, nb, seq, lat, s_prev=None):
    (wt, qg, kg, gsum, conv_w, conv_b, bias128, alog128, dskip, ssm_g, tri, gmlp_g, ws, bs_t,
     w_branch, w_out, w_ff1, w_ff2, layer) = lw
    h, qkv, f_in, xbc, z, dt, o_gmlp = _inproj(x, mod3, wt, layer, gmlp_g, ws, bs_t, row_fn)
    if lat is None:
        o_att, k_new, v_new = _attention_ctx(qkv, qg, kg, gsum, nb, seq)
        h0 = None
    else:
        rope_tabs, cache_k, cache_v, h0 = lat
        o_att = _attention_lat(qkv, qg, kg, gsum, rope_tabs, cache_k, cache_v, nb, seq)
        k_new = v_new = None
    o_fnet = _fourier(f_in, nb, seq)
    ssd_out = _ssd(xbc, z, dt, conv_w, conv_b, bias128, alog128, dskip, ssm_g, tri, h0, s_prev, nb, seq,
                   emit_state=lat is None)
    o_ssm = ssd_out[0]
    s_fin = ssd_out[1] if lat is None else None
    x = _merge(x, h, mod3, (o_att, o_fnet, o_ssm, o_gmlp), wt, w_branch, w_out, layer, row_fn)
    x = _ffn(x, mod3, w_ff1, w_ff2, layer, row_fn)
    return x, k_new, v_new, s_fin


def kernel(x_prompt, x_sample, c, cache_k, cache_v, state_ssm, c_ctx, w_mod, b_mod, w_in, q_norm_g, k_norm_g,
           conv_w, conv_b, a_log, dt_bias, d_skip, ssm_norm_g, gmlp_norm_g, w_spatial, b_spatial, w_branch,
           w_out, w_ff1, w_ff2):
    nb_ctx, seq_ctx, _ = x_prompt.shape
    nb_lat, seq_lat, _ = x_sample.shape
    assert seq_lat == LAT_LEN and 1 + nb_lat <= 8

    cond8 = jnp.zeros((8, D_MODEL), F32).at[0].set(c_ctx).at[1:1 + nb_lat].set(c)
    mod = _modulation(cond8, w_mod, b_mod)

    gsum = jnp.asarray(np.kron(np.eye(ATT_HEADS), np.ones((HEAD_DIM, HEAD_DIM))), BF16)
    tri = jnp.asarray(np.tril(np.ones((SSM_CHUNK, SSM_CHUNK))), BF16)
    rope_tabs = _rope_tables(seq_lat)

    xp = x_prompt.reshape(nb_ctx * seq_ctx, D_MODEL)
    xs = x_sample.reshape(nb_lat * seq_lat, D_MODEL)
    ctx_row = lambda tok: 0
    lat_row = lambda tok: 1 + tok // LAT_LEN

    wt = _cast_bf16(jnp.swapaxes(w_in, 1, 2))
    w_branch = _cast_bf16(w_branch)
    w_out = _cast_bf16(w_out)

    assert DEPTH == 2
    new_k, new_v, s_l = [], [], None
    for l in range(DEPTH):
        zeros80 = jnp.zeros((128 - 48,), F32)
        bias128 = jnp.concatenate([jnp.tile(dt_bias[l].reshape(16), 3), zeros80]).reshape(1, 128)
        alog128 = jnp.concatenate([jnp.zeros((16,), F32), jnp.tile(a_log[l].reshape(16), 2), zeros80]).reshape(1, 128)
        lw = (
            wt,
            jnp.tile(q_norm_g[l], ATT_HEADS).reshape(1, 512), jnp.tile(k_norm_g[l], ATT_KV_HEADS).reshape(1, 128),
            gsum, conv_w[l], conv_b[l].reshape(1, SSM_CONV_CH), bias128, alog128,
            d_skip[l], ssm_norm_g[l].reshape(1, SSM_D_INNER), tri,
            gmlp_norm_g[l].reshape(1, MIX_W), w_spatial[l].astype(BF16), b_spatial[l].T,
            w_branch, w_out, w_ff1, w_ff2, l,
        )
        mod3 = mod[l].reshape(8, 1, 6 * D_MODEL)
        xp, k_l, v_l, s_l = _layer_path(xp, mod3, lw, ctx_row, nb_ctx, seq_ctx, None, s_prev=s_l)
        new_k.append(k_l.reshape(nb_ctx, seq_ctx, ATT_KV_HEADS, HEAD_DIM))
        new_v.append(v_l.reshape(nb_ctx, seq_ctx, ATT_KV_HEADS, HEAD_DIM))
        lat = (rope_tabs,
               cache_k[:, l].reshape(nb_lat, -1, ATT_KV_HEADS * HEAD_DIM),
               cache_v[:, l].reshape(nb_lat, -1, ATT_KV_HEADS * HEAD_DIM),
               state_ssm[:, l].reshape(nb_lat, 2, SSM_HEADS * SSM_HEAD_DIM, SSM_STATE))
        xs, _, _, _ = _layer_path(xs, mod3, lw, lat_row, nb_lat, seq_lat, lat)

    return (xp.reshape(nb_ctx, seq_ctx, D_MODEL), xs.reshape(nb_lat, seq_lat, D_MODEL),
            jnp.stack(new_k, axis=1), jnp.stack(new_v, axis=1),
            s_l.reshape(nb_ctx, DEPTH, 2, SSM_HEADS, SSM_HEAD_DIM, SSM_STATE))
```

```python
import functools
import math

import numpy as np
import jax
import jax.numpy as jnp
from jax import lax
from jax.experimental import pallas as pl
from jax.experimental.pallas import tpu as pltpu

F32 = jnp.float32
BF16 = jnp.bfloat16

D_MODEL = 2048
DEPTH = 2
GRID_W = 64
MIX_W = 512
N_BRANCH = 4
ATT_HEADS = 8
ATT_KV_HEADS = 2
HEAD_DIM = 64
ROPE_THETA = 10000.0
FNET_GW = 128
SSM_HEADS = 8
SSM_HEAD_DIM = 64
SSM_STATE = 128
SSM_CHUNK = 128
SSM_D_INNER = 512
SSM_CONV_CH = 1024
GMLP_CHUNK = 128
D_FF = 4 * D_MODEL
EPS = 1e-6
LAT_LEN = 1024

QKV_W = 768
COL_F = 768
COL_XBC = 1280
COL_Z = 2304
COL_DT = 2816
COL_UV = 2832
GATE_COL0 = 3856
N_IN = GATE_COL0 + N_BRANCH * D_MODEL
_IN_PIECES = ((0, QKV_W), (COL_F, COL_XBC), (COL_XBC, COL_Z), (COL_Z, COL_DT),
              (COL_DT, COL_DT + 128),
              (COL_UV, GATE_COL0))

VMEM_LIMIT = 56 * 1024 * 1024
NEG_BIG = -1e30


def _params(*sem):
    return pltpu.CompilerParams(dimension_semantics=sem, vmem_limit_bytes=VMEM_LIMIT)


def _dot(a, b):
    return jnp.dot(a, b, preferred_element_type=F32)


def _dot_nt(a, b):
    return lax.dot_general(a, b, (((1,), (1,)), ((), ())), preferred_element_type=F32)


def _dot_tn(a, b):
    return lax.dot_general(a, b, (((0,), (0,)), ((), ())), preferred_element_type=F32)


def _sigmoid(x):
    return 1.0 / (1.0 + jnp.exp(-x))


def _split2(x):
    hi = x.astype(BF16)
    lo = (x - hi.astype(F32)).astype(BF16)
    return hi, lo


def _rms_mod(x, sc, sh):
    ms = jnp.mean(x * x, axis=-1, keepdims=True)
    return x * lax.rsqrt(ms + EPS) * (1.0 + sc) + sh


def _rms_mod_rows(x_ref, sc_ref, sh_ref, dst_ref):
    slab = 16

    def body(i, carry):
        rows = pl.ds(pl.multiple_of(i * slab, slab), slab)
        dst_ref[rows, :] = _rms_mod(x_ref[rows, :], sc_ref[...], sh_ref[...]).astype(dst_ref.dtype)
        return carry

    lax.fori_loop(0, x_ref.shape[0] // slab, body, 0, unroll=8)


def _mod_kernel(cond_ref, w_ref, b_ref, o_ref):
    c = cond_ref[...]
    s = c * _sigmoid(c)
    o_ref[...] = _dot(s.astype(BF16), w_ref[...].astype(BF16)) + b_ref[...]


def _modulation(cond8, w_mod, b_mod):
    tn = 1024
    n6 = 6 * D_MODEL
    return pl.pallas_call(
        _mod_kernel,
        grid=(DEPTH, n6 // tn),
        in_specs=[
            pl.BlockSpec((8, D_MODEL), lambda l, n: (0, 0)),
            pl.BlockSpec((None, D_MODEL, tn), lambda l, n: (l, 0, n)),
            pl.BlockSpec((None, 1, tn), lambda l, n: (l, 0, n)),
        ],
        out_specs=pl.BlockSpec((None, 8, tn), lambda l, n: (l, 0, n)),
        out_shape=jax.ShapeDtypeStruct((DEPTH, 8, n6), F32),
        compiler_params=_params("arbitrary", "arbitrary"),
        name="modulation",
    )(cond8, w_mod, b_mod.reshape(DEPTH, 1, n6))


def _mod_spec(chunk, row_fn):
    return pl.BlockSpec((None, 1, D_MODEL), lambda m, *_: (row_fn(m), 0, chunk))


def _cast_kernel(w_ref, o_ref):
    o_ref[...] = w_ref[...].astype(o_ref.dtype)


def _cast_layer_bf16(w, layer):
    n_rows = w.shape[1]
    rows = 1024
    return pl.pallas_call(
        _cast_kernel,
        grid=(pl.cdiv(n_rows, rows),),
        in_specs=[pl.BlockSpec((None, rows, D_MODEL), lambda r: (layer, r, 0))],
        out_specs=pl.BlockSpec((None, rows, D_MODEL), lambda r: (0, r, 0)),
        out_shape=jax.ShapeDtypeStruct((1, n_rows, D_MODEL), BF16),
        compiler_params=_params("arbitrary"),
        name="cast_layer_bf16",
    )(w)


def _gmlp_chunk(x, gain, ws_ref, bs_ref):
    c0 = math.sqrt(2.0 / math.pi)
    ge = x * (0.5 * (1.0 + jnp.tanh(c0 * (x + 0.044715 * (x * x * x)))))
    u = ge[:, :MIX_W]
    v = ge[:, MIX_W:]
    v = v * lax.rsqrt(jnp.mean(v * v, axis=-1, keepdims=True) + EPS) * gain
    outs = []
    for g in range(MIX_W // 128):
        gl = slice(g * 128, (g + 1) * 128)
        mixed = _dot(ws_ref[g], v[:, gl].astype(BF16)) + bs_ref[:, g:g + 1]
        outs.append(u[:, gl] * mixed)
    return jnp.concatenate(outs, axis=1)


def _inproj_kernel(*refs, n_side):
    x_ref, sc_ref, sh_ref, w_ref, gg_ref, ws_ref, bs_ref = refs[:7]
    side_in = refs[7:7 + n_side]
    h_ref = refs[7 + n_side]
    out_refs = refs[8 + n_side:len(refs) - n_side]
    side_out = refs[len(refs) - n_side:]
    for src, dst in zip(side_in, side_out):
        dst[...] = src[...].astype(dst.dtype)
    hb = _rms_mod(x_ref[...], sc_ref[...], sh_ref[...]).astype(BF16)
    h_ref[...] = hb
    o_gmlp = out_refs[-1]
    a, b = _IN_PIECES[-1]
    uv = _dot_nt(hb, w_ref[a:b, :])
    for c in range(x_ref.shape[0] // GMLP_CHUNK):
        sl = slice(c * GMLP_CHUNK, (c + 1) * GMLP_CHUNK)
        o_gmlp[sl, :] = _gmlp_chunk(uv[sl, :], gg_ref[...], ws_ref, bs_ref).astype(o_gmlp.dtype)
    for o_ref, (a, b) in zip(out_refs[:-1], _IN_PIECES[:-1]):
        o_ref[...] = _dot_nt(hb, w_ref[a:b, :])


def _inproj(x, mod3, wt, gmlp_g, ws, bs_t, row_fn, side_casts=()):
    t = x.shape[0]
    tm = 256
    assert t % tm == 0
    steps = t // tm
    widths = [b - a for a, b in _IN_PIECES[:-1]]
    const = lambda shape: pl.BlockSpec(shape, lambda m: (0,) * len(shape))
    in_specs = [
        pl.BlockSpec((tm, D_MODEL), lambda m: (m, 0)),
        _mod_spec(1, lambda m: row_fn(m * tm)),
        _mod_spec(0, lambda m: row_fn(m * tm)),
        pl.BlockSpec((None, GATE_COL0, D_MODEL), lambda m: (0, 0, 0), pipeline_mode=pl.Buffered(1)),
        const((1, MIX_W)), const((4, 128, 128)), const((128, 4)),
    ]
    args = [x, mod3, mod3, wt, gmlp_g, ws, bs_t]
    out_specs = ([pl.BlockSpec((tm, D_MODEL), lambda m: (m, 0))]
                 + [pl.BlockSpec((tm, w), lambda m: (m, 0)) for w in widths]
                 + [pl.BlockSpec((tm, MIX_W), lambda m: (m, 0))])
    out_shape = ([jax.ShapeDtypeStruct((t, D_MODEL), BF16)]
                 + [jax.ShapeDtypeStruct((t, w), F32) for w in widths]
                 + [jax.ShapeDtypeStruct((t, MIX_W), BF16)])
    side_in_specs, side_args = [], []
    for w_src, lead in side_casts:
        n_rows = w_src.shape[-2]
        rows = -(-n_rows // (16 * steps)) * 16
        assert pl.cdiv(n_rows, rows) == steps
        if lead is None:
            side_in_specs.append(pl.BlockSpec((rows, D_MODEL), lambda m: (m, 0)))
            out_specs.append(pl.BlockSpec((rows, D_MODEL), lambda m: (m, 0)))
            out_shape.append(jax.ShapeDtypeStruct((n_rows, D_MODEL), BF16))
        else:
            side_in_specs.append(pl.BlockSpec((None, rows, D_MODEL), lambda m, lead=lead: (lead, m, 0)))
            out_specs.append(pl.BlockSpec((None, rows, D_MODEL), lambda m: (0, m, 0)))
            out_shape.append(jax.ShapeDtypeStruct((1, n_rows, D_MODEL), BF16))
        side_args.append(w_src)
    in_specs += side_in_specs
    args += side_args
    return pl.pallas_call(
        functools.partial(_inproj_kernel, n_side=len(side_casts)),
        grid=(steps,),
        in_specs=in_specs,
        out_specs=out_specs,
        out_shape=out_shape,
        compiler_params=_params("arbitrary"),
        name="inproj",
    )(*args)


def _head_rms(t, gsum, gain):
    t2 = t * t
    hi, lo = _split2(t2)
    ss = _dot(hi, gsum) + _dot(lo, gsum)
    return t * lax.rsqrt(ss * (1.0 / HEAD_DIM) + EPS) * gain


def _rope(t, cos, sin_signed):
    w = t.shape[-1]
    up = pltpu.roll(t, w - 16, 1)
    dn = pltpu.roll(t, 16, 1)
    lane = lax.broadcasted_iota(jnp.int32, t.shape, 1)
    partner = jnp.where((lane & 31) < 16, up, dn)
    return t * cos + partner * sin_signed


ATT_GROUP = ATT_HEADS // ATT_KV_HEADS
GROUP_W = ATT_GROUP * HEAD_DIM


def _store_group_keys(k, kbd_scr, row0):
    s = k.shape[0]
    k2 = jnp.concatenate([k, k], axis=1)
    k2r = pltpu.roll(k2, HEAD_DIM, 1)
    lane = lax.broadcasted_iota(jnp.int32, k2.shape, 1)
    first = (lane & (2 * HEAD_DIM - 1)) < HEAD_DIM
    for j, tiled in enumerate((jnp.where(first, k2, k2r), jnp.where(first, k2r, k2))):
        for hh in range(ATT_GROUP):
            slab = jnp.where(lax.shift_right_logical(lane, 6) == hh, tiled, 0.0)
            kbd_scr[j, hh, row0:row0 + s, :] = slab.astype(kbd_scr.dtype)


def _attend_heads(q, kbd_scr, vt_scr, o_ref):
    qb = (q * (HEAD_DIM ** -0.5 * math.log2(math.e))).astype(BF16)
    s_len = kbd_scr.shape[2]
    for j in range(ATT_KV_HEADS):
        kbd = kbd_scr[j].reshape(ATT_GROUP * s_len, GROUP_W)
        s_all = _dot_nt(kbd, qb[:, j * GROUP_W:(j + 1) * GROUP_W])
        outs = []
        for hh in range(ATT_GROUP):
            s = s_all[hh * s_len:(hh + 1) * s_len, :]
            m = jnp.max(s, axis=0, keepdims=True)
            e = jnp.exp2(s - m)
            den = jnp.sum(e, axis=0, keepdims=True)
            outs.append(_dot(vt_scr[j * HEAD_DIM:(j + 1) * HEAD_DIM, :], e.astype(BF16)) / den)
        o_ref[:, j * GROUP_W:(j + 1) * GROUP_W] = jnp.concatenate(outs, axis=0).T.astype(o_ref.dtype)


def _attn_ctx_kernel(q_ref, kv_ref, qg_ref, kg_ref, gsum_ref, o_ref, kout_ref, vout_ref, kbd_scr, vt_scr):
    gsum = gsum_ref[...]
    seq = vt_scr.shape[-1]
    for i in range(vt_scr.shape[0]):
        rows = pl.ds(i * seq, seq)
        kv = kv_ref[rows, :]
        k = _head_rms(kv[:, :128], gsum[:128, :128], kg_ref[...])
        v = kv[:, 128:]
        kout_ref[rows, :] = k
        vout_ref[rows, :] = v
        _store_group_keys(k, kbd_scr.at[i], 0)
        vt_scr[i] = v.T.astype(BF16)
        q = _head_rms(q_ref[rows, :], gsum, qg_ref[...])
        _attend_heads(q, kbd_scr.at[i], vt_scr.at[i], o_ref.at[rows])


def _attn_lat_kernel(q_ref, kv_ref, qg_ref, kg_ref, gsum_ref, cq_ref, sq_ref, ck_ref, sk_ref,
                     cache_k_ref, cache_v_ref, o_ref, kbd_scr, vt_scr):
    gsum = gsum_ref[...]
    past = cache_k_ref.shape[0]

    @pl.when(pl.program_id(1) == 0)
    def _():
        kv = kv_ref[...]
        k = _head_rms(kv[:, :128], gsum[:128, :128], kg_ref[...])
        k = _rope(k, ck_ref[...], sk_ref[...])
        _store_group_keys(cache_k_ref[...], kbd_scr, 0)
        _store_group_keys(k, kbd_scr, past)
        vt_scr[:, :past] = cache_v_ref[...].T.astype(BF16)
        vt_scr[:, past:] = kv[:, 128:].T.astype(BF16)

    q = _head_rms(q_ref[...], gsum, qg_ref[...])
    q = _rope(q, cq_ref[...], sq_ref[...])
    _attend_heads(q, kbd_scr, vt_scr, o_ref)


def _attention_ctx(qkv, qg, kg, gsum, nb, seq):
    t = qkv.shape[0]
    per = 4 if nb % 4 == 0 else 1
    rows = per * seq
    const = lambda shape: pl.BlockSpec(shape, lambda b: (0, 0))
    return pl.pallas_call(
        _attn_ctx_kernel,
        grid=(nb // per,),
        in_specs=[
            pl.BlockSpec((rows, 512), lambda b: (b, 0)),
            pl.BlockSpec((rows, 256), lambda b: (b, 2)),
            const((1, 512)), const((1, 128)), const((512, 512)),
        ],
        out_specs=[
            pl.BlockSpec((rows, 512), lambda b: (b, 0)),
            pl.BlockSpec((rows, 128), lambda b: (b, 0)),
            pl.BlockSpec((rows, 128), lambda b: (b, 0)),
        ],
        out_shape=[
            jax.ShapeDtypeStruct((t, 512), BF16),
            jax.ShapeDtypeStruct((t, 128), F32),
            jax.ShapeDtypeStruct((t, 128), F32),
        ],
        scratch_shapes=[pltpu.VMEM((per, ATT_KV_HEADS, ATT_GROUP, seq, GROUP_W), BF16),
                        pltpu.VMEM((per, 128, seq), BF16)],
        compiler_params=_params("arbitrary"),
        name="attn_ctx",
    )(qkv, qkv, qg, kg, gsum)


def _attention_lat(qkv, qg, kg, gsum, rope_tabs, cache_k, cache_v, nb, seq):
    t = qkv.shape[0]
    qblk = 256
    nq = seq // qblk
    past = cache_k.shape[1]
    cq, sq, ck, sk = rope_tabs
    const = lambda shape: pl.BlockSpec(shape, lambda b, i: (0, 0))
    return pl.pallas_call(
        _attn_lat_kernel,
        grid=(nb, nq),
        in_specs=[
            pl.BlockSpec((qblk, 512), lambda b, i: (b * nq + i, 0)),
            pl.BlockSpec((seq, 256), lambda b, i: (b, 2)),
            const((1, 512)), const((1, 128)), const((512, 512)),
            pl.BlockSpec((qblk, 512), lambda b, i: (i, 0)),
            pl.BlockSpec((qblk, 512), lambda b, i: (i, 0)),
            const((seq, 128)), const((seq, 128)),
            pl.BlockSpec((None, past, 128), lambda b, i: (b, 0, 0)),
            pl.BlockSpec((None, past, 128), lambda b, i: (b, 0, 0)),
        ],
        out_specs=pl.BlockSpec((qblk, 512), lambda b, i: (b * nq + i, 0)),
        out_shape=jax.ShapeDtypeStruct((t, 512), BF16),
        scratch_shapes=[pltpu.VMEM((ATT_KV_HEADS, ATT_GROUP, past + seq, GROUP_W), BF16),
                        pltpu.VMEM((128, past + seq), BF16)],
        compiler_params=_params("arbitrary", "arbitrary"),
        name="attn_lat",
    )(qkv, qkv, qg, kg, gsum, cq, sq, ck, sk, cache_k, cache_v)


def _fourier_kernel(f_ref, bdh_ref, bdl_ref, cl_ref, sl_ref, o_ref, *, scale):
    bdh = bdh_ref[...]
    bdl = bdl_ref[...]
    cl = cl_ref[...].astype(BF16)
    sl = sl_ref[...].astype(BF16)
    seq = cl_ref.shape[0]
    for i in range(f_ref.shape[0] // seq):
        rows = pl.ds(i * seq, seq)
        xh, xl = _split2(f_ref[rows, :])
        yc, ys = [], []
        for g in range(MIX_W // FNET_GW):
            gl = slice(g * FNET_GW, (g + 1) * FNET_GW)
            y = _dot(xh[:, gl], bdh) + _dot(xl[:, gl], bdh) + _dot(xh[:, gl], bdl)
            yc.append(y[:, :FNET_GW])
            ys.append(y[:, FNET_GW:])
        re = _dot(cl, jnp.concatenate(yc, axis=1).astype(BF16))
        im = _dot(sl, jnp.concatenate(ys, axis=1).astype(BF16))
        o_ref[rows, :] = ((re - im) * scale).astype(o_ref.dtype)


def _dft_consts(seq):
    def cs(n):
        k = np.arange(n, dtype=np.int64)
        ang = 2.0 * np.pi * ((k[:, None] * k[None, :]) % n).astype(np.float64) / n
        return np.cos(ang), np.sin(ang)

    bd = np.concatenate(cs(FNET_GW), axis=1)
    cl, sl_ = cs(seq)

    def hl(a):
        a32 = jnp.asarray(a, F32)
        hi = a32.astype(BF16)
        return hi, (a32 - hi.astype(F32)).astype(BF16)

    return hl(bd) + (jnp.asarray(cl, F32), jnp.asarray(sl_, F32))


def _fourier(f, nb, seq):
    t = f.shape[0]
    consts = _dft_consts(seq)
    const = lambda a: pl.BlockSpec(a.shape, lambda b: (0, 0))
    per = 4 if (seq <= 256 and nb % 4 == 0) else 1
    return pl.pallas_call(
        functools.partial(_fourier_kernel, scale=1.0 / math.sqrt(seq * FNET_GW)),
        grid=(nb // per,),
        in_specs=[pl.BlockSpec((per * seq, MIX_W), lambda b: (b, 0))] + [const(a) for a in consts],
        out_specs=pl.BlockSpec((per * seq, MIX_W), lambda b: (b, 0)),
        out_shape=jax.ShapeDtypeStruct((t, MIX_W), BF16),
        compiler_params=_params("arbitrary"),
        name="fourier",
    )(f, *consts)


def _ssd_kernel(*refs, seq, has_h0, emit_state, has_prev):
    (xbc_ref, z_ref, dt_ref, cw_ref, cb_ref, bias_ref, alog_ref, dskip_ref, ng_ref, tri_ref) = refs[:10]
    rest = list(refs[10:])
    h0_ref = rest.pop(0) if has_h0 else None
    sprev_ref = rest.pop(0) if has_prev else None
    o_ref = rest.pop(0)
    sfin_ref = rest.pop(0) if emit_state else None
    xt_scr, bc_scr, tab, tabt, yt_scr, s_scr = rest
    nc = seq // SSM_CHUNK
    q = SSM_CHUNK

    lane = lax.broadcasted_iota(jnp.int32, (q, 128), 1)
    ii = lax.broadcasted_iota(jnp.int32, (q, q), 0)
    jj = lax.broadcasted_iota(jnp.int32, (q, q), 1)
    below = ii > jj
    above = ii < jj
    row = lax.broadcasted_iota(jnp.int32, (q, 1), 0)
    cw = cw_ref[...]
    amul = jnp.where(lane < 16, 1.0, jnp.where(lane < 48, -jnp.exp(alog_ref[...]), 0.0))
    tri = tri_ref[...]

    if has_h0:
        s_scr[...] = h0_ref[...]
    else:
        s_scr[...] = jnp.zeros_like(s_scr)

    def prep(c, carry):
        r0 = pl.multiple_of(c * q, q)
        cur = xbc_ref[pl.ds(r0, q), :]
        prev8 = xbc_ref[pl.ds(pl.multiple_of(jnp.maximum(r0 - 8, 0), 8), 8), :]
        next8 = xbc_ref[pl.ds(pl.multiple_of(jnp.minimum(r0 + q, seq - 8), 8), 8), :]
        prev_row = jnp.where(c > 0, prev8[7:8, :], 0.0)
        next_row = jnp.where(c < nc - 1, next8[0:1, :], 0.0)
        up = jnp.where(row == 0, prev_row, pltpu.roll(cur, 1, 0))
        dn = jnp.where(row == q - 1, next_row, pltpu.roll(cur, q - 1, 0))
        conv = up * cw[0:1, :] + cur * cw[1:2, :] + dn * cw[2:3, :] + cb_ref[...]
        act = conv * _sigmoid(conv)
        xt_scr[c] = act[:, :SSM_D_INNER].T
        bc_scr[pl.ds(r0, q), :] = act[:, SSM_D_INNER:].astype(BF16)

        raw = dt_ref[pl.ds(r0, q), :]
        rep = jnp.where(lane < 16, raw, jnp.where(lane < 32, pltpu.roll(raw, 16, 1), pltpu.roll(raw, 32, 1)))
        v = rep + bias_ref[...]
        sp = jnp.maximum(v, 0.0) + jnp.log1p(jnp.exp(-jnp.abs(v)))
        t = sp * amul
        hi = t.astype(BF16)
        r1 = t - hi.astype(F32)
        mid = r1.astype(BF16)
        lo = (r1 - mid.astype(F32)).astype(BF16)
        cum = _dot(tri, hi) + _dot(tri, mid) + _dot(tri, lo)
        tc = jnp.where(lane < 32, t, cum)
        tab[pl.ds(r0, q), :] = tc
        tabt[c] = tc.T
        return carry

    lax.fori_loop(0, nc, prep, 0, unroll=nc)

    def pass_fwd(c, carry):
        r0 = pl.multiple_of(c * q, q)
        pc = tab[pl.ds(r0, q), :]
        pt = tabt[c]
        for g in range(2):
            bg = bc_scr[pl.ds(r0, q), 128 * g:128 * (g + 1)]
            cg = bc_scr[pl.ds(r0, q), 256 + 128 * g:256 + 128 * (g + 1)]
            cb = _dot_nt(cg, bg)
            y_in = _dot_nt(s_scr[0, 256 * g:256 * (g + 1), :].astype(BF16), cg)
            for hh in range(4):
                h = 4 * g + hh
                acf_row = pt[32 + h:33 + h, :]
                eb_col = pc[:, 40 + h:41 + h] - pc[:, 24 + h:25 + h]
                eb_row = pt[40 + h:41 + h, :] - pt[24 + h:25 + h, :]
                dtf_row = pt[h:h + 1, :]
                dtb_row = pt[8 + h:9 + h, :]
                seg = jnp.where(below, pc[:, 32 + h:33 + h] - acf_row, jnp.where(above, eb_row - eb_col, 0.0))
                wgt = jnp.where(below, dtf_row, jnp.where(above, dtb_row, dtf_row + dtb_row))
                mp = cb * (jnp.exp(seg) * wgt)
                xth = xt_scr[c, 64 * h:64 * (h + 1), :]
                yd = _dot_nt(xth.astype(BF16), mp.astype(BF16))
                yo = y_in[64 * hh:64 * (hh + 1), :] * jnp.exp(acf_row)
                yt_scr[c, 64 * h:64 * (h + 1), :] = yd + yo + dskip_ref[h] * xth
                tot = pt[32 + h:33 + h, q - 1:q]
                wf_row = dtf_row * jnp.exp(tot - acf_row)
                st = _dot((xth * wf_row).astype(BF16), bg)
                s_scr[0, 64 * h:64 * (h + 1), :] = s_scr[0, 64 * h:64 * (h + 1), :] * jnp.exp(tot) + st
        return carry

    lax.fori_loop(0, nc, pass_fwd, 0, unroll=nc)

    def pass_bwd(ci, carry):
        c = nc - 1 - ci
        r0 = pl.multiple_of(c * q, q)
        pt = tabt[c]
        for g in range(2):
            bg = bc_scr[pl.ds(r0, q), 128 * g:128 * (g + 1)]
            cg = bc_scr[pl.ds(r0, q), 256 + 128 * g:256 + 128 * (g + 1)]
            y_in = _dot_nt(s_scr[1, 256 * g:256 * (g + 1), :].astype(BF16), cg)
            for hh in range(4):
                h = 4 * g + hh
                eb_row = pt[40 + h:41 + h, :] - pt[24 + h:25 + h, :]
                tot = pt[40 + h:41 + h, q - 1:q]
                xth = xt_scr[c, 64 * h:64 * (h + 1), :]
                yt_scr[c, 64 * h:64 * (h + 1), :] += y_in[64 * hh:64 * (hh + 1), :] * jnp.exp(tot - eb_row)
                wb_row = pt[8 + h:9 + h, :] * jnp.exp(eb_row)
                st = _dot((xth * wb_row).astype(BF16), bg)
                s_scr[1, 64 * h:64 * (h + 1), :] = s_scr[1, 64 * h:64 * (h + 1), :] * jnp.exp(tot) + st
        zc = z_ref[pl.ds(r0, q), :]
        yg = yt_scr[c].T * (zc * _sigmoid(zc))
        ms = jnp.mean(yg * yg, axis=-1, keepdims=True)
        o_ref[pl.ds(r0, q), :] = (yg * lax.rsqrt(ms + EPS) * ng_ref[...]).astype(o_ref.dtype)
        return carry

    lax.fori_loop(0, nc, pass_bwd, 0, unroll=nc)

    if emit_state and has_prev:
        sfin_ref[0] = sprev_ref[...]
        sfin_ref[1] = s_scr[...]
    elif emit_state:
        sfin_ref[...] = s_scr[...]


def _ssd(xbc, z, dt, conv_w, conv_b, bias128, alog128, dskip, norm_g, tri, h0, s_prev, nb, seq, emit_state):
    t = xbc.shape[0]
    has_h0 = h0 is not None
    has_prev = s_prev is not None
    const = lambda shape: pl.BlockSpec(shape, lambda b: (0,) * len(shape))
    in_specs = [
        pl.BlockSpec((seq, SSM_CONV_CH), lambda b: (b, 0)),
        pl.BlockSpec((seq, SSM_D_INNER), lambda b: (b, 0)),
        pl.BlockSpec((seq, 128), lambda b: (b, 0)),
        const((3, SSM_CONV_CH)), const((1, SSM_CONV_CH)), const((1, 128)), const((1, 128)),
        pl.BlockSpec(memory_space=pltpu.SMEM), const((1, SSM_D_INNER)), const((128, 128)),
    ]
    args = [xbc, z, dt, conv_w, conv_b, bias128, alog128, dskip, norm_g, tri]
    if has_h0:
        in_specs.append(pl.BlockSpec((None, 2, 512, 128), lambda b: (b, 0, 0, 0)))
        args.append(h0)
    if has_prev:
        in_specs.append(pl.BlockSpec((None, 2, 512, 128), lambda b: (b, 0, 0, 0)))
        args.append(s_prev)
    out_specs = [pl.BlockSpec((seq, SSM_D_INNER), lambda b: (b, 0))]
    out_shape = [jax.ShapeDtypeStruct((t, SSM_D_INNER), BF16)]
    if emit_state and has_prev:
        out_specs.append(pl.BlockSpec((None, 2, 2, 512, 128), lambda b: (b, 0, 0, 0, 0)))
        out_shape.append(jax.ShapeDtypeStruct((nb, 2, 2, 512, 128), F32))
    elif emit_state:
        out_specs.append(pl.BlockSpec((None, 2, 512, 128), lambda b: (b, 0, 0, 0)))
        out_shape.append(jax.ShapeDtypeStruct((nb, 2, 512, 128), F32))
    return pl.pallas_call(
        functools.partial(_ssd_kernel, seq=seq, has_h0=has_h0, emit_state=emit_state, has_prev=has_prev),
        grid=(nb,),
        in_specs=in_specs,
        out_specs=out_specs,
        out_shape=out_shape,
        scratch_shapes=[
            pltpu.VMEM((seq // SSM_CHUNK, SSM_D_INNER, SSM_CHUNK), F32),
            pltpu.VMEM((seq, 512), BF16),
            pltpu.VMEM((seq, 128), F32),
            pltpu.VMEM((seq // SSM_CHUNK, 128, SSM_CHUNK), F32),
            pltpu.VMEM((seq // SSM_CHUNK, SSM_D_INNER, SSM_CHUNK), F32),
            pltpu.VMEM((2, 512, 128), F32),
        ],
        compiler_params=_params("arbitrary"),
        name="ssd",
    )(*args)


def _merge_kernel(x_ref, h_ref, g_ref, b0, b1, b2, b3, wg0, wg1, wg2, wg3, wb_ref, wo_ref, o_ref):
    n = pl.program_id(1)

    @pl.when(n == 0)
    def _():
        o_ref[...] = x_ref[...]

    h = h_ref[...]
    merged = None
    for k, (b_ref, wg_ref) in enumerate(((b0, wg0), (b1, wg1), (b2, wg2), (b3, wg3))):
        gate = _sigmoid(_dot_nt(h, wg_ref[0]))
        term = gate * _dot(b_ref[...], wb_ref[k])
        merged = term if merged is None else merged + term
    o_ref[...] += g_ref[...] * _dot(merged.astype(BF16), wo_ref[...])


def _merge(x, h, mod3, branches, wt, w_branch, w_out, layer, row_fn):
    t = x.shape[0]
    tm, tn = 512, 512
    assert t % tm == 0
    nn = D_MODEL // tn
    rf = lambda m: row_fn(m * tm)
    gate_spec = lambda k: pl.BlockSpec(
        (pl.Element(1), pl.Element(tn), pl.Element(D_MODEL)),
        lambda m, n: (0, pl.multiple_of(GATE_COL0 + k * D_MODEL + n * tn, 16), 0))
    return pl.pallas_call(
        _merge_kernel,
        grid=(t // tm, nn),
        in_specs=[
            pl.BlockSpec((tm, D_MODEL), lambda m, n: (m, 0)),
            pl.BlockSpec((tm, D_MODEL), lambda m, n: (m, 0)),
            _mod_spec(2, rf),
        ] + [pl.BlockSpec((tm, MIX_W), lambda m, n: (m, 0))] * 4
          + [gate_spec(k) for k in range(4)] + [
            pl.BlockSpec((None, N_BRANCH, MIX_W, tn), lambda m, n: (layer, 0, 0, n)),
            pl.BlockSpec((None, tn, D_MODEL), lambda m, n: (layer, n, 0)),
        ],
        out_specs=pl.BlockSpec((tm, D_MODEL), lambda m, n: (m, 0)),
        out_shape=jax.ShapeDtypeStruct((t, D_MODEL), F32),
        compiler_params=_params("arbitrary", "arbitrary"),
        name="merge",
    )(x, h, mod3, *branches, wt, wt, wt, wt, w_branch, w_out)


def _ffn_kernel(x_ref, sc_ref, sh_ref, g_ref, w1_ref, w2_ref, o_ref, h_scr):
    n = pl.program_id(1)

    @pl.when(n == 0)
    def _():
        _rms_mod_rows(x_ref, sc_ref, sh_ref, h_scr)
        o_ref[...] = x_ref[...]

    a = jnp.maximum(_dot(h_scr[...], w1_ref[...].astype(BF16)), 0.0)
    o_ref[...] += g_ref[...] * _dot((a * a).astype(BF16), w2_ref[...].astype(BF16))


def _ffn(x, mod3, w1, w2, layer, row_fn):
    t = x.shape[0]
    tm, tf = 1024, 512
    assert t % tm == 0
    rf = lambda m: row_fn(m * tm)
    return pl.pallas_call(
        _ffn_kernel,
        grid=(t // tm, D_FF // tf),
        in_specs=[
            pl.BlockSpec((tm, D_MODEL), lambda m, n: (m, 0), pipeline_mode=pl.Buffered(1)),
            _mod_spec(4, rf), _mod_spec(3, rf), _mod_spec(5, rf),
            pl.BlockSpec((None, D_MODEL, tf), lambda m, n: (layer, 0, n)),
            pl.BlockSpec((None, tf, D_MODEL), lambda m, n: (layer, n, 0)),
        ],
        out_specs=pl.BlockSpec((tm, D_MODEL), lambda m, n: (m, 0)),
        out_shape=jax.ShapeDtypeStruct((t, D_MODEL), F32),
        scratch_shapes=[pltpu.VMEM((tm, D_MODEL), BF16)],
        compiler_params=_params("arbitrary", "arbitrary"),
        name="ffn",
    )(x, mod3, mod3, mod3, w1, w2)


def _rope_tables(seq):
    half = 16
    freqs = ROPE_THETA ** (-jnp.arange(half, dtype=F32) / half)
    n_rows = seq // GRID_W
    rows = jnp.repeat(jnp.arange(n_rows), GRID_W).astype(F32)
    cols = jnp.tile(jnp.arange(GRID_W), n_rows).astype(F32)
    ar = rows[:, None] * freqs[None, :]
    ac = cols[:, None] * freqs[None, :]
    cos64 = jnp.concatenate([jnp.cos(ar), jnp.cos(ar), jnp.cos(ac), jnp.cos(ac)], axis=-1)
    sin64 = jnp.concatenate([-jnp.sin(ar), jnp.sin(ar), -jnp.sin(ac), jnp.sin(ac)], axis=-1)
    return (jnp.tile(cos64, (1, ATT_HEADS)), jnp.tile(sin64, (1, ATT_HEADS)),
            jnp.tile(cos64, (1, ATT_KV_HEADS)), jnp.tile(sin64, (1, ATT_KV_HEADS)))


def _layer_path(x, mod3, lw, row_fn, nb, seq, lat, s_prev=None, side_casts=()):
    (wt, qg, kg, gsum, conv_w, conv_b, bias128, alog128, dskip, ssm_g, tri, gmlp_g, ws, bs_t,
     w_branch, w_out, w_ff1, w_ff2, layer) = lw
    h, qkv, f_in, xbc, z, dt, o_gmlp, *side = _inproj(x, mod3, wt, gmlp_g, ws, bs_t, row_fn, side_casts)
    if w_branch is None:
        w_branch = side[-2].reshape(DEPTH, N_BRANCH, MIX_W, D_MODEL)
        w_out = side[-1].reshape(DEPTH, D_MODEL, D_MODEL)
    if lat is None:
        o_att, k_new, v_new = _attention_ctx(qkv, qg, kg, gsum, nb, seq)
        h0 = None
    else:
        rope_tabs, cache_k, cache_v, h0 = lat
        o_att = _attention_lat(qkv, qg, kg, gsum, rope_tabs, cache_k, cache_v, nb, seq)
        k_new = v_new = None
    o_fnet = _fourier(f_in, nb, seq)
    ssd_out = _ssd(xbc, z, dt, conv_w, conv_b, bias128, alog128, dskip, ssm_g, tri, h0, s_prev, nb, seq,
                   emit_state=lat is None)
    o_ssm = ssd_out[0]
    s_fin = ssd_out[1] if lat is None else None
    x = _merge(x, h, mod3, (o_att, o_fnet, o_ssm, o_gmlp), wt, w_branch, w_out, layer, row_fn)
    x = _ffn(x, mod3, w_ff1, w_ff2, layer, row_fn)
    return x, k_new, v_new, s_fin, side, w_branch, w_out


def kernel(x_prompt, x_sample, c, cache_k, cache_v, state_ssm, c_ctx, w_mod, b_mod, w_in, q_norm_g, k_norm_g,
           conv_w, conv_b, a_log, dt_bias, d_skip, ssm_norm_g, gmlp_norm_g, w_spatial, b_spatial, w_branch,
           w_out, w_ff1, w_ff2):
    nb_ctx, seq_ctx, _ = x_prompt.shape
    nb_lat, seq_lat, _ = x_sample.shape
    assert seq_lat == LAT_LEN and 1 + nb_lat <= 8

    cond8 = jnp.zeros((8, D_MODEL), F32).at[0].set(c_ctx).at[1:1 + nb_lat].set(c)
    mod = _modulation(cond8, w_mod, b_mod)

    gsum = jnp.asarray(np.kron(np.eye(ATT_HEADS), np.ones((HEAD_DIM, HEAD_DIM))), BF16)
    tri = jnp.asarray(np.tril(np.ones((SSM_CHUNK, SSM_CHUNK))), BF16)
    rope_tabs = _rope_tables(seq_lat)

    xp = x_prompt.reshape(nb_ctx * seq_ctx, D_MODEL)
    xs = x_sample.reshape(nb_lat * seq_lat, D_MODEL)
    ctx_row = lambda tok: 0
    lat_row = lambda tok: 1 + tok // LAT_LEN

    w_in_t = jnp.swapaxes(w_in, 1, 2)
    wt = _cast_layer_bf16(w_in_t, 0)
    wb_bf, wo_bf = None, None

    assert DEPTH == 2
    new_k, new_v, s_l = [], [], None
    for l in range(DEPTH):
        zeros80 = jnp.zeros((128 - 48,), F32)
        bias128 = jnp.concatenate([jnp.tile(dt_bias[l].reshape(16), 3), zeros80]).reshape(1, 128)
        alog128 = jnp.concatenate([jnp.zeros((16,), F32), jnp.tile(a_log[l].reshape(16), 2), zeros80]).reshape(1, 128)
        lw = (
            wt,
            jnp.tile(q_norm_g[l], ATT_HEADS).reshape(1, 512), jnp.tile(k_norm_g[l], ATT_KV_HEADS).reshape(1, 128),
            gsum, conv_w[l], conv_b[l].reshape(1, SSM_CONV_CH), bias128, alog128,
            d_skip[l], ssm_norm_g[l].reshape(1, SSM_D_INNER), tri,
            gmlp_norm_g[l].reshape(1, MIX_W), w_spatial[l].astype(BF16), b_spatial[l].T,
            wb_bf, wo_bf, w_ff1, w_ff2, l,
        )
        mod3 = mod[l].reshape(8, 1, 6 * D_MODEL)
        side_casts = [(w_in_t, l + 1)] if l + 1 < DEPTH else []
        if l == 0:
            side_casts += [(w_branch.reshape(-1, D_MODEL), None), (w_out.reshape(-1, D_MODEL), None)]
        xp, k_l, v_l, s_l, side, wb_bf, wo_bf = _layer_path(xp, mod3, lw, ctx_row, nb_ctx, seq_ctx, None,
                                                            s_prev=s_l, side_casts=side_casts)
        lw = lw[:-5] + (wb_bf, wo_bf) + lw[-3:]
        new_k.append(k_l.reshape(nb_ctx, seq_ctx, ATT_KV_HEADS, HEAD_DIM))
        new_v.append(v_l.reshape(nb_ctx, seq_ctx, ATT_KV_HEADS, HEAD_DIM))
        lat = (rope_tabs,
               cache_k[:, l].reshape(nb_lat, -1, ATT_KV_HEADS * HEAD_DIM),
               cache_v[:, l].reshape(nb_lat, -1, ATT_KV_HEADS * HEAD_DIM),
               state_ssm[:, l].reshape(nb_lat, 2, SSM_HEADS * SSM_HEAD_DIM, SSM_STATE))
        xs = _layer_path(xs, mod3, lw, lat_row, nb_lat, seq_lat, lat)[0]
        if l + 1 < DEPTH:
            wt = side[0]

    return (xp.reshape(nb_ctx, seq_ctx, D_MODEL), xs.reshape(nb_lat, seq_lat, D_MODEL),
            jnp.stack(new_k, axis=1), jnp.stack(new_v, axis=1),
            s_l.reshape(nb_ctx, DEPTH, 2, SSM_HEADS, SSM_HEAD_DIM, SSM_STATE))
```

```python
import functools
import math

import numpy as np
import jax
import jax.numpy as jnp
from jax import lax
from jax.experimental import pallas as pl
from jax.experimental.pallas import tpu as pltpu

F32 = jnp.float32
BF16 = jnp.bfloat16

D_MODEL = 2048
DEPTH = 2
GRID_W = 64
MIX_W = 512
N_BRANCH = 4
ATT_HEADS = 8
ATT_KV_HEADS = 2
HEAD_DIM = 64
ROPE_THETA = 10000.0
FNET_GW = 128
SSM_HEADS = 8
SSM_HEAD_DIM = 64
SSM_STATE = 128
SSM_CHUNK = 128
SSM_D_INNER = 512
SSM_CONV_CH = 1024
GMLP_CHUNK = 128
D_FF = 4 * D_MODEL
EPS = 1e-6
LAT_LEN = 1024

QKV_W = 768
COL_F = 768
COL_XBC = 1280
COL_Z = 2304
COL_DT = 2816
COL_UV = 2832
GATE_COL0 = 3856
N_IN = GATE_COL0 + N_BRANCH * D_MODEL
_IN_PIECES = ((0, QKV_W), (COL_F, COL_XBC), (COL_XBC, COL_Z), (COL_Z, COL_DT),
              (COL_DT, COL_DT + 128),
              (COL_UV, GATE_COL0))

VMEM_LIMIT = 56 * 1024 * 1024
NEG_BIG = -1e30


def _params(*sem):
    return pltpu.CompilerParams(dimension_semantics=sem, vmem_limit_bytes=VMEM_LIMIT)


def _dot(a, b):
    return jnp.dot(a, b, preferred_element_type=F32)


def _dot_nt(a, b):
    return lax.dot_general(a, b, (((1,), (1,)), ((), ())), preferred_element_type=F32)


def _dot_tn(a, b):
    return lax.dot_general(a, b, (((0,), (0,)), ((), ())), preferred_element_type=F32)


def _sigmoid(x):
    return 1.0 / (1.0 + jnp.exp(-x))


def _split2(x):
    hi = x.astype(BF16)
    lo = (x - hi.astype(F32)).astype(BF16)
    return hi, lo


def _rms_mod(x, sc, sh):
    ms = jnp.mean(x * x, axis=-1, keepdims=True)
    return x * lax.rsqrt(ms + EPS) * (1.0 + sc) + sh


def _rms_mod_rows(x_ref, sc_ref, sh_ref, dst_ref):
    slab = 16

    def body(i, carry):
        rows = pl.ds(pl.multiple_of(i * slab, slab), slab)
        dst_ref[rows, :] = _rms_mod(x_ref[rows, :], sc_ref[...], sh_ref[...]).astype(dst_ref.dtype)
        return carry

    lax.fori_loop(0, x_ref.shape[0] // slab, body, 0, unroll=8)


def _mod_kernel(cond_ref, w_ref, b_ref, o_ref):
    c = cond_ref[...]
    s = c * _sigmoid(c)
    o_ref[...] = _dot(s.astype(BF16), w_ref[...].astype(BF16)) + b_ref[...]


def _modulation(cond8, w_mod, b_mod):
    tn = 1024
    n6 = 6 * D_MODEL
    return pl.pallas_call(
        _mod_kernel,
        grid=(DEPTH, n6 // tn),
        in_specs=[
            pl.BlockSpec((8, D_MODEL), lambda l, n: (0, 0)),
            pl.BlockSpec((None, D_MODEL, tn), lambda l, n: (l, 0, n)),
            pl.BlockSpec((None, 1, tn), lambda l, n: (l, 0, n)),
        ],
        out_specs=pl.BlockSpec((None, 8, tn), lambda l, n: (l, 0, n)),
        out_shape=jax.ShapeDtypeStruct((DEPTH, 8, n6), F32),
        compiler_params=_params("arbitrary", "arbitrary"),
        name="modulation",
    )(cond8, w_mod, b_mod.reshape(DEPTH, 1, n6))


def _mod_spec(chunk, row_fn):
    return pl.BlockSpec((None, 1, D_MODEL), lambda m, *_: (row_fn(m), 0, chunk))


def _cast_kernel(w_ref, o_ref):
    o_ref[...] = w_ref[...].astype(o_ref.dtype)


def _cast_layer_bf16(w, layer, n_rows):
    rows = 1024
    return pl.pallas_call(
        _cast_kernel,
        grid=(pl.cdiv(n_rows, rows),),
        in_specs=[pl.BlockSpec((None, rows, D_MODEL), lambda r: (layer, r, 0))],
        out_specs=pl.BlockSpec((None, rows, D_MODEL), lambda r: (0, r, 0)),
        out_shape=jax.ShapeDtypeStruct((1, n_rows, D_MODEL), BF16),
        compiler_params=_params("arbitrary"),
        name="cast_layer_bf16",
    )(w)


def _gmlp_chunk(x, gain, ws_ref, bs_ref):
    c0 = math.sqrt(2.0 / math.pi)
    ge = x * (0.5 * (1.0 + jnp.tanh(c0 * (x + 0.044715 * (x * x * x)))))
    u = ge[:, :MIX_W]
    v = ge[:, MIX_W:]
    v = v * lax.rsqrt(jnp.mean(v * v, axis=-1, keepdims=True) + EPS) * gain
    outs = []
    for g in range(MIX_W // 128):
        gl = slice(g * 128, (g + 1) * 128)
        mixed = _dot(ws_ref[g], v[:, gl].astype(BF16)) + bs_ref[:, g:g + 1]
        outs.append(u[:, gl] * mixed)
    return jnp.concatenate(outs, axis=1)


def _inproj_kernel(*refs, n_side):
    x_ref, sc_ref, sh_ref, w_ref, gg_ref, ws_ref, bs_ref = refs[:7]
    side_in = refs[7:7 + n_side]
    h_ref = refs[7 + n_side]
    out_refs = refs[8 + n_side:len(refs) - n_side]
    side_out = refs[len(refs) - n_side:]
    for src, dst in zip(side_in, side_out):
        dst[...] = src[...].astype(dst.dtype)
    hb = _rms_mod(x_ref[...], sc_ref[...], sh_ref[...]).astype(BF16)
    h_ref[...] = hb
    o_gmlp = out_refs[-1]
    a, b = _IN_PIECES[-1]
    uv = _dot_nt(hb, w_ref[a:b, :])
    for c in range(x_ref.shape[0] // GMLP_CHUNK):
        sl = slice(c * GMLP_CHUNK, (c + 1) * GMLP_CHUNK)
        o_gmlp[sl, :] = _gmlp_chunk(uv[sl, :], gg_ref[...], ws_ref, bs_ref).astype(o_gmlp.dtype)
    for o_ref, (a, b) in zip(out_refs[:-1], _IN_PIECES[:-1]):
        o_ref[...] = _dot_nt(hb, w_ref[a:b, :])


def _inproj(x, mod3, wt, gmlp_g, ws, bs_t, row_fn, side_casts=()):
    t = x.shape[0]
    tm = 256
    assert t % tm == 0
    steps = t // tm
    widths = [b - a for a, b in _IN_PIECES[:-1]]
    const = lambda shape: pl.BlockSpec(shape, lambda m: (0,) * len(shape))
    in_specs = [
        pl.BlockSpec((tm, D_MODEL), lambda m: (m, 0)),
        _mod_spec(1, lambda m: row_fn(m * tm)),
        _mod_spec(0, lambda m: row_fn(m * tm)),
        pl.BlockSpec((None, GATE_COL0, D_MODEL), lambda m: (0, 0, 0), pipeline_mode=pl.Buffered(1)),
        const((1, MIX_W)), const((4, 128, 128)), const((128, 4)),
    ]
    args = [x, mod3, mod3, wt, gmlp_g, ws, bs_t]
    out_specs = ([pl.BlockSpec((tm, D_MODEL), lambda m: (m, 0))]
                 + [pl.BlockSpec((tm, w), lambda m: (m, 0)) for w in widths]
                 + [pl.BlockSpec((tm, MIX_W), lambda m: (m, 0))])
    out_shape = ([jax.ShapeDtypeStruct((t, D_MODEL), BF16)]
                 + [jax.ShapeDtypeStruct((t, w), F32) for w in widths]
                 + [jax.ShapeDtypeStruct((t, MIX_W), BF16)])
    side_in_specs, side_args = [], []
    for w_src, lead, row0, n_rows in side_casts:
        rows = -(-n_rows // (16 * steps)) * 16
        assert pl.cdiv(n_rows, rows) == steps
        if lead is None:
            side_in_specs.append(pl.BlockSpec((rows, D_MODEL), lambda m: (m, 0)))
            out_specs.append(pl.BlockSpec((rows, D_MODEL), lambda m: (m, 0)))
            out_shape.append(jax.ShapeDtypeStruct((n_rows, D_MODEL), BF16))
        elif row0 == 0:
            side_in_specs.append(pl.BlockSpec((None, rows, D_MODEL), lambda m, lead=lead: (lead, m, 0)))
            out_specs.append(pl.BlockSpec((None, rows, D_MODEL), lambda m: (0, m, 0)))
            out_shape.append(jax.ShapeDtypeStruct((1, n_rows, D_MODEL), BF16))
        else:
            assert n_rows == rows * steps and row0 % 16 == 0
            side_in_specs.append(pl.BlockSpec(
                (pl.Element(1), pl.Element(rows), pl.Element(D_MODEL)),
                lambda m, lead=lead, row0=row0, rows=rows: (lead, pl.multiple_of(row0 + m * rows, 16), 0)))
            out_specs.append(pl.BlockSpec((1, rows, D_MODEL), lambda m: (0, m, 0)))
            out_shape.append(jax.ShapeDtypeStruct((1, n_rows, D_MODEL), BF16))
        side_args.append(w_src)
    in_specs += side_in_specs
    args += side_args
    return pl.pallas_call(
        functools.partial(_inproj_kernel, n_side=len(side_casts)),
        grid=(steps,),
        in_specs=in_specs,
        out_specs=out_specs,
        out_shape=out_shape,
        compiler_params=_params("arbitrary"),
        name="inproj",
    )(*args)


def _head_rms(t, gsum, gain):
    t2 = t * t
    hi, lo = _split2(t2)
    ss = _dot(hi, gsum) + _dot(lo, gsum)
    return t * lax.rsqrt(ss * (1.0 / HEAD_DIM) + EPS) * gain


def _rope(t, cos, sin_signed):
    w = t.shape[-1]
    up = pltpu.roll(t, w - 16, 1)
    dn = pltpu.roll(t, 16, 1)
    lane = lax.broadcasted_iota(jnp.int32, t.shape, 1)
    partner = jnp.where((lane & 31) < 16, up, dn)
    return t * cos + partner * sin_signed


ATT_GROUP = ATT_HEADS // ATT_KV_HEADS
GROUP_W = ATT_GROUP * HEAD_DIM


def _store_group_keys(k, kbd_scr, row0):
    s = k.shape[0]
    k2 = jnp.concatenate([k, k], axis=1)
    k2r = pltpu.roll(k2, HEAD_DIM, 1)
    lane = lax.broadcasted_iota(jnp.int32, k2.shape, 1)
    first = (lane & (2 * HEAD_DIM - 1)) < HEAD_DIM
    for j, tiled in enumerate((jnp.where(first, k2, k2r), jnp.where(first, k2r, k2))):
        for hh in range(ATT_GROUP):
            slab = jnp.where(lax.shift_right_logical(lane, 6) == hh, tiled, 0.0)
            kbd_scr[j, hh, row0:row0 + s, :] = slab.astype(kbd_scr.dtype)


def _attend_heads(q, kbd_scr, vt_scr, o_ref):
    qb = (q * (HEAD_DIM ** -0.5 * math.log2(math.e))).astype(BF16)
    s_len = kbd_scr.shape[2]
    for j in range(ATT_KV_HEADS):
        kbd = kbd_scr[j].reshape(ATT_GROUP * s_len, GROUP_W)
        s_all = _dot_nt(kbd, qb[:, j * GROUP_W:(j + 1) * GROUP_W])
        outs = []
        for hh in range(ATT_GROUP):
            s = s_all[hh * s_len:(hh + 1) * s_len, :]
            m = jnp.max(s, axis=0, keepdims=True)
            e = jnp.exp2(s - m)
            den = jnp.sum(e, axis=0, keepdims=True)
            outs.append(_dot(vt_scr[j * HEAD_DIM:(j + 1) * HEAD_DIM, :], e.astype(BF16)) / den)
        o_ref[:, j * GROUP_W:(j + 1) * GROUP_W] = jnp.concatenate(outs, axis=0).T.astype(o_ref.dtype)


def _attn_ctx_kernel(q_ref, kv_ref, qg_ref, kg_ref, gsum_ref, o_ref, kout_ref, vout_ref, kbd_scr, vt_scr):
    gsum = gsum_ref[...]
    seq = vt_scr.shape[-1]
    for i in range(vt_scr.shape[0]):
        rows = pl.ds(i * seq, seq)
        kv = kv_ref[rows, :]
        k = _head_rms(kv[:, :128], gsum[:128, :128], kg_ref[...])
        v = kv[:, 128:]
        kout_ref[rows, :] = k
        vout_ref[rows, :] = v
        _store_group_keys(k, kbd_scr.at[i], 0)
        vt_scr[i] = v.T.astype(BF16)
        q = _head_rms(q_ref[rows, :], gsum, qg_ref[...])
        _attend_heads(q, kbd_scr.at[i], vt_scr.at[i], o_ref.at[rows])


def _attn_lat_kernel(q_ref, kv_ref, qg_ref, kg_ref, gsum_ref, cq_ref, sq_ref, ck_ref, sk_ref,
                     cache_k_ref, cache_v_ref, o_ref, kbd_scr, vt_scr):
    gsum = gsum_ref[...]
    past = cache_k_ref.shape[0]

    @pl.when(pl.program_id(1) == 0)
    def _():
        kv = kv_ref[...]
        k = _head_rms(kv[:, :128], gsum[:128, :128], kg_ref[...])
        k = _rope(k, ck_ref[...], sk_ref[...])
        _store_group_keys(cache_k_ref[...], kbd_scr, 0)
        _store_group_keys(k, kbd_scr, past)
        vt_scr[:, :past] = cache_v_ref[...].T.astype(BF16)
        vt_scr[:, past:] = kv[:, 128:].T.astype(BF16)

    q = _head_rms(q_ref[...], gsum, qg_ref[...])
    q = _rope(q, cq_ref[...], sq_ref[...])
    _attend_heads(q, kbd_scr, vt_scr, o_ref)


def _attention_ctx(qkv, qg, kg, gsum, nb, seq):
    t = qkv.shape[0]
    per = 4 if nb % 4 == 0 else 1
    rows = per * seq
    const = lambda shape: pl.BlockSpec(shape, lambda b: (0, 0))
    return pl.pallas_call(
        _attn_ctx_kernel,
        grid=(nb // per,),
        in_specs=[
            pl.BlockSpec((rows, 512), lambda b: (b, 0)),
            pl.BlockSpec((rows, 256), lambda b: (b, 2)),
            const((1, 512)), const((1, 128)), const((512, 512)),
        ],
        out_specs=[
            pl.BlockSpec((rows, 512), lambda b: (b, 0)),
            pl.BlockSpec((rows, 128), lambda b: (b, 0)),
            pl.BlockSpec((rows, 128), lambda b: (b, 0)),
        ],
        out_shape=[
            jax.ShapeDtypeStruct((t, 512), BF16),
            jax.ShapeDtypeStruct((t, 128), F32),
            jax.ShapeDtypeStruct((t, 128), F32),
        ],
        scratch_shapes=[pltpu.VMEM((per, ATT_KV_HEADS, ATT_GROUP, seq, GROUP_W), BF16),
                        pltpu.VMEM((per, 128, seq), BF16)],
        compiler_params=_params("arbitrary"),
        name="attn_ctx",
    )(qkv, qkv, qg, kg, gsum)


def _attention_lat(qkv, qg, kg, gsum, rope_tabs, cache_k, cache_v, nb, seq):
    t = qkv.shape[0]
    qblk = 256
    nq = seq // qblk
    past = cache_k.shape[1]
    cq, sq, ck, sk = rope_tabs
    const = lambda shape: pl.BlockSpec(shape, lambda b, i: (0, 0))
    return pl.pallas_call(
        _attn_lat_kernel,
        grid=(nb, nq),
        in_specs=[
            pl.BlockSpec((qblk, 512), lambda b, i: (b * nq + i, 0)),
            pl.BlockSpec((seq, 256), lambda b, i: (b, 2)),
            const((1, 512)), const((1, 128)), const((512, 512)),
            pl.BlockSpec((qblk, 512), lambda b, i: (i, 0)),
            pl.BlockSpec((qblk, 512), lambda b, i: (i, 0)),
            const((seq, 128)), const((seq, 128)),
            pl.BlockSpec((None, past, 128), lambda b, i: (b, 0, 0)),
            pl.BlockSpec((None, past, 128), lambda b, i: (b, 0, 0)),
        ],
        out_specs=pl.BlockSpec((qblk, 512), lambda b, i: (b * nq + i, 0)),
        out_shape=jax.ShapeDtypeStruct((t, 512), BF16),
        scratch_shapes=[pltpu.VMEM((ATT_KV_HEADS, ATT_GROUP, past + seq, GROUP_W), BF16),
                        pltpu.VMEM((128, past + seq), BF16)],
        compiler_params=_params("arbitrary", "arbitrary"),
        name="attn_lat",
    )(qkv, qkv, qg, kg, gsum, cq, sq, ck, sk, cache_k, cache_v)


def _fourier_kernel(f_ref, bdh_ref, bdl_ref, cl_ref, sl_ref, o_ref, *, scale):
    bdh = bdh_ref[...]
    bdl = bdl_ref[...]
    cl = cl_ref[...].astype(BF16)
    sl = sl_ref[...].astype(BF16)
    seq = cl_ref.shape[0]
    for i in range(f_ref.shape[0] // seq):
        rows = pl.ds(i * seq, seq)
        xh, xl = _split2(f_ref[rows, :])
        yc, ys = [], []
        for g in range(MIX_W // FNET_GW):
            gl = slice(g * FNET_GW, (g + 1) * FNET_GW)
            y = _dot(xh[:, gl], bdh) + _dot(xl[:, gl], bdh) + _dot(xh[:, gl], bdl)
            yc.append(y[:, :FNET_GW])
            ys.append(y[:, FNET_GW:])
        re = _dot(cl, jnp.concatenate(yc, axis=1).astype(BF16))
        im = _dot(sl, jnp.concatenate(ys, axis=1).astype(BF16))
        o_ref[rows, :] = ((re - im) * scale).astype(o_ref.dtype)


def _dft_consts(seq):
    def cs(n):
        k = np.arange(n, dtype=np.int64)
        ang = 2.0 * np.pi * ((k[:, None] * k[None, :]) % n).astype(np.float64) / n
        return np.cos(ang), np.sin(ang)

    bd = np.concatenate(cs(FNET_GW), axis=1)
    cl, sl_ = cs(seq)

    def hl(a):
        a32 = jnp.asarray(a, F32)
        hi = a32.astype(BF16)
        return hi, (a32 - hi.astype(F32)).astype(BF16)

    return hl(bd) + (jnp.asarray(cl, F32), jnp.asarray(sl_, F32))


def _fourier(f, nb, seq):
    t = f.shape[0]
    consts = _dft_consts(seq)
    const = lambda a: pl.BlockSpec(a.shape, lambda b: (0, 0))
    per = 4 if (seq <= 256 and nb % 4 == 0) else 1
    return pl.pallas_call(
        functools.partial(_fourier_kernel, scale=1.0 / math.sqrt(seq * FNET_GW)),
        grid=(nb // per,),
        in_specs=[pl.BlockSpec((per * seq, MIX_W), lambda b: (b, 0))] + [const(a) for a in consts],
        out_specs=pl.BlockSpec((per * seq, MIX_W), lambda b: (b, 0)),
        out_shape=jax.ShapeDtypeStruct((t, MIX_W), BF16),
        compiler_params=_params("arbitrary"),
        name="fourier",
    )(f, *consts)


def _ssd_kernel(*refs, seq, has_h0, emit_state, has_prev):
    (xbc_ref, z_ref, dt_ref, cw_ref, cb_ref, bias_ref, alog_ref, dskip_ref, ng_ref, tri_ref) = refs[:10]
    rest = list(refs[10:])
    h0_ref = rest.pop(0) if has_h0 else None
    sprev_ref = rest.pop(0) if has_prev else None
    o_ref = rest.pop(0)
    sfin_ref = rest.pop(0) if emit_state else None
    xt_scr, bc_scr, tab, tabt, yt_scr, s_scr = rest
    nc = seq // SSM_CHUNK
    q = SSM_CHUNK

    lane = lax.broadcasted_iota(jnp.int32, (q, 128), 1)
    ii = lax.broadcasted_iota(jnp.int32, (q, q), 0)
    jj = lax.broadcasted_iota(jnp.int32, (q, q), 1)
    below = ii > jj
    above = ii < jj
    row = lax.broadcasted_iota(jnp.int32, (q, 1), 0)
    cw = cw_ref[...]
    amul = jnp.where(lane < 16, 1.0, jnp.where(lane < 48, -jnp.exp(alog_ref[...]), 0.0))
    tri = tri_ref[...]

    if has_h0:
        s_scr[...] = h0_ref[...]
    else:
        s_scr[...] = jnp.zeros_like(s_scr)

    def prep(c, carry):
        r0 = pl.multiple_of(c * q, q)
        cur = xbc_ref[pl.ds(r0, q), :]
        prev8 = xbc_ref[pl.ds(pl.multiple_of(jnp.maximum(r0 - 8, 0), 8), 8), :]
        next8 = xbc_ref[pl.ds(pl.multiple_of(jnp.minimum(r0 + q, seq - 8), 8), 8), :]
        prev_row = jnp.where(c > 0, prev8[7:8, :], 0.0)
        next_row = jnp.where(c < nc - 1, next8[0:1, :], 0.0)
        up = jnp.where(row == 0, prev_row, pltpu.roll(cur, 1, 0))
        dn = jnp.where(row == q - 1, next_row, pltpu.roll(cur, q - 1, 0))
        conv = up * cw[0:1, :] + cur * cw[1:2, :] + dn * cw[2:3, :] + cb_ref[...]
        act = conv * _sigmoid(conv)
        xt_scr[c] = act[:, :SSM_D_INNER].T
        bc_scr[pl.ds(r0, q), :] = act[:, SSM_D_INNER:].astype(BF16)

        raw = dt_ref[pl.ds(r0, q), :]
        rep = jnp.where(lane < 16, raw, jnp.where(lane < 32, pltpu.roll(raw, 16, 1), pltpu.roll(raw, 32, 1)))
        v = rep + bias_ref[...]
        sp = jnp.maximum(v, 0.0) + jnp.log1p(jnp.exp(-jnp.abs(v)))
        t = sp * amul
        hi = t.astype(BF16)
        r1 = t - hi.astype(F32)
        mid = r1.astype(BF16)
        lo = (r1 - mid.astype(F32)).astype(BF16)
        cum = _dot(tri, hi) + _dot(tri, mid) + _dot(tri, lo)
        tc = jnp.where(lane < 32, t, cum)
        tab[pl.ds(r0, q), :] = tc
        tabt[c] = tc.T
        return carry

    lax.fori_loop(0, nc, prep, 0, unroll=nc)

    def pass_fwd(c, carry):
        r0 = pl.multiple_of(c * q, q)
        pc = tab[pl.ds(r0, q), :]
        pt = tabt[c]
        for g in range(2):
            bg = bc_scr[pl.ds(r0, q), 128 * g:128 * (g + 1)]
            cg = bc_scr[pl.ds(r0, q), 256 + 128 * g:256 + 128 * (g + 1)]
            cb = _dot_nt(cg, bg)
            y_in = _dot_nt(s_scr[0, 256 * g:256 * (g + 1), :].astype(BF16), cg)
            for hh in range(4):
                h = 4 * g + hh
                acf_row = pt[32 + h:33 + h, :]
                eb_col = pc[:, 40 + h:41 + h] - pc[:, 24 + h:25 + h]
                eb_row = pt[40 + h:41 + h, :] - pt[24 + h:25 + h, :]
                dtf_row = pt[h:h + 1, :]
                dtb_row = pt[8 + h:9 + h, :]
                seg = jnp.where(below, pc[:, 32 + h:33 + h] - acf_row, jnp.where(above, eb_row - eb_col, 0.0))
                wgt = jnp.where(below, dtf_row, jnp.where(above, dtb_row, dtf_row + dtb_row))
                mp = cb * (jnp.exp(seg) * wgt)
                xth = xt_scr[c, 64 * h:64 * (h + 1), :]
                yd = _dot_nt(xth.astype(BF16), mp.astype(BF16))
                yo = y_in[64 * hh:64 * (hh + 1), :] * jnp.exp(acf_row)
                yt_scr[c, 64 * h:64 * (h + 1), :] = yd + yo + dskip_ref[h] * xth
                tot = pt[32 + h:33 + h, q - 1:q]
                wf_row = dtf_row * jnp.exp(tot - acf_row)
                st = _dot((xth * wf_row).astype(BF16), bg)
                s_scr[0, 64 * h:64 * (h + 1), :] = s_scr[0, 64 * h:64 * (h + 1), :] * jnp.exp(tot) + st
        return carry

    lax.fori_loop(0, nc, pass_fwd, 0, unroll=nc)

    def pass_bwd(ci, carry):
        c = nc - 1 - ci
        r0 = pl.multiple_of(c * q, q)
        pt = tabt[c]
        for g in range(2):
            bg = bc_scr[pl.ds(r0, q), 128 * g:128 * (g + 1)]
            cg = bc_scr[pl.ds(r0, q), 256 + 128 * g:256 + 128 * (g + 1)]
            y_in = _dot_nt(s_scr[1, 256 * g:256 * (g + 1), :].astype(BF16), cg)
            for hh in range(4):
                h = 4 * g + hh
                eb_row = pt[40 + h:41 + h, :] - pt[24 + h:25 + h, :]
                tot = pt[40 + h:41 + h, q - 1:q]
                xth = xt_scr[c, 64 * h:64 * (h + 1), :]
                yt_scr[c, 64 * h:64 * (h + 1), :] += y_in[64 * hh:64 * (hh + 1), :] * jnp.exp(tot - eb_row)
                wb_row = pt[8 + h:9 + h, :] * jnp.exp(eb_row)
                st = _dot((xth * wb_row).astype(BF16), bg)
                s_scr[1, 64 * h:64 * (h + 1), :] = s_scr[1, 64 * h:64 * (h + 1), :] * jnp.exp(tot) + st
        zc = z_ref[pl.ds(r0, q), :]
        yg = yt_scr[c].T * (zc * _sigmoid(zc))
        ms = jnp.mean(yg * yg, axis=-1, keepdims=True)
        o_ref[pl.ds(r0, q), :] = (yg * lax.rsqrt(ms + EPS) * ng_ref[...]).astype(o_ref.dtype)
        return carry

    lax.fori_loop(0, nc, pass_bwd, 0, unroll=nc)

    if emit_state and has_prev:
        sfin_ref[0] = sprev_ref[...]
        sfin_ref[1] = s_scr[...]
    elif emit_state:
        sfin_ref[...] = s_scr[...]


def _ssd(xbc, z, dt, conv_w, conv_b, bias128, alog128, dskip, norm_g, tri, h0, s_prev, nb, seq, emit_state):
    t = xbc.shape[0]
    has_h0 = h0 is not None
    has_prev = s_prev is not None
    const = lambda shape: pl.BlockSpec(shape, lambda b: (0,) * len(shape))
    in_specs = [
        pl.BlockSpec((seq, SSM_CONV_CH), lambda b: (b, 0)),
        pl.BlockSpec((seq, SSM_D_INNER), lambda b: (b, 0)),
        pl.BlockSpec((seq, 128), lambda b: (b, 0)),
        const((3, SSM_CONV_CH)), const((1, SSM_CONV_CH)), const((1, 128)), const((1, 128)),
        pl.BlockSpec(memory_space=pltpu.SMEM), const((1, SSM_D_INNER)), const((128, 128)),
    ]
    args = [xbc, z, dt, conv_w, conv_b, bias128, alog128, dskip, norm_g, tri]
    if has_h0:
        in_specs.append(pl.BlockSpec((None, 2, 512, 128), lambda b: (b, 0, 0, 0)))
        args.append(h0)
    if has_prev:
        in_specs.append(pl.BlockSpec((None, 2, 512, 128), lambda b: (b, 0, 0, 0)))
        args.append(s_prev)
    out_specs = [pl.BlockSpec((seq, SSM_D_INNER), lambda b: (b, 0))]
    out_shape = [jax.ShapeDtypeStruct((t, SSM_D_INNER), BF16)]
    if emit_state and has_prev:
        out_specs.append(pl.BlockSpec((None, 2, 2, 512, 128), lambda b: (b, 0, 0, 0, 0)))
        out_shape.append(jax.ShapeDtypeStruct((nb, 2, 2, 512, 128), F32))
    elif emit_state:
        out_specs.append(pl.BlockSpec((None, 2, 512, 128), lambda b: (b, 0, 0, 0)))
        out_shape.append(jax.ShapeDtypeStruct((nb, 2, 512, 128), F32))
    return pl.pallas_call(
        functools.partial(_ssd_kernel, seq=seq, has_h0=has_h0, emit_state=emit_state, has_prev=has_prev),
        grid=(nb,),
        in_specs=in_specs,
        out_specs=out_specs,
        out_shape=out_shape,
        scratch_shapes=[
            pltpu.VMEM((seq // SSM_CHUNK, SSM_D_INNER, SSM_CHUNK), F32),
            pltpu.VMEM((seq, 512), BF16),
            pltpu.VMEM((seq, 128), F32),
            pltpu.VMEM((seq // SSM_CHUNK, 128, SSM_CHUNK), F32),
            pltpu.VMEM((seq // SSM_CHUNK, SSM_D_INNER, SSM_CHUNK), F32),
            pltpu.VMEM((2, 512, 128), F32),
        ],
        compiler_params=_params("arbitrary"),
        name="ssd",
    )(*args)


def _merge_kernel(x_ref, h_ref, g_ref, b0, b1, b2, b3, wg0, wg1, wg2, wg3, wb_ref, wo_ref, o_ref):
    n = pl.program_id(1)

    @pl.when(n == 0)
    def _():
        o_ref[...] = x_ref[...]

    h = h_ref[...]
    merged = None
    for k, (b_ref, wg_ref) in enumerate(((b0, wg0), (b1, wg1), (b2, wg2), (b3, wg3))):
        gate = _sigmoid(_dot_nt(h, wg_ref[0]))
        term = gate * _dot(b_ref[...], wb_ref[k])
        merged = term if merged is None else merged + term
    o_ref[...] += g_ref[...] * _dot(merged.astype(BF16), wo_ref[...])


def _merge(x, h, mod3, branches, wt, gate_row0, w_branch, w_out, layer, row_fn):
    t = x.shape[0]
    tm, tn = 512, 512
    assert t % tm == 0
    nn = D_MODEL // tn
    rf = lambda m: row_fn(m * tm)
    gate_spec = lambda k: pl.BlockSpec(
        (pl.Element(1), pl.Element(tn), pl.Element(D_MODEL)),
        lambda m, n: (0, pl.multiple_of(gate_row0 + k * D_MODEL + n * tn, 16), 0))
    return pl.pallas_call(
        _merge_kernel,
        grid=(t // tm, nn),
        in_specs=[
            pl.BlockSpec((tm, D_MODEL), lambda m, n: (m, 0)),
            pl.BlockSpec((tm, D_MODEL), lambda m, n: (m, 0)),
            _mod_spec(2, rf),
        ] + [pl.BlockSpec((tm, MIX_W), lambda m, n: (m, 0))] * 4
          + [gate_spec(k) for k in range(4)] + [
            pl.BlockSpec((None, N_BRANCH, MIX_W, tn), lambda m, n: (layer, 0, 0, n)),
            pl.BlockSpec((None, tn, D_MODEL), lambda m, n: (layer, n, 0)),
        ],
        out_specs=pl.BlockSpec((tm, D_MODEL), lambda m, n: (m, 0)),
        out_shape=jax.ShapeDtypeStruct((t, D_MODEL), F32),
        compiler_params=_params("arbitrary", "arbitrary"),
        name="merge",
    )(x, h, mod3, *branches, wt, wt, wt, wt, w_branch, w_out)


def _ffn_kernel(x_ref, sc_ref, sh_ref, g_ref, w1_ref, w2_ref, o_ref, h_scr):
    n = pl.program_id(1)

    @pl.when(n == 0)
    def _():
        _rms_mod_rows(x_ref, sc_ref, sh_ref, h_scr)
        o_ref[...] = x_ref[...]

    a = jnp.maximum(_dot(h_scr[...], w1_ref[...].astype(BF16)), 0.0)
    o_ref[...] += g_ref[...] * _dot((a * a).astype(BF16), w2_ref[...].astype(BF16))


def _ffn(x, mod3, w1, w2, layer, row_fn):
    t = x.shape[0]
    tm, tf = 1024, 512
    assert t % tm == 0
    rf = lambda m: row_fn(m * tm)
    return pl.pallas_call(
        _ffn_kernel,
        grid=(t // tm, D_FF // tf),
        in_specs=[
            pl.BlockSpec((tm, D_MODEL), lambda m, n: (m, 0)),
            _mod_spec(4, rf), _mod_spec(3, rf), _mod_spec(5, rf),
            pl.BlockSpec((None, D_MODEL, tf), lambda m, n: (layer, 0, n)),
            pl.BlockSpec((None, tf, D_MODEL), lambda m, n: (layer, n, 0)),
        ],
        out_specs=pl.BlockSpec((tm, D_MODEL), lambda m, n: (m, 0)),
        out_shape=jax.ShapeDtypeStruct((t, D_MODEL), F32),
        scratch_shapes=[pltpu.VMEM((tm, D_MODEL), BF16)],
        compiler_params=_params("arbitrary", "arbitrary"),
        name="ffn",
    )(x, mod3, mod3, mod3, w1, w2)


def _rope_tables(seq):
    half = 16
    freqs = ROPE_THETA ** (-jnp.arange(half, dtype=F32) / half)
    n_rows = seq // GRID_W
    rows = jnp.repeat(jnp.arange(n_rows), GRID_W).astype(F32)
    cols = jnp.tile(jnp.arange(GRID_W), n_rows).astype(F32)
    ar = rows[:, None] * freqs[None, :]
    ac = cols[:, None] * freqs[None, :]
    cos64 = jnp.concatenate([jnp.cos(ar), jnp.cos(ar), jnp.cos(ac), jnp.cos(ac)], axis=-1)
    sin64 = jnp.concatenate([-jnp.sin(ar), jnp.sin(ar), -jnp.sin(ac), jnp.sin(ac)], axis=-1)
    return (jnp.tile(cos64, (1, ATT_HEADS)), jnp.tile(sin64, (1, ATT_HEADS)),
            jnp.tile(cos64, (1, ATT_KV_HEADS)), jnp.tile(sin64, (1, ATT_KV_HEADS)))


def _layer_path(x, mod3, lw, row_fn, nb, seq, lat, s_prev=None, side_casts=()):
    (wt, qg, kg, gsum, conv_w, conv_b, bias128, alog128, dskip, ssm_g, tri, gmlp_g, ws, bs_t,
     w_ff1, w_ff2, layer) = lw[:-1]
    bf = lw[-1]
    h, qkv, f_in, xbc, z, dt, o_gmlp, *side = _inproj(x, mod3, wt, gmlp_g, ws, bs_t, row_fn, side_casts)
    if bf is None:
        bf = (side[1], 0, side[2].reshape(DEPTH, N_BRANCH, MIX_W, D_MODEL), side[3].reshape(DEPTH, D_MODEL, D_MODEL))
    wg, gate_row0, w_branch, w_out = bf
    if lat is None:
        o_att, k_new, v_new = _attention_ctx(qkv, qg, kg, gsum, nb, seq)
        h0 = None
    else:
        rope_tabs, cache_k, cache_v, h0 = lat
        o_att = _attention_lat(qkv, qg, kg, gsum, rope_tabs, cache_k, cache_v, nb, seq)
        k_new = v_new = None
    o_fnet = _fourier(f_in, nb, seq)
    ssd_out = _ssd(xbc, z, dt, conv_w, conv_b, bias128, alog128, dskip, ssm_g, tri, h0, s_prev, nb, seq,
                   emit_state=lat is None)
    o_ssm = ssd_out[0]
    s_fin = ssd_out[1] if lat is None else None
    x = _merge(x, h, mod3, (o_att, o_fnet, o_ssm, o_gmlp), wg, gate_row0, w_branch, w_out, layer, row_fn)
    x = _ffn(x, mod3, w_ff1, w_ff2, layer, row_fn)
    return x, k_new, v_new, s_fin, side, bf


def kernel(x_prompt, x_sample, c, cache_k, cache_v, state_ssm, c_ctx, w_mod, b_mod, w_in, q_norm_g, k_norm_g,
           conv_w, conv_b, a_log, dt_bias, d_skip, ssm_norm_g, gmlp_norm_g, w_spatial, b_spatial, w_branch,
           w_out, w_ff1, w_ff2):
    nb_ctx, seq_ctx, _ = x_prompt.shape
    nb_lat, seq_lat, _ = x_sample.shape
    assert seq_lat == LAT_LEN and 1 + nb_lat <= 8

    cond8 = jnp.zeros((8, D_MODEL), F32).at[0].set(c_ctx).at[1:1 + nb_lat].set(c)
    mod = _modulation(cond8, w_mod, b_mod)

    gsum = jnp.asarray(np.kron(np.eye(ATT_HEADS), np.ones((HEAD_DIM, HEAD_DIM))), BF16)
    tri = jnp.asarray(np.tril(np.ones((SSM_CHUNK, SSM_CHUNK))), BF16)
    rope_tabs = _rope_tables(seq_lat)

    xp = x_prompt.reshape(nb_ctx * seq_ctx, D_MODEL)
    xs = x_sample.reshape(nb_lat * seq_lat, D_MODEL)
    ctx_row = lambda tok: 0
    lat_row = lambda tok: 1 + tok // LAT_LEN

    w_in_t = jnp.swapaxes(w_in, 1, 2)
    wt = _cast_layer_bf16(w_in_t, 0, GATE_COL0)
    bf = None

    assert DEPTH == 2
    new_k, new_v, s_l = [], [], None
    for l in range(DEPTH):
        zeros80 = jnp.zeros((128 - 48,), F32)
        bias128 = jnp.concatenate([jnp.tile(dt_bias[l].reshape(16), 3), zeros80]).reshape(1, 128)
        alog128 = jnp.concatenate([jnp.zeros((16,), F32), jnp.tile(a_log[l].reshape(16), 2), zeros80]).reshape(1, 128)
        lw = (
            wt,
            jnp.tile(q_norm_g[l], ATT_HEADS).reshape(1, 512), jnp.tile(k_norm_g[l], ATT_KV_HEADS).reshape(1, 128),
            gsum, conv_w[l], conv_b[l].reshape(1, SSM_CONV_CH), bias128, alog128,
            d_skip[l], ssm_norm_g[l].reshape(1, SSM_D_INNER), tri,
            gmlp_norm_g[l].reshape(1, MIX_W), w_spatial[l].astype(BF16), b_spatial[l].T,
            w_ff1, w_ff2, l, bf,
        )
        mod3 = mod[l].reshape(8, 1, 6 * D_MODEL)
        side_casts = []
        if l == 0:
            side_casts = [(w_in_t, 1, 0, N_IN), (w_in_t, 0, GATE_COL0, N_BRANCH * D_MODEL),
                          (w_branch.reshape(-1, D_MODEL), None, 0, DEPTH * N_BRANCH * MIX_W),
                          (w_out.reshape(-1, D_MODEL), None, 0, DEPTH * D_MODEL)]
        xp, k_l, v_l, s_l, side, bf = _layer_path(xp, mod3, lw, ctx_row, nb_ctx, seq_ctx, None,
                                                  s_prev=s_l, side_casts=side_casts)
        lw = lw[:-1] + (bf,)
        new_k.append(k_l.reshape(nb_ctx, seq_ctx, ATT_KV_HEADS, HEAD_DIM))
        new_v.append(v_l.reshape(nb_ctx, seq_ctx, ATT_KV_HEADS, HEAD_DIM))
        lat = (rope_tabs,
               cache_k[:, l].reshape(nb_lat, -1, ATT_KV_HEADS * HEAD_DIM),
               cache_v[:, l].reshape(nb_lat, -1, ATT_KV_HEADS * HEAD_DIM),
               state_ssm[:, l].reshape(nb_lat, 2, SSM_HEADS * SSM_HEAD_DIM, SSM_STATE))
        xs = _layer_path(xs, mod3, lw, lat_row, nb_lat, seq_lat, lat)[0]
        if l == 0:
            wt = side[0]
            bf = (wt, GATE_COL0) + bf[2:]

    return (xp.reshape(nb_ctx, seq_ctx, D_MODEL), xs.reshape(nb_lat, seq_lat, D_MODEL),
            jnp.stack(new_k, axis=1), jnp.stack(new_v, axis=1),
            s_l.reshape(nb_ctx, DEPTH, 2, SSM_HEADS, SSM_HEAD_DIM, SSM_STATE))
```

```python
import functools
import math

import numpy as np
import jax
import jax.numpy as jnp
from jax import lax
from jax.experimental import pallas as pl
from jax.experimental.pallas import tpu as pltpu

F32 = jnp.float32
BF16 = jnp.bfloat16

D_MODEL = 2048
DEPTH = 2
GRID_W = 64
MIX_W = 512
N_BRANCH = 4
ATT_HEADS = 8
ATT_KV_HEADS = 2
HEAD_DIM = 64
ROPE_THETA = 10000.0
FNET_GW = 128
SSM_HEADS = 8
SSM_HEAD_DIM = 64
SSM_STATE = 128
SSM_CHUNK = 128
SSM_D_INNER = 512
SSM_CONV_CH = 1024
GMLP_CHUNK = 128
D_FF = 4 * D_MODEL
EPS = 1e-6
LAT_LEN = 1024

QKV_W = 768
COL_F = 768
COL_XBC = 1280
COL_Z = 2304
COL_DT = 2816
COL_UV = 2832
GATE_COL0 = 3856
N_IN = GATE_COL0 + N_BRANCH * D_MODEL
_IN_PIECES = ((0, QKV_W), (COL_F, COL_XBC), (COL_XBC, COL_Z), (COL_Z, COL_DT),
              (COL_DT, COL_DT + 128),
              (COL_UV, GATE_COL0))

VMEM_LIMIT = 56 * 1024 * 1024
NEG_BIG = -1e30


def _params(*sem):
    return pltpu.CompilerParams(dimension_semantics=sem, vmem_limit_bytes=VMEM_LIMIT)


def _dot(a, b):
    return jnp.dot(a, b, preferred_element_type=F32)


def _dot_nt(a, b):
    return lax.dot_general(a, b, (((1,), (1,)), ((), ())), preferred_element_type=F32)


def _dot_tn(a, b):
    return lax.dot_general(a, b, (((0,), (0,)), ((), ())), preferred_element_type=F32)


def _sigmoid(x):
    return 1.0 / (1.0 + jnp.exp(-x))


def _split2(x):
    hi = x.astype(BF16)
    lo = (x - hi.astype(F32)).astype(BF16)
    return hi, lo


def _rms_mod(x, sc, sh):
    ms = jnp.mean(x * x, axis=-1, keepdims=True)
    return x * lax.rsqrt(ms + EPS) * (1.0 + sc) + sh


def _rms_mod_rows(x_ref, sc_ref, sh_ref, dst_ref):
    slab = 16

    def body(i, carry):
        rows = pl.ds(pl.multiple_of(i * slab, slab), slab)
        dst_ref[rows, :] = _rms_mod(x_ref[rows, :], sc_ref[...], sh_ref[...]).astype(dst_ref.dtype)
        return carry

    lax.fori_loop(0, x_ref.shape[0] // slab, body, 0, unroll=16)


def _mod_kernel(cond_ref, w_ref, b_ref, o_ref):
    c = cond_ref[...]
    s = c * _sigmoid(c)
    o_ref[...] = _dot(s.astype(BF16), w_ref[...].astype(BF16)) + b_ref[...]


def _modulation(cond8, w_mod, b_mod3, layer):
    tn = 1024
    n6 = 6 * D_MODEL
    return pl.pallas_call(
        _mod_kernel,
        grid=(n6 // tn,),
        in_specs=[
            pl.BlockSpec((8, D_MODEL), lambda n: (0, 0)),
            pl.BlockSpec((None, D_MODEL, tn), lambda n: (layer, 0, n)),
            pl.BlockSpec((None, 1, tn), lambda n: (layer, 0, n)),
        ],
        out_specs=pl.BlockSpec((8, tn), lambda n: (0, n)),
        out_shape=jax.ShapeDtypeStruct((8, n6), F32),
        compiler_params=_params("arbitrary"),
        name="modulation",
    )(cond8, w_mod, b_mod3)


def _mod_spec(chunk, row_fn):
    return pl.BlockSpec((None, 1, D_MODEL), lambda m, *_: (row_fn(m), 0, chunk))


def _cast_kernel(w_ref, o_ref):
    o_ref[...] = w_ref[...].astype(o_ref.dtype)


def _cast_layer_bf16(w, layer, n_rows):
    rows = 1024
    return pl.pallas_call(
        _cast_kernel,
        grid=(pl.cdiv(n_rows, rows),),
        in_specs=[pl.BlockSpec((None, rows, D_MODEL), lambda r: (layer, r, 0))],
        out_specs=pl.BlockSpec((None, rows, D_MODEL), lambda r: (0, r, 0)),
        out_shape=jax.ShapeDtypeStruct((1, n_rows, D_MODEL), BF16),
        compiler_params=_params("arbitrary"),
        name="cast_layer_bf16",
    )(w)


def _gmlp_chunk(x, gain, ws_ref, bs_ref):
    c0 = math.sqrt(2.0 / math.pi)
    ge = x * (0.5 * (1.0 + jnp.tanh(c0 * (x + 0.044715 * (x * x * x)))))
    u = ge[:, :MIX_W]
    v = ge[:, MIX_W:]
    v = v * lax.rsqrt(jnp.mean(v * v, axis=-1, keepdims=True) + EPS) * gain
    outs = []
    for g in range(MIX_W // 128):
        gl = slice(g * 128, (g + 1) * 128)
        mixed = _dot(ws_ref[g], v[:, gl].astype(BF16)) + bs_ref[:, g:g + 1]
        outs.append(u[:, gl] * mixed)
    return jnp.concatenate(outs, axis=1)


def _inproj_kernel(*refs, n_side):
    x_ref, sc_ref, sh_ref, w_ref, gg_ref, ws_ref, bs_ref = refs[:7]
    side_in = refs[7:7 + n_side]
    h_ref = refs[7 + n_side]
    out_refs = refs[8 + n_side:len(refs) - n_side]
    side_out = refs[len(refs) - n_side:]
    for src, dst in zip(side_in, side_out):
        dst[...] = src[...].astype(dst.dtype)
    hb = _rms_mod(x_ref[...], sc_ref[...], sh_ref[...]).astype(BF16)
    h_ref[...] = hb
    o_gmlp = out_refs[-1]
    a, b = _IN_PIECES[-1]
    uv = _dot_nt(hb, w_ref[a:b, :])
    for c in range(x_ref.shape[0] // GMLP_CHUNK):
        sl = slice(c * GMLP_CHUNK, (c + 1) * GMLP_CHUNK)
        o_gmlp[sl, :] = _gmlp_chunk(uv[sl, :], gg_ref[...], ws_ref, bs_ref).astype(o_gmlp.dtype)
    for o_ref, (a, b) in zip(out_refs[:-1], _IN_PIECES[:-1]):
        o_ref[...] = _dot_nt(hb, w_ref[a:b, :])


def _inproj(x, mod3, wt, gmlp_g, ws, bs_t, row_fn, side_casts=()):
    t = x.shape[0]
    tm = 256
    assert t % tm == 0
    steps = t // tm
    widths = [b - a for a, b in _IN_PIECES[:-1]]
    const = lambda shape: pl.BlockSpec(shape, lambda m: (0,) * len(shape))
    in_specs = [
        pl.BlockSpec((tm, D_MODEL), lambda m: (m, 0)),
        _mod_spec(1, lambda m: row_fn(m * tm)),
        _mod_spec(0, lambda m: row_fn(m * tm)),
        pl.BlockSpec((None, GATE_COL0, D_MODEL), lambda m: (0, 0, 0), pipeline_mode=pl.Buffered(1)),
        const((1, MIX_W)), const((4, 128, 128)), const((128, 4)),
    ]
    args = [x, mod3, mod3, wt, gmlp_g, ws, bs_t]
    out_specs = ([pl.BlockSpec((tm, D_MODEL), lambda m: (m, 0))]
                 + [pl.BlockSpec((tm, w), lambda m: (m, 0)) for w in widths]
                 + [pl.BlockSpec((tm, MIX_W), lambda m: (m, 0))])
    out_shape = ([jax.ShapeDtypeStruct((t, D_MODEL), BF16)]
                 + [jax.ShapeDtypeStruct((t, w), F32) for w in widths]
                 + [jax.ShapeDtypeStruct((t, MIX_W), BF16)])
    side_in_specs, side_args = [], []
    for w_src, lead, row0, n_rows in side_casts:
        rows = -(-n_rows // (16 * steps)) * 16
        assert pl.cdiv(n_rows, rows) == steps
        if lead is None:
            side_in_specs.append(pl.BlockSpec((rows, D_MODEL), lambda m: (m, 0)))
            out_specs.append(pl.BlockSpec((rows, D_MODEL), lambda m: (m, 0)))
            out_shape.append(jax.ShapeDtypeStruct((n_rows, D_MODEL), BF16))
        elif row0 == 0:
            side_in_specs.append(pl.BlockSpec((None, rows, D_MODEL), lambda m, lead=lead: (lead, m, 0)))
            out_specs.append(pl.BlockSpec((None, rows, D_MODEL), lambda m: (0, m, 0)))
            out_shape.append(jax.ShapeDtypeStruct((1, n_rows, D_MODEL), BF16))
        else:
            assert n_rows == rows * steps and row0 % 16 == 0
            side_in_specs.append(pl.BlockSpec(
                (pl.Element(1), pl.Element(rows), pl.Element(D_MODEL)),
                lambda m, lead=lead, row0=row0, rows=rows: (lead, pl.multiple_of(row0 + m * rows, 16), 0)))
            out_specs.append(pl.BlockSpec((1, rows, D_MODEL), lambda m: (0, m, 0)))
            out_shape.append(jax.ShapeDtypeStruct((1, n_rows, D_MODEL), BF16))
        side_args.append(w_src)
    in_specs += side_in_specs
    args += side_args
    return pl.pallas_call(
        functools.partial(_inproj_kernel, n_side=len(side_casts)),
        grid=(steps,),
        in_specs=in_specs,
        out_specs=out_specs,
        out_shape=out_shape,
        compiler_params=_params("arbitrary"),
        name="inproj",
    )(*args)


def _head_rms(t, gsum, gain):
    t2 = t * t
    hi, lo = _split2(t2)
    ss = _dot(hi, gsum) + _dot(lo, gsum)
    return t * lax.rsqrt(ss * (1.0 / HEAD_DIM) + EPS) * gain


def _rope(t, cos, sin_signed):
    w = t.shape[-1]
    up = pltpu.roll(t, w - 16, 1)
    dn = pltpu.roll(t, 16, 1)
    lane = lax.broadcasted_iota(jnp.int32, t.shape, 1)
    partner = jnp.where((lane & 31) < 16, up, dn)
    return t * cos + partner * sin_signed


ATT_GROUP = ATT_HEADS // ATT_KV_HEADS
GROUP_W = ATT_GROUP * HEAD_DIM


def _store_group_keys(k, kbd_scr, row0):
    s = k.shape[0]
    k2 = jnp.concatenate([k, k], axis=1)
    k2r = pltpu.roll(k2, HEAD_DIM, 1)
    lane = lax.broadcasted_iota(jnp.int32, k2.shape, 1)
    first = (lane & (2 * HEAD_DIM - 1)) < HEAD_DIM
    for j, tiled in enumerate((jnp.where(first, k2, k2r), jnp.where(first, k2r, k2))):
        for hh in range(ATT_GROUP):
            slab = jnp.where(lax.shift_right_logical(lane, 6) == hh, tiled, 0.0)
            kbd_scr[j, hh, row0:row0 + s, :] = slab.astype(kbd_scr.dtype)


def _attend_heads(q, kbd_scr, vt_scr, o_ref):
    qb = (q * (HEAD_DIM ** -0.5 * math.log2(math.e))).astype(BF16)
    s_len = kbd_scr.shape[2]
    for j in range(ATT_KV_HEADS):
        kbd = kbd_scr[j].reshape(ATT_GROUP * s_len, GROUP_W)
        s_all = _dot_nt(kbd, qb[:, j * GROUP_W:(j + 1) * GROUP_W])
        outs = []
        for hh in range(ATT_GROUP):
            s = s_all[hh * s_len:(hh + 1) * s_len, :]
            m = jnp.max(s, axis=0, keepdims=True)
            e = jnp.exp2(s - m)
            den = jnp.sum(e, axis=0, keepdims=True)
            outs.append(_dot(vt_scr[j * HEAD_DIM:(j + 1) * HEAD_DIM, :], e.astype(BF16)) / den)
        o_ref[:, j * GROUP_W:(j + 1) * GROUP_W] = jnp.concatenate(outs, axis=0).T.astype(o_ref.dtype)


def _attn_ctx_kernel(q_ref, kv_ref, qg_ref, kg_ref, gsum_ref, o_ref, kout_ref, vout_ref, kbd_scr, vt_scr):
    gsum = gsum_ref[...]
    seq = vt_scr.shape[-1]
    for i in range(vt_scr.shape[0]):
        rows = pl.ds(i * seq, seq)
        kv = kv_ref[rows, :]
        k = _head_rms(kv[:, :128], gsum[:128, :128], kg_ref[...])
        v = kv[:, 128:]
        kout_ref[rows, :] = k
        vout_ref[rows, :] = v
        _store_group_keys(k, kbd_scr.at[i], 0)
        vt_scr[i] = v.T.astype(BF16)
        q = _head_rms(q_ref[rows, :], gsum, qg_ref[...])
        _attend_heads(q, kbd_scr.at[i], vt_scr.at[i], o_ref.at[rows])


def _attn_lat_kernel(q_ref, kv_ref, qg_ref, kg_ref, gsum_ref, cq_ref, sq_ref, ck_ref, sk_ref,
                     cache_k_ref, cache_v_ref, o_ref, kbd_scr, vt_scr):
    gsum = gsum_ref[...]
    past = cache_k_ref.shape[0]

    @pl.when(pl.program_id(1) == 0)
    def _():
        kv = kv_ref[...]
        k = _head_rms(kv[:, :128], gsum[:128, :128], kg_ref[...])
        k = _rope(k, ck_ref[...], sk_ref[...])
        _store_group_keys(cache_k_ref[...], kbd_scr, 0)
        _store_group_keys(k, kbd_scr, past)
        vt_scr[:, :past] = cache_v_ref[...].T.astype(BF16)
        vt_scr[:, past:] = kv[:, 128:].T.astype(BF16)

    q = _head_rms(q_ref[...], gsum, qg_ref[...])
    q = _rope(q, cq_ref[...], sq_ref[...])
    _attend_heads(q, kbd_scr, vt_scr, o_ref)


def _attention_ctx(qkv, qg, kg, gsum, nb, seq):
    t = qkv.shape[0]
    per = 4 if nb % 4 == 0 else 1
    rows = per * seq
    const = lambda shape: pl.BlockSpec(shape, lambda b: (0, 0))
    return pl.pallas_call(
        _attn_ctx_kernel,
        grid=(nb // per,),
        in_specs=[
            pl.BlockSpec((rows, 512), lambda b: (b, 0)),
            pl.BlockSpec((rows, 256), lambda b: (b, 2)),
            const((1, 512)), const((1, 128)), const((512, 512)),
        ],
        out_specs=[
            pl.BlockSpec((rows, 512), lambda b: (b, 0)),
            pl.BlockSpec((rows, 128), lambda b: (b, 0)),
            pl.BlockSpec((rows, 128), lambda b: (b, 0)),
        ],
        out_shape=[
            jax.ShapeDtypeStruct((t, 512), BF16),
            jax.ShapeDtypeStruct((t, 128), F32),
            jax.ShapeDtypeStruct((t, 128), F32),
        ],
        scratch_shapes=[pltpu.VMEM((per, ATT_KV_HEADS, ATT_GROUP, seq, GROUP_W), BF16),
                        pltpu.VMEM((per, 128, seq), BF16)],
        compiler_params=_params("arbitrary"),
        name="attn_ctx",
    )(qkv, qkv, qg, kg, gsum)


def _attention_lat(qkv, qg, kg, gsum, rope_tabs, cache_k, cache_v, nb, seq):
    t = qkv.shape[0]
    qblk = 256
    nq = seq // qblk
    past = cache_k.shape[1]
    cq, sq, ck, sk = rope_tabs
    const = lambda shape: pl.BlockSpec(shape, lambda b, i: (0, 0))
    return pl.pallas_call(
        _attn_lat_kernel,
        grid=(nb, nq),
        in_specs=[
            pl.BlockSpec((qblk, 512), lambda b, i: (b * nq + i, 0)),
            pl.BlockSpec((seq, 256), lambda b, i: (b, 2)),
            const((1, 512)), const((1, 128)), const((512, 512)),
            pl.BlockSpec((qblk, 512), lambda b, i: (i, 0)),
            pl.BlockSpec((qblk, 512), lambda b, i: (i, 0)),
            const((seq, 128)), const((seq, 128)),
            pl.BlockSpec((None, past, 128), lambda b, i: (b, 0, 0)),
            pl.BlockSpec((None, past, 128), lambda b, i: (b, 0, 0)),
        ],
        out_specs=pl.BlockSpec((qblk, 512), lambda b, i: (b * nq + i, 0)),
        out_shape=jax.ShapeDtypeStruct((t, 512), BF16),
        scratch_shapes=[pltpu.VMEM((ATT_KV_HEADS, ATT_GROUP, past + seq, GROUP_W), BF16),
                        pltpu.VMEM((128, past + seq), BF16)],
        compiler_params=_params("arbitrary", "arbitrary"),
        name="attn_lat",
    )(qkv, qkv, qg, kg, gsum, cq, sq, ck, sk, cache_k, cache_v)


def _fourier_kernel(f_ref, bdh_ref, bdl_ref, cl_ref, sl_ref, o_ref, *, scale):
    bdh = bdh_ref[...]
    bdl = bdl_ref[...]
    cl = cl_ref[...].astype(BF16)
    sl = sl_ref[...].astype(BF16)
    seq = cl_ref.shape[0]
    for i in range(f_ref.shape[0] // seq):
        rows = pl.ds(i * seq, seq)
        xh, xl = _split2(f_ref[rows, :])
        yc, ys = [], []
        for g in range(MIX_W // FNET_GW):
            gl = slice(g * FNET_GW, (g + 1) * FNET_GW)
            y = _dot(xh[:, gl], bdh) + _dot(xl[:, gl], bdh) + _dot(xh[:, gl], bdl)
            yc.append(y[:, :FNET_GW])
            ys.append(y[:, FNET_GW:])
        re = _dot(cl, jnp.concatenate(yc, axis=1).astype(BF16))
        im = _dot(sl, jnp.concatenate(ys, axis=1).astype(BF16))
        o_ref[rows, :] = ((re - im) * scale).astype(o_ref.dtype)


def _dft_consts(seq):
    def cs(n):
        k = np.arange(n, dtype=np.int64)
        ang = 2.0 * np.pi * ((k[:, None] * k[None, :]) % n).astype(np.float64) / n
        return np.cos(ang), np.sin(ang)

    bd = np.concatenate(cs(FNET_GW), axis=1)
    cl, sl_ = cs(seq)

    def hl(a):
        a32 = jnp.asarray(a, F32)
        hi = a32.astype(BF16)
        return hi, (a32 - hi.astype(F32)).astype(BF16)

    return hl(bd) + (jnp.asarray(cl, F32), jnp.asarray(sl_, F32))


def _fourier(f, nb, seq):
    t = f.shape[0]
    consts = _dft_consts(seq)
    const = lambda a: pl.BlockSpec(a.shape, lambda b: (0, 0))
    per = 4 if (seq <= 256 and nb % 4 == 0) else 1
    return pl.pallas_call(
        functools.partial(_fourier_kernel, scale=1.0 / math.sqrt(seq * FNET_GW)),
        grid=(nb // per,),
        in_specs=[pl.BlockSpec((per * seq, MIX_W), lambda b: (b, 0))] + [const(a) for a in consts],
        out_specs=pl.BlockSpec((per * seq, MIX_W), lambda b: (b, 0)),
        out_shape=jax.ShapeDtypeStruct((t, MIX_W), BF16),
        compiler_params=_params("arbitrary"),
        name="fourier",
    )(f, *consts)


def _ssd_kernel(*refs, seq, has_h0, emit_state, has_prev, side_mod):
    (xbc_ref, z_ref, dt_ref, cw_ref, cb_ref, bias_ref, alog_ref, dskip_ref, ng_ref, tri_ref) = refs[:10]
    rest = list(refs[10:])
    h0_ref = rest.pop(0) if has_h0 else None
    sprev_ref = rest.pop(0) if has_prev else None
    mod_in = [rest.pop(0) for _ in range(3)] if side_mod else None
    o_ref = rest.pop(0)
    sfin_ref = rest.pop(0) if emit_state else None
    if side_mod:
        cond_ref, wmod_ref, bmod_ref = mod_in
        _mod_kernel(cond_ref, wmod_ref, bmod_ref, rest.pop(0))
    xt_scr, bc_scr, tab, tabt, yt_scr, s_scr = rest
    nc = seq // SSM_CHUNK
    q = SSM_CHUNK

    lane = lax.broadcasted_iota(jnp.int32, (q, 128), 1)
    ii = lax.broadcasted_iota(jnp.int32, (q, q), 0)
    jj = lax.broadcasted_iota(jnp.int32, (q, q), 1)
    below = ii > jj
    above = ii < jj
    row = lax.broadcasted_iota(jnp.int32, (q, 1), 0)
    cw = cw_ref[...]
    amul = jnp.where(lane < 16, 1.0, jnp.where(lane < 48, -jnp.exp(alog_ref[...]), 0.0))
    tri = tri_ref[...]

    if has_h0:
        s_scr[...] = h0_ref[...]
    else:
        s_scr[...] = jnp.zeros_like(s_scr)

    def prep(c, carry):
        r0 = pl.multiple_of(c * q, q)
        cur = xbc_ref[pl.ds(r0, q), :]
        prev8 = xbc_ref[pl.ds(pl.multiple_of(jnp.maximum(r0 - 8, 0), 8), 8), :]
        next8 = xbc_ref[pl.ds(pl.multiple_of(jnp.minimum(r0 + q, seq - 8), 8), 8), :]
        prev_row = jnp.where(c > 0, prev8[7:8, :], 0.0)
        next_row = jnp.where(c < nc - 1, next8[0:1, :], 0.0)
        up = jnp.where(row == 0, prev_row, pltpu.roll(cur, 1, 0))
        dn = jnp.where(row == q - 1, next_row, pltpu.roll(cur, q - 1, 0))
        conv = up * cw[0:1, :] + cur * cw[1:2, :] + dn * cw[2:3, :] + cb_ref[...]
        act = conv * _sigmoid(conv)
        xt_scr[c] = act[:, :SSM_D_INNER].T
        bc_scr[pl.ds(r0, q), :] = act[:, SSM_D_INNER:].astype(BF16)

        raw = dt_ref[pl.ds(r0, q), :]
        rep = jnp.where(lane < 16, raw, jnp.where(lane < 32, pltpu.roll(raw, 16, 1), pltpu.roll(raw, 32, 1)))
        v = rep + bias_ref[...]
        sp = jnp.maximum(v, 0.0) + jnp.log1p(jnp.exp(-jnp.abs(v)))
        t = sp * amul
        hi = t.astype(BF16)
        r1 = t - hi.astype(F32)
        mid = r1.astype(BF16)
        lo = (r1 - mid.astype(F32)).astype(BF16)
        cum = _dot(tri, hi) + _dot(tri, mid) + _dot(tri, lo)
        tc = jnp.where(lane < 32, t, cum)
        tab[pl.ds(r0, q), :] = tc
        tabt[c] = tc.T
        return carry

    lax.fori_loop(0, nc, prep, 0, unroll=nc)

    def pass_fwd(c, carry):
        r0 = pl.multiple_of(c * q, q)
        pc = tab[pl.ds(r0, q), :]
        pt = tabt[c]
        for g in range(2):
            bg = bc_scr[pl.ds(r0, q), 128 * g:128 * (g + 1)]
            cg = bc_scr[pl.ds(r0, q), 256 + 128 * g:256 + 128 * (g + 1)]
            cb = _dot_nt(cg, bg)
            y_in = _dot_nt(s_scr[0, 256 * g:256 * (g + 1), :].astype(BF16), cg)
            for hh in range(4):
                h = 4 * g + hh
                acf_row = pt[32 + h:33 + h, :]
                eb_col = pc[:, 40 + h:41 + h] - pc[:, 24 + h:25 + h]
                eb_row = pt[40 + h:41 + h, :] - pt[24 + h:25 + h, :]
                dtf_row = pt[h:h + 1, :]
                dtb_row = pt[8 + h:9 + h, :]
                seg = jnp.where(below, pc[:, 32 + h:33 + h] - acf_row, jnp.where(above, eb_row - eb_col, 0.0))
                wgt = jnp.where(below, dtf_row, jnp.where(above, dtb_row, dtf_row + dtb_row))
                mp = cb * (jnp.exp(seg) * wgt)
                xth = xt_scr[c, 64 * h:64 * (h + 1), :]
                yd = _dot_nt(xth.astype(BF16), mp.astype(BF16))
                yo = y_in[64 * hh:64 * (hh + 1), :] * jnp.exp(acf_row)
                yt_scr[c, 64 * h:64 * (h + 1), :] = yd + yo + dskip_ref[h] * xth
                tot = pt[32 + h:33 + h, q - 1:q]
                wf_row = dtf_row * jnp.exp(tot - acf_row)
                st = _dot((xth * wf_row).astype(BF16), bg)
                s_scr[0, 64 * h:64 * (h + 1), :] = s_scr[0, 64 * h:64 * (h + 1), :] * jnp.exp(tot) + st
        return carry

    lax.fori_loop(0, nc, pass_fwd, 0, unroll=nc)

    def pass_bwd(ci, carry):
        c = nc - 1 - ci
        r0 = pl.multiple_of(c * q, q)
        pt = tabt[c]
        for g in range(2):
            bg = bc_scr[pl.ds(r0, q), 128 * g:128 * (g + 1)]
            cg = bc_scr[pl.ds(r0, q), 256 + 128 * g:256 + 128 * (g + 1)]
            y_in = _dot_nt(s_scr[1, 256 * g:256 * (g + 1), :].astype(BF16), cg)
            for hh in range(4):
                h = 4 * g + hh
                eb_row = pt[40 + h:41 + h, :] - pt[24 + h:25 + h, :]
                tot = pt[40 + h:41 + h, q - 1:q]
                xth = xt_scr[c, 64 * h:64 * (h + 1), :]
                yt_scr[c, 64 * h:64 * (h + 1), :] += y_in[64 * hh:64 * (hh + 1), :] * jnp.exp(tot - eb_row)
                wb_row = pt[8 + h:9 + h, :] * jnp.exp(eb_row)
                st = _dot((xth * wb_row).astype(BF16), bg)
                s_scr[1, 64 * h:64 * (h + 1), :] = s_scr[1, 64 * h:64 * (h + 1), :] * jnp.exp(tot) + st
        zc = z_ref[pl.ds(r0, q), :]
        yg = yt_scr[c].T * (zc * _sigmoid(zc))
        ms = jnp.mean(yg * yg, axis=-1, keepdims=True)
        o_ref[pl.ds(r0, q), :] = (yg * lax.rsqrt(ms + EPS) * ng_ref[...]).astype(o_ref.dtype)
        return carry

    lax.fori_loop(0, nc, pass_bwd, 0, unroll=nc)

    if emit_state and has_prev:
        sfin_ref[0] = sprev_ref[...]
        sfin_ref[1] = s_scr[...]
    elif emit_state:
        sfin_ref[...] = s_scr[...]


def _ssd(xbc, z, dt, conv_w, conv_b, bias128, alog128, dskip, norm_g, tri, h0, s_prev, nb, seq, emit_state,
         side_mod=None):
    t = xbc.shape[0]
    has_h0 = h0 is not None
    has_prev = s_prev is not None
    const = lambda shape: pl.BlockSpec(shape, lambda b: (0,) * len(shape))
    in_specs = [
        pl.BlockSpec((seq, SSM_CONV_CH), lambda b: (b, 0)),
        pl.BlockSpec((seq, SSM_D_INNER), lambda b: (b, 0)),
        pl.BlockSpec((seq, 128), lambda b: (b, 0)),
        const((3, SSM_CONV_CH)), const((1, SSM_CONV_CH)), const((1, 128)), const((1, 128)),
        pl.BlockSpec(memory_space=pltpu.SMEM), const((1, SSM_D_INNER)), const((128, 128)),
    ]
    args = [xbc, z, dt, conv_w, conv_b, bias128, alog128, dskip, norm_g, tri]
    if has_h0:
        in_specs.append(pl.BlockSpec((None, 2, 512, 128), lambda b: (b, 0, 0, 0)))
        args.append(h0)
    if has_prev:
        in_specs.append(pl.BlockSpec((None, 2, 512, 128), lambda b: (b, 0, 0, 0)))
        args.append(s_prev)
    out_specs = [pl.BlockSpec((seq, SSM_D_INNER), lambda b: (b, 0))]
    out_shape = [jax.ShapeDtypeStruct((t, SSM_D_INNER), BF16)]
    if emit_state and has_prev:
        out_specs.append(pl.BlockSpec((None, 2, 2, 512, 128), lambda b: (b, 0, 0, 0, 0)))
        out_shape.append(jax.ShapeDtypeStruct((nb, 2, 2, 512, 128), F32))
    elif emit_state:
        out_specs.append(pl.BlockSpec((None, 2, 512, 128), lambda b: (b, 0, 0, 0)))
        out_shape.append(jax.ShapeDtypeStruct((nb, 2, 512, 128), F32))
    if side_mod is not None:
        cond8, w_mod, b_mod3, mod_layer = side_mod
        n6 = 6 * D_MODEL
        tn = n6 // nb
        assert tn * nb == n6 and tn % 128 == 0
        in_specs += [
            pl.BlockSpec((8, D_MODEL), lambda b: (0, 0)),
            pl.BlockSpec((None, D_MODEL, tn), lambda b: (mod_layer, 0, b)),
            pl.BlockSpec((None, 1, tn), lambda b: (mod_layer, 0, b)),
        ]
        args += [cond8, w_mod, b_mod3]
        out_specs.append(pl.BlockSpec((8, tn), lambda b: (0, b)))
        out_shape.append(jax.ShapeDtypeStruct((8, n6), F32))
    return pl.pallas_call(
        functools.partial(_ssd_kernel, seq=seq, has_h0=has_h0, emit_state=emit_state, has_prev=has_prev,
                          side_mod=side_mod is not None),
        grid=(nb,),
        in_specs=in_specs,
        out_specs=out_specs,
        out_shape=out_shape,
        scratch_shapes=[
            pltpu.VMEM((seq // SSM_CHUNK, SSM_D_INNER, SSM_CHUNK), F32),
            pltpu.VMEM((seq, 512), BF16),
            pltpu.VMEM((seq, 128), F32),
            pltpu.VMEM((seq // SSM_CHUNK, 128, SSM_CHUNK), F32),
            pltpu.VMEM((seq // SSM_CHUNK, SSM_D_INNER, SSM_CHUNK), F32),
            pltpu.VMEM((2, 512, 128), F32),
        ],
        compiler_params=_params("arbitrary"),
        name="ssd",
    )(*args)


def _merge_kernel(x_ref, h_ref, g_ref, b0, b1, b2, b3, wg0, wg1, wg2, wg3, wb_ref, wo_ref, o_ref):
    n = pl.program_id(1)

    @pl.when(n == 0)
    def _():
        o_ref[...] = x_ref[...]

    h = h_ref[...]
    merged = None
    for k, (b_ref, wg_ref) in enumerate(((b0, wg0), (b1, wg1), (b2, wg2), (b3, wg3))):
        gate = _sigmoid(_dot_nt(h, wg_ref[0]))
        term = gate * _dot(b_ref[...], wb_ref[k])
        merged = term if merged is None else merged + term
    o_ref[...] += g_ref[...] * _dot(merged.astype(BF16), wo_ref[...])


def _merge(x, h, mod3, branches, wt, gate_row0, w_branch, w_out, layer, row_fn):
    t = x.shape[0]
    tm, tn = 512, 512
    assert t % tm == 0
    nn = D_MODEL // tn
    rf = lambda m: row_fn(m * tm)
    gate_spec = lambda k: pl.BlockSpec(
        (pl.Element(1), pl.Element(tn), pl.Element(D_MODEL)),
        lambda m, n: (0, pl.multiple_of(gate_row0 + k * D_MODEL + n * tn, 16), 0))
    return pl.pallas_call(
        _merge_kernel,
        grid=(t // tm, nn),
        in_specs=[
            pl.BlockSpec((tm, D_MODEL), lambda m, n: (m, 0)),
            pl.BlockSpec((tm, D_MODEL), lambda m, n: (m, 0)),
            _mod_spec(2, rf),
        ] + [pl.BlockSpec((tm, MIX_W), lambda m, n: (m, 0))] * 4
          + [gate_spec(k) for k in range(4)] + [
            pl.BlockSpec((None, N_BRANCH, MIX_W, tn), lambda m, n: (layer, 0, 0, n)),
            pl.BlockSpec((None, tn, D_MODEL), lambda m, n: (layer, n, 0)),
        ],
        out_specs=pl.BlockSpec((tm, D_MODEL), lambda m, n: (m, 0)),
        out_shape=jax.ShapeDtypeStruct((t, D_MODEL), F32),
        compiler_params=_params("arbitrary", "arbitrary"),
        name="merge",
    )(x, h, mod3, *branches, wt, wt, wt, wt, w_branch, w_out)


def _ffn_kernel(x_ref, sc_ref, sh_ref, g_ref, w1_ref, w2_ref, o_ref, h_scr):
    n = pl.program_id(1)

    @pl.when(n == 0)
    def _():
        _rms_mod_rows(x_ref, sc_ref, sh_ref, h_scr)
        o_ref[...] = x_ref[...]

    a = jnp.maximum(_dot(h_scr[...], w1_ref[...].astype(BF16)), 0.0)
    o_ref[...] += g_ref[...] * _dot((a * a).astype(BF16), w2_ref[...].astype(BF16))


def _ffn(x, mod3, w1, w2, layer, row_fn):
    t = x.shape[0]
    tm, tf = 1024, 512
    assert t % tm == 0
    rf = lambda m: row_fn(m * tm)
    return pl.pallas_call(
        _ffn_kernel,
        grid=(t // tm, D_FF // tf),
        in_specs=[
            pl.BlockSpec((tm, D_MODEL), lambda m, n: (m, 0)),
            _mod_spec(4, rf), _mod_spec(3, rf), _mod_spec(5, rf),
            pl.BlockSpec((None, D_MODEL, tf), lambda m, n: (layer, 0, n)),
            pl.BlockSpec((None, tf, D_MODEL), lambda m, n: (layer, n, 0)),
        ],
        out_specs=pl.BlockSpec((tm, D_MODEL), lambda m, n: (m, 0)),
        out_shape=jax.ShapeDtypeStruct((t, D_MODEL), F32),
        scratch_shapes=[pltpu.VMEM((tm, D_MODEL), BF16)],
        compiler_params=_params("arbitrary", "arbitrary"),
        name="ffn",
    )(x, mod3, mod3, mod3, w1, w2)


def _rope_tables(seq):
    half = 16
    freqs = ROPE_THETA ** (-jnp.arange(half, dtype=F32) / half)
    n_rows = seq // GRID_W
    rows = jnp.repeat(jnp.arange(n_rows), GRID_W).astype(F32)
    cols = jnp.tile(jnp.arange(GRID_W), n_rows).astype(F32)
    ar = rows[:, None] * freqs[None, :]
    ac = cols[:, None] * freqs[None, :]
    cos64 = jnp.concatenate([jnp.cos(ar), jnp.cos(ar), jnp.cos(ac), jnp.cos(ac)], axis=-1)
    sin64 = jnp.concatenate([-jnp.sin(ar), jnp.sin(ar), -jnp.sin(ac), jnp.sin(ac)], axis=-1)
    return (jnp.tile(cos64, (1, ATT_HEADS)), jnp.tile(sin64, (1, ATT_HEADS)),
            jnp.tile(cos64, (1, ATT_KV_HEADS)), jnp.tile(sin64, (1, ATT_KV_HEADS)))


def _layer_path(x, mod3, lw, row_fn, nb, seq, lat, s_prev=None, side_casts=(), side_mod=None):
    (wt, qg, kg, gsum, conv_w, conv_b, bias128, alog128, dskip, ssm_g, tri, gmlp_g, ws, bs_t,
     w_ff1, w_ff2, layer) = lw[:-1]
    bf = lw[-1]
    h, qkv, f_in, xbc, z, dt, o_gmlp, *side = _inproj(x, mod3, wt, gmlp_g, ws, bs_t, row_fn, side_casts)
    if bf is None:
        bf = (side[1], 0, side[2].reshape(DEPTH, N_BRANCH, MIX_W, D_MODEL), side[3].reshape(DEPTH, D_MODEL, D_MODEL))
    wg, gate_row0, w_branch, w_out = bf
    if lat is None:
        o_att, k_new, v_new = _attention_ctx(qkv, qg, kg, gsum, nb, seq)
        h0 = None
    else:
        rope_tabs, cache_k, cache_v, h0 = lat
        o_att = _attention_lat(qkv, qg, kg, gsum, rope_tabs, cache_k, cache_v, nb, seq)
        k_new = v_new = None
    o_fnet = _fourier(f_in, nb, seq)
    ssd_out = _ssd(xbc, z, dt, conv_w, conv_b, bias128, alog128, dskip, ssm_g, tri, h0, s_prev, nb, seq,
                   emit_state=lat is None, side_mod=side_mod)
    o_ssm = ssd_out[0]
    s_fin = ssd_out[1] if lat is None else None
    mod_next = ssd_out[-1] if side_mod is not None else None
    x = _merge(x, h, mod3, (o_att, o_fnet, o_ssm, o_gmlp), wg, gate_row0, w_branch, w_out, layer, row_fn)
    x = _ffn(x, mod3, w_ff1, w_ff2, layer, row_fn)
    return x, k_new, v_new, s_fin, side, bf, mod_next


def kernel(x_prompt, x_sample, c, cache_k, cache_v, state_ssm, c_ctx, w_mod, b_mod, w_in, q_norm_g, k_norm_g,
           conv_w, conv_b, a_log, dt_bias, d_skip, ssm_norm_g, gmlp_norm_g, w_spatial, b_spatial, w_branch,
           w_out, w_ff1, w_ff2):
    nb_ctx, seq_ctx, _ = x_prompt.shape
    nb_lat, seq_lat, _ = x_sample.shape
    assert seq_lat == LAT_LEN and 1 + nb_lat <= 8

    cond8 = jnp.zeros((8, D_MODEL), F32).at[0].set(c_ctx).at[1:1 + nb_lat].set(c)
    b_mod3 = b_mod.reshape(DEPTH, 1, 6 * D_MODEL)
    mod_l = _modulation(cond8, w_mod, b_mod3, 0)

    gsum = jnp.asarray(np.kron(np.eye(ATT_HEADS), np.ones((HEAD_DIM, HEAD_DIM))), BF16)
    tri = jnp.asarray(np.tril(np.ones((SSM_CHUNK, SSM_CHUNK))), BF16)
    rope_tabs = _rope_tables(seq_lat)

    xp = x_prompt.reshape(nb_ctx * seq_ctx, D_MODEL)
    xs = x_sample.reshape(nb_lat * seq_lat, D_MODEL)
    ctx_row = lambda tok: 0
    lat_row = lambda tok: 1 + tok // LAT_LEN

    w_in_t = jnp.swapaxes(w_in, 1, 2)
    wt = _cast_layer_bf16(w_in_t, 0, GATE_COL0)
    bf = None

    assert DEPTH == 2
    new_k, new_v, s_l = [], [], None
    for l in range(DEPTH):
        zeros80 = jnp.zeros((128 - 48,), F32)
        bias128 = jnp.concatenate([jnp.tile(dt_bias[l].reshape(16), 3), zeros80]).reshape(1, 128)
        alog128 = jnp.concatenate([jnp.zeros((16,), F32), jnp.tile(a_log[l].reshape(16), 2), zeros80]).reshape(1, 128)
        lw = (
            wt,
            jnp.tile(q_norm_g[l], ATT_HEADS).reshape(1, 512), jnp.tile(k_norm_g[l], ATT_KV_HEADS).reshape(1, 128),
            gsum, conv_w[l], conv_b[l].reshape(1, SSM_CONV_CH), bias128, alog128,
            d_skip[l], ssm_norm_g[l].reshape(1, SSM_D_INNER), tri,
            gmlp_norm_g[l].reshape(1, MIX_W), w_spatial[l].astype(BF16), b_spatial[l].T,
            w_ff1, w_ff2, l, bf,
        )
        mod3 = mod_l.reshape(8, 1, 6 * D_MODEL)
        side_mod = (cond8, w_mod, b_mod3, l + 1) if l + 1 < DEPTH else None
        side_casts = []
        if l == 0:
            side_casts = [(w_in_t, 1, 0, N_IN), (w_in_t, 0, GATE_COL0, N_BRANCH * D_MODEL),
                          (w_branch.reshape(-1, D_MODEL), None, 0, DEPTH * N_BRANCH * MIX_W),
                          (w_out.reshape(-1, D_MODEL), None, 0, DEPTH * D_MODEL)]
        xp, k_l, v_l, s_l, side, bf, mod_next = _layer_path(xp, mod3, lw, ctx_row, nb_ctx, seq_ctx, None, s_prev=s_l,
                                                            side_casts=side_casts, side_mod=side_mod)
        lw = lw[:-1] + (bf,)
        new_k.append(k_l.reshape(nb_ctx, seq_ctx, ATT_KV_HEADS, HEAD_DIM))
        new_v.append(v_l.reshape(nb_ctx, seq_ctx, ATT_KV_HEADS, HEAD_DIM))
        lat = (rope_tabs,
               cache_k[:, l].reshape(nb_lat, -1, ATT_KV_HEADS * HEAD_DIM),
               cache_v[:, l].reshape(nb_lat, -1, ATT_KV_HEADS * HEAD_DIM),
               state_ssm[:, l].reshape(nb_lat, 2, SSM_HEADS * SSM_HEAD_DIM, SSM_STATE))
        xs = _layer_path(xs, mod3, lw, lat_row, nb_lat, seq_lat, lat)[0]
        if l == 0:
            wt = side[0]
            bf = (wt, GATE_COL0) + bf[2:]
            mod_l = mod_next

    return (xp.reshape(nb_ctx, seq_ctx, D_MODEL), xs.reshape(nb_lat, seq_lat, D_MODEL),
            jnp.stack(new_k, axis=1), jnp.stack(new_v, axis=1),
            s_l.reshape(nb_ctx, DEPTH, 2, SSM_HEADS, SSM_HEAD_DIM, SSM_STATE))
```

```python
import functools
import math

import numpy as np
import jax
import jax.numpy as jnp
from jax import lax
from jax.experimental import pallas as pl
from jax.experimental.pallas import tpu as pltpu

F32 = jnp.float32
BF16 = jnp.bfloat16

D_MODEL = 2048
DEPTH = 2
GRID_W = 64
MIX_W = 512
N_BRANCH = 4
ATT_HEADS = 8
ATT_KV_HEADS = 2
HEAD_DIM = 64
ROPE_THETA = 10000.0
FNET_GW = 128
SSM_HEADS = 8
SSM_HEAD_DIM = 64
SSM_STATE = 128
SSM_CHUNK = 128
SSM_D_INNER = 512
SSM_CONV_CH = 1024
GMLP_CHUNK = 128
D_FF = 4 * D_MODEL
EPS = 1e-6
LAT_LEN = 1024

QKV_W = 768
COL_F = 768
COL_XBC = 1280
COL_Z = 2304
COL_DT = 2816
COL_UV = 2832
GATE_COL0 = 3856
N_IN = GATE_COL0 + N_BRANCH * D_MODEL
_IN_PIECES = ((0, QKV_W), (COL_F, COL_XBC), (COL_XBC, COL_Z), (COL_Z, COL_DT),
              (COL_DT, COL_DT + 128),
              (COL_UV, GATE_COL0))

VMEM_LIMIT = 56 * 1024 * 1024
NEG_BIG = -1e30


def _params(*sem):
    return pltpu.CompilerParams(dimension_semantics=sem, vmem_limit_bytes=VMEM_LIMIT)


def _dot(a, b):
    return jnp.dot(a, b, preferred_element_type=F32)


def _dot_nt(a, b):
    return lax.dot_general(a, b, (((1,), (1,)), ((), ())), preferred_element_type=F32)


def _dot_tn(a, b):
    return lax.dot_general(a, b, (((0,), (0,)), ((), ())), preferred_element_type=F32)


def _sigmoid(x):
    return 1.0 / (1.0 + jnp.exp(-x))


def _split2(x):
    hi = x.astype(BF16)
    lo = (x - hi.astype(F32)).astype(BF16)
    return hi, lo


def _rms_mod(x, sc, sh):
    ms = jnp.mean(x * x, axis=-1, keepdims=True)
    return x * lax.rsqrt(ms + EPS) * (1.0 + sc) + sh


def _rms_mod_rows(x_ref, sc_ref, sh_ref, dst_ref):
    slab = 16

    def body(i, carry):
        rows = pl.ds(pl.multiple_of(i * slab, slab), slab)
        dst_ref[rows, :] = _rms_mod(x_ref[rows, :], sc_ref[...], sh_ref[...]).astype(dst_ref.dtype)
        return carry

    lax.fori_loop(0, x_ref.shape[0] // slab, body, 0, unroll=16)


def _mod_kernel(cond_ref, w_ref, b_ref, o_ref):
    c = cond_ref[...]
    s = c * _sigmoid(c)
    o_ref[...] = _dot(s.astype(BF16), w_ref[...].astype(BF16)) + b_ref[...]


def _modulation(cond8, w_mod, b_mod3, layer):
    tn = 1024
    n6 = 6 * D_MODEL
    return pl.pallas_call(
        _mod_kernel,
        grid=(n6 // tn,),
        in_specs=[
            pl.BlockSpec((8, D_MODEL), lambda n: (0, 0)),
            pl.BlockSpec((None, D_MODEL, tn), lambda n: (layer, 0, n)),
            pl.BlockSpec((None, 1, tn), lambda n: (layer, 0, n)),
        ],
        out_specs=pl.BlockSpec((8, tn), lambda n: (0, n)),
        out_shape=jax.ShapeDtypeStruct((8, n6), F32),
        compiler_params=_params("arbitrary"),
        name="modulation",
    )(cond8, w_mod, b_mod3)


def _mod_spec(chunk, row_fn):
    return pl.BlockSpec((None, 1, D_MODEL), lambda m, *_: (row_fn(m), 0, chunk))


def _cast_kernel(w_ref, o_ref):
    o_ref[...] = w_ref[...].astype(o_ref.dtype)


def _cast_layer_bf16(w, layer, n_rows):
    rows = 1024
    return pl.pallas_call(
        _cast_kernel,
        grid=(pl.cdiv(n_rows, rows),),
        in_specs=[pl.BlockSpec((None, rows, D_MODEL), lambda r: (layer, r, 0))],
        out_specs=pl.BlockSpec((None, rows, D_MODEL), lambda r: (0, r, 0)),
        out_shape=jax.ShapeDtypeStruct((1, n_rows, D_MODEL), BF16),
        compiler_params=_params("arbitrary"),
        name="cast_layer_bf16",
    )(w)


def _gmlp_chunk(x, gain, ws_ref, bs_ref):
    c0 = math.sqrt(2.0 / math.pi)
    ge = x * (0.5 * (1.0 + jnp.tanh(c0 * (x + 0.044715 * (x * x * x)))))
    u = ge[:, :MIX_W]
    v = ge[:, MIX_W:]
    v = v * lax.rsqrt(jnp.mean(v * v, axis=-1, keepdims=True) + EPS) * gain
    outs = []
    for g in range(MIX_W // 128):
        gl = slice(g * 128, (g + 1) * 128)
        mixed = _dot(ws_ref[g], v[:, gl].astype(BF16)) + bs_ref[:, g:g + 1]
        outs.append(u[:, gl] * mixed)
    return jnp.concatenate(outs, axis=1)


def _inproj_kernel(*refs, n_side):
    x_ref, sc_ref, sh_ref, w_ref, gg_ref, ws_ref, bs_ref = refs[:7]
    side_in = refs[7:7 + n_side]
    h_ref = refs[7 + n_side]
    out_refs = refs[8 + n_side:len(refs) - n_side]
    side_out = refs[len(refs) - n_side:]
    for src, dst in zip(side_in, side_out):
        dst[...] = src[...].astype(dst.dtype)
    hb = _rms_mod(x_ref[...], sc_ref[...], sh_ref[...]).astype(BF16)
    h_ref[...] = hb
    o_gmlp = out_refs[-1]
    a, b = _IN_PIECES[-1]
    uv = _dot_nt(hb, w_ref[a:b, :])
    for c in range(x_ref.shape[0] // GMLP_CHUNK):
        sl = slice(c * GMLP_CHUNK, (c + 1) * GMLP_CHUNK)
        o_gmlp[sl, :] = _gmlp_chunk(uv[sl, :], gg_ref[...], ws_ref, bs_ref).astype(o_gmlp.dtype)
    for o_ref, (a, b) in zip(out_refs[:-1], _IN_PIECES[:-1]):
        o_ref[...] = _dot_nt(hb, w_ref[a:b, :])


def _inproj(x, mod3, wt, gmlp_g, ws, bs_t, row_fn, side_casts=()):
    t = x.shape[0]
    tm = 256
    assert t % tm == 0
    steps = t // tm
    widths = [b - a for a, b in _IN_PIECES[:-1]]
    const = lambda shape: pl.BlockSpec(shape, lambda m: (0,) * len(shape))
    in_specs = [
        pl.BlockSpec((tm, D_MODEL), lambda m: (m, 0)),
        _mod_spec(1, lambda m: row_fn(m * tm)),
        _mod_spec(0, lambda m: row_fn(m * tm)),
        pl.BlockSpec((None, GATE_COL0, D_MODEL), lambda m: (0, 0, 0), pipeline_mode=pl.Buffered(1)),
        const((1, MIX_W)), const((4, 128, 128)), const((128, 4)),
    ]
    args = [x, mod3, mod3, wt, gmlp_g, ws, bs_t]
    out_specs = ([pl.BlockSpec((tm, D_MODEL), lambda m: (m, 0))]
                 + [pl.BlockSpec((tm, w), lambda m: (m, 0)) for w in widths]
                 + [pl.BlockSpec((tm, MIX_W), lambda m: (m, 0))])
    out_shape = ([jax.ShapeDtypeStruct((t, D_MODEL), BF16)]
                 + [jax.ShapeDtypeStruct((t, w), F32) for w in widths]
                 + [jax.ShapeDtypeStruct((t, MIX_W), BF16)])
    side_in_specs, side_args = [], []
    for w_src, lead, row0, n_rows in side_casts:
        rows = -(-n_rows // (16 * steps)) * 16
        assert pl.cdiv(n_rows, rows) == steps
        if lead is None:
            side_in_specs.append(pl.BlockSpec((rows, D_MODEL), lambda m: (m, 0)))
            out_specs.append(pl.BlockSpec((rows, D_MODEL), lambda m: (m, 0)))
            out_shape.append(jax.ShapeDtypeStruct((n_rows, D_MODEL), BF16))
        elif row0 == 0:
            side_in_specs.append(pl.BlockSpec((None, rows, D_MODEL), lambda m, lead=lead: (lead, m, 0)))
            out_specs.append(pl.BlockSpec((None, rows, D_MODEL), lambda m: (0, m, 0)))
            out_shape.append(jax.ShapeDtypeStruct((1, n_rows, D_MODEL), BF16))
        else:
            assert n_rows == rows * steps and row0 % 16 == 0
            side_in_specs.append(pl.BlockSpec(
                (pl.Element(1), pl.Element(rows), pl.Element(D_MODEL)),
                lambda m, lead=lead, row0=row0, rows=rows: (lead, pl.multiple_of(row0 + m * rows, 16), 0)))
            out_specs.append(pl.BlockSpec((1, rows, D_MODEL), lambda m: (0, m, 0)))
            out_shape.append(jax.ShapeDtypeStruct((1, n_rows, D_MODEL), BF16))
        side_args.append(w_src)
    in_specs += side_in_specs
    args += side_args
    return pl.pallas_call(
        functools.partial(_inproj_kernel, n_side=len(side_casts)),
        grid=(steps,),
        in_specs=in_specs,
        out_specs=out_specs,
        out_shape=out_shape,
        compiler_params=_params("arbitrary"),
        name="inproj",
    )(*args)


def _head_rms(t, gsum, gain):
    t2 = t * t
    hi, lo = _split2(t2)
    ss = _dot(hi, gsum) + _dot(lo, gsum)
    return t * lax.rsqrt(ss * (1.0 / HEAD_DIM) + EPS) * gain


def _rope(t, cos, sin_signed):
    w = t.shape[-1]
    up = pltpu.roll(t, w - 16, 1)
    dn = pltpu.roll(t, 16, 1)
    lane = lax.broadcasted_iota(jnp.int32, t.shape, 1)
    partner = jnp.where((lane & 31) < 16, up, dn)
    return t * cos + partner * sin_signed


ATT_GROUP = ATT_HEADS // ATT_KV_HEADS
GROUP_W = ATT_GROUP * HEAD_DIM


def _store_group_keys(k, kbd_scr, row0):
    s = k.shape[0]
    k2 = jnp.concatenate([k, k], axis=1)
    k2r = pltpu.roll(k2, HEAD_DIM, 1)
    lane = lax.broadcasted_iota(jnp.int32, k2.shape, 1)
    first = (lane & (2 * HEAD_DIM - 1)) < HEAD_DIM
    for j, tiled in enumerate((jnp.where(first, k2, k2r), jnp.where(first, k2r, k2))):
        for hh in range(ATT_GROUP):
            slab = jnp.where(lax.shift_right_logical(lane, 6) == hh, tiled, 0.0)
            kbd_scr[j, hh, row0:row0 + s, :] = slab.astype(kbd_scr.dtype)


def _attend_heads(q, kbd_scr, vt_scr, o_ref):
    qb = (q * (HEAD_DIM ** -0.5 * math.log2(math.e))).astype(BF16)
    s_len = kbd_scr.shape[2]
    for j in range(ATT_KV_HEADS):
        kbd = kbd_scr[j].reshape(ATT_GROUP * s_len, GROUP_W)
        s_all = _dot_nt(kbd, qb[:, j * GROUP_W:(j + 1) * GROUP_W])
        outs = []
        for hh in range(ATT_GROUP):
            s = s_all[hh * s_len:(hh + 1) * s_len, :]
            m = jnp.max(s, axis=0, keepdims=True)
            e = jnp.exp2(s - m)
            den = jnp.sum(e, axis=0, keepdims=True)
            outs.append(_dot(vt_scr[j * HEAD_DIM:(j + 1) * HEAD_DIM, :], e.astype(BF16)) / den)
        o_ref[:, j * GROUP_W:(j + 1) * GROUP_W] = jnp.concatenate(outs, axis=0).T.astype(o_ref.dtype)


def _attn_ctx_kernel(q_ref, kv_ref, qg_ref, kg_ref, gsum_ref, o_ref, kout_ref, vout_ref, kbd_scr, vt_scr):
    gsum = gsum_ref[...]
    seq = vt_scr.shape[-1]
    for i in range(vt_scr.shape[0]):
        rows = pl.ds(i * seq, seq)
        kv = kv_ref[rows, :]
        k = _head_rms(kv[:, :128], gsum[:128, :128], kg_ref[...])
        v = kv[:, 128:]
        kout_ref[rows, :] = k
        vout_ref[rows, :] = v
        _store_group_keys(k, kbd_scr.at[i], 0)
        vt_scr[i] = v.T.astype(BF16)
        q = _head_rms(q_ref[rows, :], gsum, qg_ref[...])
        _attend_heads(q, kbd_scr.at[i], vt_scr.at[i], o_ref.at[rows])


def _attn_lat_kernel(q_ref, kv_ref, qg_ref, kg_ref, gsum_ref, cq_ref, sq_ref, ck_ref, sk_ref,
                     cache_k_ref, cache_v_ref, o_ref, kbd_scr, vt_scr):
    gsum = gsum_ref[...]
    past = cache_k_ref.shape[0]

    @pl.when(pl.program_id(1) == 0)
    def _():
        kv = kv_ref[...]
        k = _head_rms(kv[:, :128], gsum[:128, :128], kg_ref[...])
        k = _rope(k, ck_ref[...], sk_ref[...])
        _store_group_keys(cache_k_ref[...], kbd_scr, 0)
        _store_group_keys(k, kbd_scr, past)
        vt_scr[:, :past] = cache_v_ref[...].T.astype(BF16)
        vt_scr[:, past:] = kv[:, 128:].T.astype(BF16)

    q = _head_rms(q_ref[...], gsum, qg_ref[...])
    q = _rope(q, cq_ref[...], sq_ref[...])
    _attend_heads(q, kbd_scr, vt_scr, o_ref)


def _attention_ctx(qkv, qg, kg, gsum, nb, seq):
    t = qkv.shape[0]
    per = 4 if nb % 4 == 0 else 1
    rows = per * seq
    const = lambda shape: pl.BlockSpec(shape, lambda b: (0, 0))
    return pl.pallas_call(
        _attn_ctx_kernel,
        grid=(nb // per,),
        in_specs=[
            pl.BlockSpec((rows, 512), lambda b: (b, 0)),
            pl.BlockSpec((rows, 256), lambda b: (b, 2)),
            const((1, 512)), const((1, 128)), const((512, 512)),
        ],
        out_specs=[
            pl.BlockSpec((rows, 512), lambda b: (b, 0)),
            pl.BlockSpec((rows, 128), lambda b: (b, 0)),
            pl.BlockSpec((rows, 128), lambda b: (b, 0)),
        ],
        out_shape=[
            jax.ShapeDtypeStruct((t, 512), BF16),
            jax.ShapeDtypeStruct((t, 128), F32),
            jax.ShapeDtypeStruct((t, 128), F32),
        ],
        scratch_shapes=[pltpu.VMEM((per, ATT_KV_HEADS, ATT_GROUP, seq, GROUP_W), BF16),
                        pltpu.VMEM((per, 128, seq), BF16)],
        compiler_params=_params("arbitrary"),
        name="attn_ctx",
    )(qkv, qkv, qg, kg, gsum)


def _attention_lat(qkv, qg, kg, gsum, rope_tabs, cache_k, cache_v, nb, seq):
    t = qkv.shape[0]
    qblk = 1024
    nq = seq // qblk
    past = cache_k.shape[1]
    cq, sq, ck, sk = rope_tabs
    const = lambda shape: pl.BlockSpec(shape, lambda b, i: (0, 0))
    return pl.pallas_call(
        _attn_lat_kernel,
        grid=(nb, nq),
        in_specs=[
            pl.BlockSpec((qblk, 512), lambda b, i: (b * nq + i, 0)),
            pl.BlockSpec((seq, 256), lambda b, i: (b, 2)),
            const((1, 512)), const((1, 128)), const((512, 512)),
            pl.BlockSpec((qblk, 512), lambda b, i: (i, 0)),
            pl.BlockSpec((qblk, 512), lambda b, i: (i, 0)),
            const((seq, 128)), const((seq, 128)),
            pl.BlockSpec((None, past, 128), lambda b, i: (b, 0, 0)),
            pl.BlockSpec((None, past, 128), lambda b, i: (b, 0, 0)),
        ],
        out_specs=pl.BlockSpec((qblk, 512), lambda b, i: (b * nq + i, 0)),
        out_shape=jax.ShapeDtypeStruct((t, 512), BF16),
        scratch_shapes=[pltpu.VMEM((ATT_KV_HEADS, ATT_GROUP, past + seq, GROUP_W), BF16),
                        pltpu.VMEM((128, past + seq), BF16)],
        compiler_params=_params("arbitrary", "arbitrary"),
        name="attn_lat",
    )(qkv, qkv, qg, kg, gsum, cq, sq, ck, sk, cache_k, cache_v)


def _fourier_kernel(f_ref, bdh_ref, bdl_ref, cl_ref, sl_ref, o_ref, *, scale):
    bdh = bdh_ref[...]
    bdl = bdl_ref[...]
    cl = cl_ref[...].astype(BF16)
    sl = sl_ref[...].astype(BF16)
    seq = cl_ref.shape[0]
    for i in range(f_ref.shape[0] // seq):
        rows = pl.ds(i * seq, seq)
        xh, xl = _split2(f_ref[rows, :])
        yc, ys = [], []
        for g in range(MIX_W // FNET_GW):
            gl = slice(g * FNET_GW, (g + 1) * FNET_GW)
            y = _dot(xh[:, gl], bdh) + _dot(xl[:, gl], bdh) + _dot(xh[:, gl], bdl)
            yc.append(y[:, :FNET_GW])
            ys.append(y[:, FNET_GW:])
        re = _dot(cl, jnp.concatenate(yc, axis=1).astype(BF16))
        im = _dot(sl, jnp.concatenate(ys, axis=1).astype(BF16))
        o_ref[rows, :] = ((re - im) * scale).astype(o_ref.dtype)


def _dft_consts(seq):
    def cs(n):
        k = np.arange(n, dtype=np.int64)
        ang = 2.0 * np.pi * ((k[:, None] * k[None, :]) % n).astype(np.float64) / n
        return np.cos(ang), np.sin(ang)

    bd = np.concatenate(cs(FNET_GW), axis=1)
    cl, sl_ = cs(seq)

    def hl(a):
        a32 = jnp.asarray(a, F32)
        hi = a32.astype(BF16)
        return hi, (a32 - hi.astype(F32)).astype(BF16)

    return hl(bd) + (jnp.asarray(cl, F32), jnp.asarray(sl_, F32))


def _fourier(f, nb, seq):
    t = f.shape[0]
    consts = _dft_consts(seq)
    const = lambda a: pl.BlockSpec(a.shape, lambda b: (0, 0))
    per = 4 if (seq <= 256 and nb % 4 == 0) else 1
    return pl.pallas_call(
        functools.partial(_fourier_kernel, scale=1.0 / math.sqrt(seq * FNET_GW)),
        grid=(nb // per,),
        in_specs=[pl.BlockSpec((per * seq, MIX_W), lambda b: (b, 0))] + [const(a) for a in consts],
        out_specs=pl.BlockSpec((per * seq, MIX_W), lambda b: (b, 0)),
        out_shape=jax.ShapeDtypeStruct((t, MIX_W), BF16),
        compiler_params=_params("arbitrary"),
        name="fourier",
    )(f, *consts)


def _ssd_kernel(*refs, seq, has_h0, emit_state, has_prev, side_mod):
    (xbc_ref, z_ref, dt_ref, cw_ref, cb_ref, bias_ref, alog_ref, dskip_ref, ng_ref, tri_ref) = refs[:10]
    rest = list(refs[10:])
    h0_ref = rest.pop(0) if has_h0 else None
    sprev_ref = rest.pop(0) if has_prev else None
    mod_in = [rest.pop(0) for _ in range(3)] if side_mod else None
    o_ref = rest.pop(0)
    sfin_ref = rest.pop(0) if emit_state else None
    if side_mod:
        cond_ref, wmod_ref, bmod_ref = mod_in
        _mod_kernel(cond_ref, wmod_ref, bmod_ref, rest.pop(0))
    xt_scr, bc_scr, tab, tabt, yt_scr, s_scr = rest
    nc = seq // SSM_CHUNK
    q = SSM_CHUNK

    lane = lax.broadcasted_iota(jnp.int32, (q, 128), 1)
    ii = lax.broadcasted_iota(jnp.int32, (q, q), 0)
    jj = lax.broadcasted_iota(jnp.int32, (q, q), 1)
    below = ii > jj
    above = ii < jj
    row = lax.broadcasted_iota(jnp.int32, (q, 1), 0)
    cw = cw_ref[...]
    amul = jnp.where(lane < 16, 1.0, jnp.where(lane < 48, -jnp.exp(alog_ref[...]), 0.0))
    tri = tri_ref[...]

    if has_h0:
        s_scr[...] = h0_ref[...]
    else:
        s_scr[...] = jnp.zeros_like(s_scr)

    def prep(c, carry):
        r0 = pl.multiple_of(c * q, q)
        cur = xbc_ref[pl.ds(r0, q), :]
        prev8 = xbc_ref[pl.ds(pl.multiple_of(jnp.maximum(r0 - 8, 0), 8), 8), :]
        next8 = xbc_ref[pl.ds(pl.multiple_of(jnp.minimum(r0 + q, seq - 8), 8), 8), :]
        prev_row = jnp.where(c > 0, prev8[7:8, :], 0.0)
        next_row = jnp.where(c < nc - 1, next8[0:1, :], 0.0)
        up = jnp.where(row == 0, prev_row, pltpu.roll(cur, 1, 0))
        dn = jnp.where(row == q - 1, next_row, pltpu.roll(cur, q - 1, 0))
        conv = up * cw[0:1, :] + cur * cw[1:2, :] + dn * cw[2:3, :] + cb_ref[...]
        act = conv * _sigmoid(conv)
        xt_scr[c] = act[:, :SSM_D_INNER].T
        bc_scr[pl.ds(r0, q), :] = act[:, SSM_D_INNER:].astype(BF16)

        raw = dt_ref[pl.ds(r0, q), :]
        rep = jnp.where(lane < 16, raw, jnp.where(lane < 32, pltpu.roll(raw, 16, 1), pltpu.roll(raw, 32, 1)))
        v = rep + bias_ref[...]
        sp = jnp.maximum(v, 0.0) + jnp.log1p(jnp.exp(-jnp.abs(v)))
        t = sp * amul
        hi = t.astype(BF16)
        r1 = t - hi.astype(F32)
        mid = r1.astype(BF16)
        lo = (r1 - mid.astype(F32)).astype(BF16)
        cum = _dot(tri, hi) + _dot(tri, mid) + _dot(tri, lo)
        tc = jnp.where(lane < 32, t, cum)
        tab[pl.ds(r0, q), :] = tc
        tabt[c] = tc.T
        return carry

    lax.fori_loop(0, nc, prep, 0, unroll=nc)

    def pass_fwd(c, carry):
        r0 = pl.multiple_of(c * q, q)
        pc = tab[pl.ds(r0, q), :]
        pt = tabt[c]
        for g in range(2):
            bg = bc_scr[pl.ds(r0, q), 128 * g:128 * (g + 1)]
            cg = bc_scr[pl.ds(r0, q), 256 + 128 * g:256 + 128 * (g + 1)]
            cb = _dot_nt(cg, bg)
            y_in = _dot_nt(s_scr[0, 256 * g:256 * (g + 1), :].astype(BF16), cg)
            for hh in range(4):
                h = 4 * g + hh
                acf_row = pt[32 + h:33 + h, :]
                eb_col = pc[:, 40 + h:41 + h] - pc[:, 24 + h:25 + h]
                eb_row = pt[40 + h:41 + h, :] - pt[24 + h:25 + h, :]
                dtf_row = pt[h:h + 1, :]
                dtb_row = pt[8 + h:9 + h, :]
                seg = jnp.where(below, pc[:, 32 + h:33 + h] - acf_row, jnp.where(above, eb_row - eb_col, 0.0))
                wgt = jnp.where(below, dtf_row, jnp.where(above, dtb_row, dtf_row + dtb_row))
                mp = cb * (jnp.exp(seg) * wgt)
                xth = xt_scr[c, 64 * h:64 * (h + 1), :]
                yd = _dot_nt(xth.astype(BF16), mp.astype(BF16))
                yo = y_in[64 * hh:64 * (hh + 1), :] * jnp.exp(acf_row)
                yt_scr[c, 64 * h:64 * (h + 1), :] = yd + yo + dskip_ref[h] * xth
                tot = pt[32 + h:33 + h, q - 1:q]
                wf_row = dtf_row * jnp.exp(tot - acf_row)
                st = _dot((xth * wf_row).astype(BF16), bg)
                s_scr[0, 64 * h:64 * (h + 1), :] = s_scr[0, 64 * h:64 * (h + 1), :] * jnp.exp(tot) + st
        return carry

    lax.fori_loop(0, nc, pass_fwd, 0, unroll=nc)

    def pass_bwd(ci, carry):
        c = nc - 1 - ci
        r0 = pl.multiple_of(c * q, q)
        pt = tabt[c]
        for g in range(2):
            bg = bc_scr[pl.ds(r0, q), 128 * g:128 * (g + 1)]
            cg = bc_scr[pl.ds(r0, q), 256 + 128 * g:256 + 128 * (g + 1)]
            y_in = _dot_nt(s_scr[1, 256 * g:256 * (g + 1), :].astype(BF16), cg)
            for hh in range(4):
                h = 4 * g + hh
                eb_row = pt[40 + h:41 + h, :] - pt[24 + h:25 + h, :]
                tot = pt[40 + h:41 + h, q - 1:q]
                xth = xt_scr[c, 64 * h:64 * (h + 1), :]
                yt_scr[c, 64 * h:64 * (h + 1), :] += y_in[64 * hh:64 * (hh + 1), :] * jnp.exp(tot - eb_row)
                wb_row = pt[8 + h:9 + h, :] * jnp.exp(eb_row)
                st = _dot((xth * wb_row).astype(BF16), bg)
                s_scr[1, 64 * h:64 * (h + 1), :] = s_scr[1, 64 * h:64 * (h + 1), :] * jnp.exp(tot) + st
        zc = z_ref[pl.ds(r0, q), :]
        yg = yt_scr[c].T * (zc * _sigmoid(zc))
        ms = jnp.mean(yg * yg, axis=-1, keepdims=True)
        o_ref[pl.ds(r0, q), :] = (yg * lax.rsqrt(ms + EPS) * ng_ref[...]).astype(o_ref.dtype)
        return carry

    lax.fori_loop(0, nc, pass_bwd, 0, unroll=nc)

    if emit_state and has_prev:
        sfin_ref[0] = sprev_ref[...]
        sfin_ref[1] = s_scr[...]
    elif emit_state:
        sfin_ref[...] = s_scr[...]


def _ssd(xbc, z, dt, conv_w, conv_b, bias128, alog128, dskip, norm_g, tri, h0, s_prev, nb, seq, emit_state,
         side_mod=None):
    t = xbc.shape[0]
    has_h0 = h0 is not None
    has_prev = s_prev is not None
    const = lambda shape: pl.BlockSpec(shape, lambda b: (0,) * len(shape))
    in_specs = [
        pl.BlockSpec((seq, SSM_CONV_CH), lambda b: (b, 0)),
        pl.BlockSpec((seq, SSM_D_INNER), lambda b: (b, 0)),
        pl.BlockSpec((seq, 128), lambda b: (b, 0)),
        const((3, SSM_CONV_CH)), const((1, SSM_CONV_CH)), const((1, 128)), const((1, 128)),
        pl.BlockSpec(memory_space=pltpu.SMEM), const((1, SSM_D_INNER)), const((128, 128)),
    ]
    args = [xbc, z, dt, conv_w, conv_b, bias128, alog128, dskip, norm_g, tri]
    if has_h0:
        in_specs.append(pl.BlockSpec((None, 2, 512, 128), lambda b: (b, 0, 0, 0)))
        args.append(h0)
    if has_prev:
        in_specs.append(pl.BlockSpec((None, 2, 512, 128), lambda b: (b, 0, 0, 0)))
        args.append(s_prev)
    out_specs = [pl.BlockSpec((seq, SSM_D_INNER), lambda b: (b, 0))]
    out_shape = [jax.ShapeDtypeStruct((t, SSM_D_INNER), BF16)]
    if emit_state and has_prev:
        out_specs.append(pl.BlockSpec((None, 2, 2, 512, 128), lambda b: (b, 0, 0, 0, 0)))
        out_shape.append(jax.ShapeDtypeStruct((nb, 2, 2, 512, 128), F32))
    elif emit_state:
        out_specs.append(pl.BlockSpec((None, 2, 512, 128), lambda b: (b, 0, 0, 0)))
        out_shape.append(jax.ShapeDtypeStruct((nb, 2, 512, 128), F32))
    if side_mod is not None:
        cond8, w_mod, b_mod3, mod_layer = side_mod
        n6 = 6 * D_MODEL
        tn = n6 // nb
        assert tn * nb == n6 and tn % 128 == 0
        in_specs += [
            pl.BlockSpec((8, D_MODEL), lambda b: (0, 0)),
            pl.BlockSpec((None, D_MODEL, tn), lambda b: (mod_layer, 0, b)),
            pl.BlockSpec((None, 1, tn), lambda b: (mod_layer, 0, b)),
        ]
        args += [cond8, w_mod, b_mod3]
        out_specs.append(pl.BlockSpec((8, tn), lambda b: (0, b)))
        out_shape.append(jax.ShapeDtypeStruct((8, n6), F32))
    return pl.pallas_call(
        functools.partial(_ssd_kernel, seq=seq, has_h0=has_h0, emit_state=emit_state, has_prev=has_prev,
                          side_mod=side_mod is not None),
        grid=(nb,),
        in_specs=in_specs,
        out_specs=out_specs,
        out_shape=out_shape,
        scratch_shapes=[
            pltpu.VMEM((seq // SSM_CHUNK, SSM_D_INNER, SSM_CHUNK), F32),
            pltpu.VMEM((seq, 512), BF16),
            pltpu.VMEM((seq, 128), F32),
            pltpu.VMEM((seq // SSM_CHUNK, 128, SSM_CHUNK), F32),
            pltpu.VMEM((seq // SSM_CHUNK, SSM_D_INNER, SSM_CHUNK), F32),
            pltpu.VMEM((2, 512, 128), F32),
        ],
        compiler_params=_params("arbitrary"),
        name="ssd",
    )(*args)


def _merge_kernel(x_ref, h_ref, g_ref, b0, b1, b2, b3, wg0, wg1, wg2, wg3, wb_ref, wo_ref, o_ref):
    n = pl.program_id(1)

    @pl.when(n == 0)
    def _():
        o_ref[...] = x_ref[...]

    h = h_ref[...]
    merged = None
    for k, (b_ref, wg_ref) in enumerate(((b0, wg0), (b1, wg1), (b2, wg2), (b3, wg3))):
        gate = _sigmoid(_dot_nt(h, wg_ref[0]))
        term = gate * _dot(b_ref[...], wb_ref[k])
        merged = term if merged is None else merged + term
    o_ref[...] += g_ref[...] * _dot(merged.astype(BF16), wo_ref[...])


def _merge(x, h, mod3, branches, wt, gate_row0, w_branch, w_out, layer, row_fn):
    t = x.shape[0]
    tm, tn = 512, 512
    assert t % tm == 0
    nn = D_MODEL // tn
    rf = lambda m: row_fn(m * tm)
    gate_spec = lambda k: pl.BlockSpec(
        (pl.Element(1), pl.Element(tn), pl.Element(D_MODEL)),
        lambda m, n: (0, pl.multiple_of(gate_row0 + k * D_MODEL + n * tn, 16), 0))
    return pl.pallas_call(
        _merge_kernel,
        grid=(t // tm, nn),
        in_specs=[
            pl.BlockSpec((tm, D_MODEL), lambda m, n: (m, 0)),
            pl.BlockSpec((tm, D_MODEL), lambda m, n: (m, 0)),
            _mod_spec(2, rf),
        ] + [pl.BlockSpec((tm, MIX_W), lambda m, n: (m, 0))] * 4
          + [gate_spec(k) for k in range(4)] + [
            pl.BlockSpec((None, N_BRANCH, MIX_W, tn), lambda m, n: (layer, 0, 0, n)),
            pl.BlockSpec((None, tn, D_MODEL), lambda m, n: (layer, n, 0)),
        ],
        out_specs=pl.BlockSpec((tm, D_MODEL), lambda m, n: (m, 0)),
        out_shape=jax.ShapeDtypeStruct((t, D_MODEL), F32),
        compiler_params=_params("arbitrary", "arbitrary"),
        name="merge",
    )(x, h, mod3, *branches, wt, wt, wt, wt, w_branch, w_out)


def _ffn_kernel(x_ref, sc_ref, sh_ref, g_ref, w1_ref, w2_ref, o_ref, h_scr):
    n = pl.program_id(1)

    @pl.when(n == 0)
    def _():
        _rms_mod_rows(x_ref, sc_ref, sh_ref, h_scr)
        o_ref[...] = x_ref[...]

    a = jnp.maximum(_dot(h_scr[...], w1_ref[...].astype(BF16)), 0.0)
    o_ref[...] += g_ref[...] * _dot((a * a).astype(BF16), w2_ref[...].astype(BF16))


def _ffn(x, mod3, w1, w2, layer, row_fn):
    t = x.shape[0]
    tm, tf = 1024, 512
    assert t % tm == 0
    rf = lambda m: row_fn(m * tm)
    return pl.pallas_call(
        _ffn_kernel,
        grid=(t // tm, D_FF // tf),
        in_specs=[
            pl.BlockSpec((tm, D_MODEL), lambda m, n: (m, 0)),
            _mod_spec(4, rf), _mod_spec(3, rf), _mod_spec(5, rf),
            pl.BlockSpec((None, D_MODEL, tf), lambda m, n: (layer, 0, n)),
            pl.BlockSpec((None, tf, D_MODEL), lambda m, n: (layer, n, 0)),
        ],
        out_specs=pl.BlockSpec((tm, D_MODEL), lambda m, n: (m, 0)),
        out_shape=jax.ShapeDtypeStruct((t, D_MODEL), F32),
        scratch_shapes=[pltpu.VMEM((tm, D_MODEL), BF16)],
        compiler_params=_params("arbitrary", "arbitrary"),
        name="ffn",
    )(x, mod3, mod3, mod3, w1, w2)


def _rope_tables(seq):
    half = 16
    freqs = ROPE_THETA ** (-jnp.arange(half, dtype=F32) / half)
    n_rows = seq // GRID_W
    rows = jnp.repeat(jnp.arange(n_rows), GRID_W).astype(F32)
    cols = jnp.tile(jnp.arange(GRID_W), n_rows).astype(F32)
    ar = rows[:, None] * freqs[None, :]
    ac = cols[:, None] * freqs[None, :]
    cos64 = jnp.concatenate([jnp.cos(ar), jnp.cos(ar), jnp.cos(ac), jnp.cos(ac)], axis=-1)
    sin64 = jnp.concatenate([-jnp.sin(ar), jnp.sin(ar), -jnp.sin(ac), jnp.sin(ac)], axis=-1)
    return (jnp.tile(cos64, (1, ATT_HEADS)), jnp.tile(sin64, (1, ATT_HEADS)),
            jnp.tile(cos64, (1, ATT_KV_HEADS)), jnp.tile(sin64, (1, ATT_KV_HEADS)))


def _layer_path(x, mod3, lw, row_fn, nb, seq, lat, s_prev=None, side_casts=(), side_mod=None):
    (wt, qg, kg, gsum, conv_w, conv_b, bias128, alog128, dskip, ssm_g, tri, gmlp_g, ws, bs_t,
     w_ff1, w_ff2, layer) = lw[:-1]
    bf = lw[-1]
    h, qkv, f_in, xbc, z, dt, o_gmlp, *side = _inproj(x, mod3, wt, gmlp_g, ws, bs_t, row_fn, side_casts)
    if bf is None:
        bf = (side[1], 0, side[2].reshape(DEPTH, N_BRANCH, MIX_W, D_MODEL), side[3].reshape(DEPTH, D_MODEL, D_MODEL))
    wg, gate_row0, w_branch, w_out = bf
    if lat is None:
        o_att, k_new, v_new = _attention_ctx(qkv, qg, kg, gsum, nb, seq)
        h0 = None
    else:
        rope_tabs, cache_k, cache_v, h0 = lat
        o_att = _attention_lat(qkv, qg, kg, gsum, rope_tabs, cache_k, cache_v, nb, seq)
        k_new = v_new = None
    o_fnet = _fourier(f_in, nb, seq)
    ssd_out = _ssd(xbc, z, dt, conv_w, conv_b, bias128, alog128, dskip, ssm_g, tri, h0, s_prev, nb, seq,
                   emit_state=lat is None, side_mod=side_mod)
    o_ssm = ssd_out[0]
    s_fin = ssd_out[1] if lat is None else None
    mod_next = ssd_out[-1] if side_mod is not None else None
    x = _merge(x, h, mod3, (o_att, o_fnet, o_ssm, o_gmlp), wg, gate_row0, w_branch, w_out, layer, row_fn)
    x = _ffn(x, mod3, w_ff1, w_ff2, layer, row_fn)
    return x, k_new, v_new, s_fin, side, bf, mod_next


def kernel(x_prompt, x_sample, c, cache_k, cache_v, state_ssm, c_ctx, w_mod, b_mod, w_in, q_norm_g, k_norm_g,
           conv_w, conv_b, a_log, dt_bias, d_skip, ssm_norm_g, gmlp_norm_g, w_spatial, b_spatial, w_branch,
           w_out, w_ff1, w_ff2):
    nb_ctx, seq_ctx, _ = x_prompt.shape
    nb_lat, seq_lat, _ = x_sample.shape
    assert seq_lat == LAT_LEN and 1 + nb_lat <= 8

    cond8 = jnp.zeros((8, D_MODEL), F32).at[0].set(c_ctx).at[1:1 + nb_lat].set(c)
    b_mod3 = b_mod.reshape(DEPTH, 1, 6 * D_MODEL)
    mod_l = _modulation(cond8, w_mod, b_mod3, 0)

    gsum = jnp.asarray(np.kron(np.eye(ATT_HEADS), np.ones((HEAD_DIM, HEAD_DIM))), BF16)
    tri = jnp.asarray(np.tril(np.ones((SSM_CHUNK, SSM_CHUNK))), BF16)
    rope_tabs = _rope_tables(seq_lat)

    xp = x_prompt.reshape(nb_ctx * seq_ctx, D_MODEL)
    xs = x_sample.reshape(nb_lat * seq_lat, D_MODEL)
    ctx_row = lambda tok: 0
    lat_row = lambda tok: 1 + tok // LAT_LEN

    w_in_t = jnp.swapaxes(w_in, 1, 2)
    wt = _cast_layer_bf16(w_in_t, 0, GATE_COL0)
    bf = None

    assert DEPTH == 2
    new_k, new_v, s_l = [], [], None
    for l in range(DEPTH):
        zeros80 = jnp.zeros((128 - 48,), F32)
        bias128 = jnp.concatenate([jnp.tile(dt_bias[l].reshape(16), 3), zeros80]).reshape(1, 128)
        alog128 = jnp.concatenate([jnp.zeros((16,), F32), jnp.tile(a_log[l].reshape(16), 2), zeros80]).reshape(1, 128)
        lw = (
            wt,
            jnp.tile(q_norm_g[l], ATT_HEADS).reshape(1, 512), jnp.tile(k_norm_g[l], ATT_KV_HEADS).reshape(1, 128),
            gsum, conv_w[l], conv_b[l].reshape(1, SSM_CONV_CH), bias128, alog128,
            d_skip[l], ssm_norm_g[l].reshape(1, SSM_D_INNER), tri,
            gmlp_norm_g[l].reshape(1, MIX_W), w_spatial[l].astype(BF16), b_spatial[l].T,
            w_ff1, w_ff2, l, bf,
        )
        mod3 = mod_l.reshape(8, 1, 6 * D_MODEL)
        side_mod = (cond8, w_mod, b_mod3, l + 1) if l + 1 < DEPTH else None
        side_casts = []
        if l == 0:
            side_casts = [(w_in_t, 1, 0, N_IN), (w_in_t, 0, GATE_COL0, N_BRANCH * D_MODEL),
                          (w_branch.reshape(-1, D_MODEL), None, 0, DEPTH * N_BRANCH * MIX_W),
                          (w_out.reshape(-1, D_MODEL), None, 0, DEPTH * D_MODEL)]
        xp, k_l, v_l, s_l, side, bf, mod_next = _layer_path(xp, mod3, lw, ctx_row, nb_ctx, seq_ctx, None, s_prev=s_l,
                                                            side_casts=side_casts, side_mod=side_mod)
        lw = lw[:-1] + (bf,)
        new_k.append(k_l.reshape(nb_ctx, seq_ctx, ATT_KV_HEADS, HEAD_DIM))
        new_v.append(v_l.reshape(nb_ctx, seq_ctx, ATT_KV_HEADS, HEAD_DIM))
        lat = (rope_tabs,
               cache_k[:, l].reshape(nb_lat, -1, ATT_KV_HEADS * HEAD_DIM),
               cache_v[:, l].reshape(nb_lat, -1, ATT_KV_HEADS * HEAD_DIM),
               state_ssm[:, l].reshape(nb_lat, 2, SSM_HEADS * SSM_HEAD_DIM, SSM_STATE))
        xs = _layer_path(xs, mod3, lw, lat_row, nb_lat, seq_lat, lat)[0]
        if l == 0:
            wt = side[0]
            bf = (wt, GATE_COL0) + bf[2:]
            mod_l = mod_next

    return (xp.reshape(nb_ctx, seq_ctx, D_MODEL), xs.reshape(nb_lat, seq_lat, D_MODEL),
            jnp.stack(new_k, axis=1), jnp.stack(new_v, axis=1),
            s_l.reshape(nb_ctx, DEPTH, 2, SSM_HEADS, SSM_HEAD_DIM, SSM_STATE))
```

```python
import functools
import math

import numpy as np
import jax
import jax.numpy as jnp
from jax import lax
from jax.experimental import pallas as pl
from jax.experimental.pallas import tpu as pltpu

F32 = jnp.float32
BF16 = jnp.bfloat16

D_MODEL = 2048
DEPTH = 2
GRID_W = 64
MIX_W = 512
N_BRANCH = 4
ATT_HEADS = 8
ATT_KV_HEADS = 2
HEAD_DIM = 64
ROPE_THETA = 10000.0
FNET_GW = 128
SSM_HEADS = 8
SSM_HEAD_DIM = 64
SSM_STATE = 128
SSM_CHUNK = 128
SSM_D_INNER = 512
SSM_CONV_CH = 1024
GMLP_CHUNK = 128
D_FF = 4 * D_MODEL
EPS = 1e-6
LAT_LEN = 1024

QKV_W = 768
COL_F = 768
COL_XBC = 1280
COL_Z = 2304
COL_DT = 2816
COL_UV = 2832
GATE_COL0 = 3856
N_IN = GATE_COL0 + N_BRANCH * D_MODEL
_IN_PIECES = ((0, QKV_W), (COL_F, COL_XBC), (COL_XBC, COL_Z), (COL_Z, COL_DT),
              (COL_DT, COL_DT + 128),
              (COL_UV, GATE_COL0))

VMEM_LIMIT = 56 * 1024 * 1024
NEG_BIG = -1e30


def _params(*sem):
    return pltpu.CompilerParams(dimension_semantics=sem, vmem_limit_bytes=VMEM_LIMIT)


def _dot(a, b):
    return jnp.dot(a, b, preferred_element_type=F32)


def _dot_nt(a, b):
    return lax.dot_general(a, b, (((1,), (1,)), ((), ())), preferred_element_type=F32)


def _dot_tn(a, b):
    return lax.dot_general(a, b, (((0,), (0,)), ((), ())), preferred_element_type=F32)


def _sigmoid(x):
    return 1.0 / (1.0 + jnp.exp(-x))


def _split2(x):
    hi = x.astype(BF16)
    lo = (x - hi.astype(F32)).astype(BF16)
    return hi, lo


def _rms_mod(x, sc, sh):
    ms = jnp.mean(x * x, axis=-1, keepdims=True)
    return x * lax.rsqrt(ms + EPS) * (1.0 + sc) + sh


def _rms_mod_rows(x_ref, sc_ref, sh_ref, dst_ref):
    slab = 16

    def body(i, carry):
        rows = pl.ds(pl.multiple_of(i * slab, slab), slab)
        dst_ref[rows, :] = _rms_mod(x_ref[rows, :], sc_ref[...], sh_ref[...]).astype(dst_ref.dtype)
        return carry

    lax.fori_loop(0, x_ref.shape[0] // slab, body, 0, unroll=16)


def _mod_kernel(cond_ref, w_ref, b_ref, o_ref):
    c = cond_ref[...]
    s = c * _sigmoid(c)
    o_ref[...] = _dot(s.astype(BF16), w_ref[...].astype(BF16)) + b_ref[...]


def _modulation(cond8, w_mod, b_mod3, layer):
    tn = 1024
    n6 = 6 * D_MODEL
    return pl.pallas_call(
        _mod_kernel,
        grid=(n6 // tn,),
        in_specs=[
            pl.BlockSpec((8, D_MODEL), lambda n: (0, 0)),
            pl.BlockSpec((None, D_MODEL, tn), lambda n: (layer, 0, n)),
            pl.BlockSpec((None, 1, tn), lambda n: (layer, 0, n)),
        ],
        out_specs=pl.BlockSpec((8, tn), lambda n: (0, n)),
        out_shape=jax.ShapeDtypeStruct((8, n6), F32),
        compiler_params=_params("arbitrary"),
        name="modulation",
    )(cond8, w_mod, b_mod3)


def _mod_spec(chunk, row_fn):
    return pl.BlockSpec((None, 1, D_MODEL), lambda m, *_: (row_fn(m), 0, chunk))


def _cast_kernel(w_ref, o_ref):
    o_ref[...] = w_ref[...].astype(o_ref.dtype)


def _cast_layer_bf16(w, layer, n_rows):
    rows = 1024
    return pl.pallas_call(
        _cast_kernel,
        grid=(pl.cdiv(n_rows, rows),),
        in_specs=[pl.BlockSpec((None, rows, D_MODEL), lambda r: (layer, r, 0))],
        out_specs=pl.BlockSpec((None, rows, D_MODEL), lambda r: (0, r, 0)),
        out_shape=jax.ShapeDtypeStruct((1, n_rows, D_MODEL), BF16),
        compiler_params=_params("arbitrary"),
        name="cast_layer_bf16",
    )(w)


def _gmlp_chunk(x, gain, ws_ref, bs_ref):
    c0 = math.sqrt(2.0 / math.pi)
    ge = x * (0.5 * (1.0 + jnp.tanh(c0 * (x + 0.044715 * (x * x * x)))))
    u = ge[:, :MIX_W]
    v = ge[:, MIX_W:]
    v = v * lax.rsqrt(jnp.mean(v * v, axis=-1, keepdims=True) + EPS) * gain
    outs = []
    for g in range(MIX_W // 128):
        gl = slice(g * 128, (g + 1) * 128)
        mixed = _dot(ws_ref[g], v[:, gl].astype(BF16)) + bs_ref[:, g:g + 1]
        outs.append(u[:, gl] * mixed)
    return jnp.concatenate(outs, axis=1)


def _inproj_kernel(*refs, n_side):
    x_ref, sc_ref, sh_ref, w_ref, gg_ref, ws_ref, bs_ref = refs[:7]
    side_in = refs[7:7 + n_side]
    h_ref = refs[7 + n_side]
    out_refs = refs[8 + n_side:len(refs) - n_side]
    side_out = refs[len(refs) - n_side:]
    for src, dst in zip(side_in, side_out):
        dst[...] = src[...].astype(dst.dtype)
    hb = _rms_mod(x_ref[...], sc_ref[...], sh_ref[...]).astype(BF16)
    h_ref[...] = hb
    o_gmlp = out_refs[-1]
    a, b = _IN_PIECES[-1]
    uv = _dot_nt(hb, w_ref[a:b, :])
    for c in range(x_ref.shape[0] // GMLP_CHUNK):
        sl = slice(c * GMLP_CHUNK, (c + 1) * GMLP_CHUNK)
        o_gmlp[sl, :] = _gmlp_chunk(uv[sl, :], gg_ref[...], ws_ref, bs_ref).astype(o_gmlp.dtype)
    for o_ref, (a, b) in zip(out_refs[:-1], _IN_PIECES[:-1]):
        o_ref[...] = _dot_nt(hb, w_ref[a:b, :])


def _inproj(x, mod3, wt, gmlp_g, ws, bs_t, row_fn, side_casts=()):
    t = x.shape[0]
    tm = 256
    assert t % tm == 0
    steps = t // tm
    widths = [b - a for a, b in _IN_PIECES[:-1]]
    const = lambda shape: pl.BlockSpec(shape, lambda m: (0,) * len(shape))
    in_specs = [
        pl.BlockSpec((tm, D_MODEL), lambda m: (m, 0)),
        _mod_spec(1, lambda m: row_fn(m * tm)),
        _mod_spec(0, lambda m: row_fn(m * tm)),
        pl.BlockSpec((None, GATE_COL0, D_MODEL), lambda m: (0, 0, 0), pipeline_mode=pl.Buffered(1)),
        const((1, MIX_W)), const((4, 128, 128)), const((128, 4)),
    ]
    args = [x, mod3, mod3, wt, gmlp_g, ws, bs_t]
    out_specs = ([pl.BlockSpec((tm, D_MODEL), lambda m: (m, 0))]
                 + [pl.BlockSpec((tm, w), lambda m: (m, 0)) for w in widths]
                 + [pl.BlockSpec((tm, MIX_W), lambda m: (m, 0))])
    out_shape = ([jax.ShapeDtypeStruct((t, D_MODEL), BF16)]
                 + [jax.ShapeDtypeStruct((t, w), F32) for w in widths]
                 + [jax.ShapeDtypeStruct((t, MIX_W), BF16)])
    side_in_specs, side_args = [], []
    for w_src, lead, row0, n_rows in side_casts:
        rows = -(-n_rows // (16 * steps)) * 16
        assert pl.cdiv(n_rows, rows) == steps
        if lead is None:
            side_in_specs.append(pl.BlockSpec((rows, D_MODEL), lambda m: (m, 0)))
            out_specs.append(pl.BlockSpec((rows, D_MODEL), lambda m: (m, 0)))
            out_shape.append(jax.ShapeDtypeStruct((n_rows, D_MODEL), BF16))
        elif row0 == 0:
            side_in_specs.append(pl.BlockSpec((None, rows, D_MODEL), lambda m, lead=lead: (lead, m, 0)))
            out_specs.append(pl.BlockSpec((None, rows, D_MODEL), lambda m: (0, m, 0)))
            out_shape.append(jax.ShapeDtypeStruct((1, n_rows, D_MODEL), BF16))
        else:
            assert n_rows == rows * steps and row0 % 16 == 0
            side_in_specs.append(pl.BlockSpec(
                (pl.Element(1), pl.Element(rows), pl.Element(D_MODEL)),
                lambda m, lead=lead, row0=row0, rows=rows: (lead, pl.multiple_of(row0 + m * rows, 16), 0)))
            out_specs.append(pl.BlockSpec((1, rows, D_MODEL), lambda m: (0, m, 0)))
            out_shape.append(jax.ShapeDtypeStruct((1, n_rows, D_MODEL), BF16))
        side_args.append(w_src)
    in_specs += side_in_specs
    args += side_args
    return pl.pallas_call(
        functools.partial(_inproj_kernel, n_side=len(side_casts)),
        grid=(steps,),
        in_specs=in_specs,
        out_specs=out_specs,
        out_shape=out_shape,
        compiler_params=_params("arbitrary"),
        name="inproj",
    )(*args)


def _head_rms(t, gsum, gain):
    t2 = t * t
    hi, lo = _split2(t2)
    ss = _dot(hi, gsum) + _dot(lo, gsum)
    return t * lax.rsqrt(ss * (1.0 / HEAD_DIM) + EPS) * gain


def _rope(t, cos, sin_signed):
    w = t.shape[-1]
    up = pltpu.roll(t, w - 16, 1)
    dn = pltpu.roll(t, 16, 1)
    lane = lax.broadcasted_iota(jnp.int32, t.shape, 1)
    partner = jnp.where((lane & 31) < 16, up, dn)
    return t * cos + partner * sin_signed


ATT_GROUP = ATT_HEADS // ATT_KV_HEADS
GROUP_W = ATT_GROUP * HEAD_DIM


def _store_group_keys(k, kbd_scr, row0):
    s = k.shape[0]
    k2 = jnp.concatenate([k, k], axis=1)
    k2r = pltpu.roll(k2, HEAD_DIM, 1)
    lane = lax.broadcasted_iota(jnp.int32, k2.shape, 1)
    first = (lane & (2 * HEAD_DIM - 1)) < HEAD_DIM
    for j, tiled in enumerate((jnp.where(first, k2, k2r), jnp.where(first, k2r, k2))):
        for hh in range(ATT_GROUP):
            slab = jnp.where(lax.shift_right_logical(lane, 6) == hh, tiled, 0.0)
            kbd_scr[j, hh, row0:row0 + s, :] = slab.astype(kbd_scr.dtype)


def _attend_heads(q, kbd_scr, vt_scr, o_ref):
    qb = (q * (HEAD_DIM ** -0.5 * math.log2(math.e))).astype(BF16)
    s_len = kbd_scr.shape[2]
    for j in range(ATT_KV_HEADS):
        kbd = kbd_scr[j].reshape(ATT_GROUP * s_len, GROUP_W)
        s_all = _dot_nt(kbd, qb[:, j * GROUP_W:(j + 1) * GROUP_W])
        outs = []
        for hh in range(ATT_GROUP):
            s = s_all[hh * s_len:(hh + 1) * s_len, :]
            m = jnp.max(s, axis=0, keepdims=True)
            e = jnp.exp2(s - m)
            den = jnp.sum(e, axis=0, keepdims=True)
            outs.append(_dot(vt_scr[j * HEAD_DIM:(j + 1) * HEAD_DIM, :], e.astype(BF16)) / den)
        o_ref[:, j * GROUP_W:(j + 1) * GROUP_W] = jnp.concatenate(outs, axis=0).T.astype(o_ref.dtype)


def _attn_ctx_kernel(q_ref, kv_ref, qg_ref, kg_ref, gsum_ref, o_ref, kout_ref, vout_ref, kbd_scr, vt_scr):
    gsum = gsum_ref[...]
    seq = vt_scr.shape[-1]
    for i in range(vt_scr.shape[0]):
        rows = pl.ds(i * seq, seq)
        kv = kv_ref[rows, :]
        k = _head_rms(kv[:, :128], gsum[:128, :128], kg_ref[...])
        v = kv[:, 128:]
        kout_ref[rows, :] = k
        vout_ref[rows, :] = v
        _store_group_keys(k, kbd_scr.at[i], 0)
        vt_scr[i] = v.T.astype(BF16)
        q = _head_rms(q_ref[rows, :], gsum, qg_ref[...])
        _attend_heads(q, kbd_scr.at[i], vt_scr.at[i], o_ref.at[rows])


def _attn_lat_kernel(q_ref, kv_ref, qg_ref, kg_ref, gsum_ref, cq_ref, sq_ref, ck_ref, sk_ref,
                     cache_k_ref, cache_v_ref, o_ref, kbd_scr, vt_scr):
    gsum = gsum_ref[...]
    past = cache_k_ref.shape[0]

    @pl.when(pl.program_id(1) == 0)
    def _():
        kv = kv_ref[...]
        k = _head_rms(kv[:, :128], gsum[:128, :128], kg_ref[...])
        k = _rope(k, ck_ref[...], sk_ref[...])
        _store_group_keys(cache_k_ref[...], kbd_scr, 0)
        _store_group_keys(k, kbd_scr, past)
        vt_scr[:, :past] = cache_v_ref[...].T.astype(BF16)
        vt_scr[:, past:] = kv[:, 128:].T.astype(BF16)

    q = _head_rms(q_ref[...], gsum, qg_ref[...])
    q = _rope(q, cq_ref[...], sq_ref[...])
    _attend_heads(q, kbd_scr, vt_scr, o_ref)


def _attention_ctx(qkv, qg, kg, gsum, nb, seq):
    t = qkv.shape[0]
    per = 8 if nb % 8 == 0 else (4 if nb % 4 == 0 else 1)
    rows = per * seq
    const = lambda shape: pl.BlockSpec(shape, lambda b: (0, 0))
    return pl.pallas_call(
        _attn_ctx_kernel,
        grid=(nb // per,),
        in_specs=[
            pl.BlockSpec((rows, 512), lambda b: (b, 0)),
            pl.BlockSpec((rows, 256), lambda b: (b, 2)),
            const((1, 512)), const((1, 128)), const((512, 512)),
        ],
        out_specs=[
            pl.BlockSpec((rows, 512), lambda b: (b, 0)),
            pl.BlockSpec((rows, 128), lambda b: (b, 0)),
            pl.BlockSpec((rows, 128), lambda b: (b, 0)),
        ],
        out_shape=[
            jax.ShapeDtypeStruct((t, 512), BF16),
            jax.ShapeDtypeStruct((t, 128), F32),
            jax.ShapeDtypeStruct((t, 128), F32),
        ],
        scratch_shapes=[pltpu.VMEM((per, ATT_KV_HEADS, ATT_GROUP, seq, GROUP_W), BF16),
                        pltpu.VMEM((per, 128, seq), BF16)],
        compiler_params=_params("arbitrary"),
        name="attn_ctx",
    )(qkv, qkv, qg, kg, gsum)


def _attention_lat(qkv, qg, kg, gsum, rope_tabs, cache_k, cache_v, nb, seq):
    t = qkv.shape[0]
    qblk = 1024
    nq = seq // qblk
    past = cache_k.shape[1]
    cq, sq, ck, sk = rope_tabs
    const = lambda shape: pl.BlockSpec(shape, lambda b, i: (0, 0))
    return pl.pallas_call(
        _attn_lat_kernel,
        grid=(nb, nq),
        in_specs=[
            pl.BlockSpec((qblk, 512), lambda b, i: (b * nq + i, 0)),
            pl.BlockSpec((seq, 256), lambda b, i: (b, 2)),
            const((1, 512)), const((1, 128)), const((512, 512)),
            pl.BlockSpec((qblk, 512), lambda b, i: (i, 0)),
            pl.BlockSpec((qblk, 512), lambda b, i: (i, 0)),
            const((seq, 128)), const((seq, 128)),
            pl.BlockSpec((None, past, 128), lambda b, i: (b, 0, 0)),
            pl.BlockSpec((None, past, 128), lambda b, i: (b, 0, 0)),
        ],
        out_specs=pl.BlockSpec((qblk, 512), lambda b, i: (b * nq + i, 0)),
        out_shape=jax.ShapeDtypeStruct((t, 512), BF16),
        scratch_shapes=[pltpu.VMEM((ATT_KV_HEADS, ATT_GROUP, past + seq, GROUP_W), BF16),
                        pltpu.VMEM((128, past + seq), BF16)],
        compiler_params=_params("arbitrary", "arbitrary"),
        name="attn_lat",
    )(qkv, qkv, qg, kg, gsum, cq, sq, ck, sk, cache_k, cache_v)


def _fourier_kernel(f_ref, bdh_ref, bdl_ref, cl_ref, sl_ref, o_ref, *, scale):
    bdh = bdh_ref[...]
    bdl = bdl_ref[...]
    cl = cl_ref[...].astype(BF16)
    sl = sl_ref[...].astype(BF16)
    seq = cl_ref.shape[0]
    for i in range(f_ref.shape[0] // seq):
        rows = pl.ds(i * seq, seq)
        xh, xl = _split2(f_ref[rows, :])
        yc, ys = [], []
        for g in range(MIX_W // FNET_GW):
            gl = slice(g * FNET_GW, (g + 1) * FNET_GW)
            y = _dot(xh[:, gl], bdh) + _dot(xl[:, gl], bdh) + _dot(xh[:, gl], bdl)
            yc.append(y[:, :FNET_GW])
            ys.append(y[:, FNET_GW:])
        re = _dot(cl, jnp.concatenate(yc, axis=1).astype(BF16))
        im = _dot(sl, jnp.concatenate(ys, axis=1).astype(BF16))
        o_ref[rows, :] = ((re - im) * scale).astype(o_ref.dtype)


def _dft_consts(seq):
    def cs(n):
        k = np.arange(n, dtype=np.int64)
        ang = 2.0 * np.pi * ((k[:, None] * k[None, :]) % n).astype(np.float64) / n
        return np.cos(ang), np.sin(ang)

    bd = np.concatenate(cs(FNET_GW), axis=1)
    cl, sl_ = cs(seq)

    def hl(a):
        a32 = jnp.asarray(a, F32)
        hi = a32.astype(BF16)
        return hi, (a32 - hi.astype(F32)).astype(BF16)

    return hl(bd) + (jnp.asarray(cl, F32), jnp.asarray(sl_, F32))


def _fourier(f, nb, seq):
    t = f.shape[0]
    consts = _dft_consts(seq)
    const = lambda a: pl.BlockSpec(a.shape, lambda b: (0, 0))
    per = 1
    if seq <= 256:
        per = 8 if nb % 8 == 0 else (4 if nb % 4 == 0 else 1)
    return pl.pallas_call(
        functools.partial(_fourier_kernel, scale=1.0 / math.sqrt(seq * FNET_GW)),
        grid=(nb // per,),
        in_specs=[pl.BlockSpec((per * seq, MIX_W), lambda b: (b, 0))] + [const(a) for a in consts],
        out_specs=pl.BlockSpec((per * seq, MIX_W), lambda b: (b, 0)),
        out_shape=jax.ShapeDtypeStruct((t, MIX_W), BF16),
        compiler_params=_params("arbitrary"),
        name="fourier",
    )(f, *consts)


def _ssd_kernel(*refs, seq, has_h0, emit_state, has_prev, side_mod):
    (xbc_ref, z_ref, dt_ref, cw_ref, cb_ref, bias_ref, alog_ref, dskip_ref, ng_ref, tri_ref) = refs[:10]
    rest = list(refs[10:])
    h0_ref = rest.pop(0) if has_h0 else None
    sprev_ref = rest.pop(0) if has_prev else None
    mod_in = [rest.pop(0) for _ in range(3)] if side_mod else None
    o_ref = rest.pop(0)
    sfin_ref = rest.pop(0) if emit_state else None
    if side_mod:
        cond_ref, wmod_ref, bmod_ref = mod_in
        _mod_kernel(cond_ref, wmod_ref, bmod_ref, rest.pop(0))
    xt_scr, bc_scr, tab, tabt, yt_scr, s_scr = rest
    nc = seq // SSM_CHUNK
    q = SSM_CHUNK

    lane = lax.broadcasted_iota(jnp.int32, (q, 128), 1)
    ii = lax.broadcasted_iota(jnp.int32, (q, q), 0)
    jj = lax.broadcasted_iota(jnp.int32, (q, q), 1)
    below = ii > jj
    above = ii < jj
    row = lax.broadcasted_iota(jnp.int32, (q, 1), 0)
    cw = cw_ref[...]
    amul = jnp.where(lane < 16, 1.0, jnp.where(lane < 48, -jnp.exp(alog_ref[...]), 0.0))
    tri = tri_ref[...]

    if has_h0:
        s_scr[...] = h0_ref[...]
    else:
        s_scr[...] = jnp.zeros_like(s_scr)

    def prep(c, carry):
        r0 = pl.multiple_of(c * q, q)
        cur = xbc_ref[pl.ds(r0, q), :]
        prev8 = xbc_ref[pl.ds(pl.multiple_of(jnp.maximum(r0 - 8, 0), 8), 8), :]
        next8 = xbc_ref[pl.ds(pl.multiple_of(jnp.minimum(r0 + q, seq - 8), 8), 8), :]
        prev_row = jnp.where(c > 0, prev8[7:8, :], 0.0)
        next_row = jnp.where(c < nc - 1, next8[0:1, :], 0.0)
        up = jnp.where(row == 0, prev_row, pltpu.roll(cur, 1, 0))
        dn = jnp.where(row == q - 1, next_row, pltpu.roll(cur, q - 1, 0))
        conv = up * cw[0:1, :] + cur * cw[1:2, :] + dn * cw[2:3, :] + cb_ref[...]
        act = conv * _sigmoid(conv)
        xt_scr[c] = act[:, :SSM_D_INNER].T
        bc_scr[pl.ds(r0, q), :] = act[:, SSM_D_INNER:].astype(BF16)

        raw = dt_ref[pl.ds(r0, q), :]
        rep = jnp.where(lane < 16, raw, jnp.where(lane < 32, pltpu.roll(raw, 16, 1), pltpu.roll(raw, 32, 1)))
        v = rep + bias_ref[...]
        sp = jnp.maximum(v, 0.0) + jnp.log1p(jnp.exp(-jnp.abs(v)))
        t = sp * amul
        hi = t.astype(BF16)
        r1 = t - hi.astype(F32)
        mid = r1.astype(BF16)
        lo = (r1 - mid.astype(F32)).astype(BF16)
        cum = _dot(tri, hi) + _dot(tri, mid) + _dot(tri, lo)
        tc = jnp.where(lane < 32, t, cum)
        tab[pl.ds(r0, q), :] = tc
        tabt[c] = tc.T
        return carry

    lax.fori_loop(0, nc, prep, 0, unroll=nc)

    def pass_fwd(c, carry):
        r0 = pl.multiple_of(c * q, q)
        pc = tab[pl.ds(r0, q), :]
        pt = tabt[c]
        for g in range(2):
            bg = bc_scr[pl.ds(r0, q), 128 * g:128 * (g + 1)]
            cg = bc_scr[pl.ds(r0, q), 256 + 128 * g:256 + 128 * (g + 1)]
            cb = _dot_nt(cg, bg)
            y_in = _dot_nt(s_scr[0, 256 * g:256 * (g + 1), :].astype(BF16), cg)
            for hh in range(4):
                h = 4 * g + hh
                acf_row = pt[32 + h:33 + h, :]
                eb_col = pc[:, 40 + h:41 + h] - pc[:, 24 + h:25 + h]
                eb_row = pt[40 + h:41 + h, :] - pt[24 + h:25 + h, :]
                dtf_row = pt[h:h + 1, :]
                dtb_row = pt[8 + h:9 + h, :]
                seg = jnp.where(below, pc[:, 32 + h:33 + h] - acf_row, jnp.where(above, eb_row - eb_col, 0.0))
                wgt = jnp.where(below, dtf_row, jnp.where(above, dtb_row, dtf_row + dtb_row))
                mp = cb * (jnp.exp(seg) * wgt)
                xth = xt_scr[c, 64 * h:64 * (h + 1), :]
                yd = _dot_nt(xth.astype(BF16), mp.astype(BF16))
                yo = y_in[64 * hh:64 * (hh + 1), :] * jnp.exp(acf_row)
                yt_scr[c, 64 * h:64 * (h + 1), :] = yd + yo + dskip_ref[h] * xth
                tot = pt[32 + h:33 + h, q - 1:q]
                wf_row = dtf_row * jnp.exp(tot - acf_row)
                st = _dot((xth * wf_row).astype(BF16), bg)
                s_scr[0, 64 * h:64 * (h + 1), :] = s_scr[0, 64 * h:64 * (h + 1), :] * jnp.exp(tot) + st
        return carry

    lax.fori_loop(0, nc, pass_fwd, 0, unroll=nc)

    def pass_bwd(ci, carry):
        c = nc - 1 - ci
        r0 = pl.multiple_of(c * q, q)
        pt = tabt[c]
        for g in range(2):
            bg = bc_scr[pl.ds(r0, q), 128 * g:128 * (g + 1)]
            cg = bc_scr[pl.ds(r0, q), 256 + 128 * g:256 + 128 * (g + 1)]
            y_in = _dot_nt(s_scr[1, 256 * g:256 * (g + 1), :].astype(BF16), cg)
            for hh in range(4):
                h = 4 * g + hh
                eb_row = pt[40 + h:41 + h, :] - pt[24 + h:25 + h, :]
                tot = pt[40 + h:41 + h, q - 1:q]
                xth = xt_scr[c, 64 * h:64 * (h + 1), :]
                yt_scr[c, 64 * h:64 * (h + 1), :] += y_in[64 * hh:64 * (hh + 1), :] * jnp.exp(tot - eb_row)
                wb_row = pt[8 + h:9 + h, :] * jnp.exp(eb_row)
                st = _dot((xth * wb_row).astype(BF16), bg)
                s_scr[1, 64 * h:64 * (h + 1), :] = s_scr[1, 64 * h:64 * (h + 1), :] * jnp.exp(tot) + st
        zc = z_ref[pl.ds(r0, q), :]
        yg = yt_scr[c].T * (zc * _sigmoid(zc))
        ms = jnp.mean(yg * yg, axis=-1, keepdims=True)
        o_ref[pl.ds(r0, q), :] = (yg * lax.rsqrt(ms + EPS) * ng_ref[...]).astype(o_ref.dtype)
        return carry

    lax.fori_loop(0, nc, pass_bwd, 0, unroll=nc)

    if emit_state and has_prev:
        sfin_ref[0] = sprev_ref[...]
        sfin_ref[1] = s_scr[...]
    elif emit_state:
        sfin_ref[...] = s_scr[...]


def _ssd(xbc, z, dt, conv_w, conv_b, bias128, alog128, dskip, norm_g, tri, h0, s_prev, nb, seq, emit_state,
         side_mod=None):
    t = xbc.shape[0]
    has_h0 = h0 is not None
    has_prev = s_prev is not None
    const = lambda shape: pl.BlockSpec(shape, lambda b: (0,) * len(shape))
    in_specs = [
        pl.BlockSpec((seq, SSM_CONV_CH), lambda b: (b, 0)),
        pl.BlockSpec((seq, SSM_D_INNER), lambda b: (b, 0)),
        pl.BlockSpec((seq, 128), lambda b: (b, 0)),
        const((3, SSM_CONV_CH)), const((1, SSM_CONV_CH)), const((1, 128)), const((1, 128)),
        pl.BlockSpec(memory_space=pltpu.SMEM), const((1, SSM_D_INNER)), const((128, 128)),
    ]
    args = [xbc, z, dt, conv_w, conv_b, bias128, alog128, dskip, norm_g, tri]
    if has_h0:
        in_specs.append(pl.BlockSpec((None, 2, 512, 128), lambda b: (b, 0, 0, 0)))
        args.append(h0)
    if has_prev:
        in_specs.append(pl.BlockSpec((None, 2, 512, 128), lambda b: (b, 0, 0, 0)))
        args.append(s_prev)
    out_specs = [pl.BlockSpec((seq, SSM_D_INNER), lambda b: (b, 0))]
    out_shape = [jax.ShapeDtypeStruct((t, SSM_D_INNER), BF16)]
    if emit_state and has_prev:
        out_specs.append(pl.BlockSpec((None, 2, 2, 512, 128), lambda b: (b, 0, 0, 0, 0)))
        out_shape.append(jax.ShapeDtypeStruct((nb, 2, 2, 512, 128), F32))
    elif emit_state:
        out_specs.append(pl.BlockSpec((None, 2, 512, 128), lambda b: (b, 0, 0, 0)))
        out_shape.append(jax.ShapeDtypeStruct((nb, 2, 512, 128), F32))
    if side_mod is not None:
        cond8, w_mod, b_mod3, mod_layer = side_mod
        n6 = 6 * D_MODEL
        tn = n6 // nb
        assert tn * nb == n6 and tn % 128 == 0
        in_specs += [
            pl.BlockSpec((8, D_MODEL), lambda b: (0, 0)),
            pl.BlockSpec((None, D_MODEL, tn), lambda b: (mod_layer, 0, b)),
            pl.BlockSpec((None, 1, tn), lambda b: (mod_layer, 0, b)),
        ]
        args += [cond8, w_mod, b_mod3]
        out_specs.append(pl.BlockSpec((8, tn), lambda b: (0, b)))
        out_shape.append(jax.ShapeDtypeStruct((8, n6), F32))
    return pl.pallas_call(
        functools.partial(_ssd_kernel, seq=seq, has_h0=has_h0, emit_state=emit_state, has_prev=has_prev,
                          side_mod=side_mod is not None),
        grid=(nb,),
        in_specs=in_specs,
        out_specs=out_specs,
        out_shape=out_shape,
        scratch_shapes=[
            pltpu.VMEM((seq // SSM_CHUNK, SSM_D_INNER, SSM_CHUNK), F32),
            pltpu.VMEM((seq, 512), BF16),
            pltpu.VMEM((seq, 128), F32),
            pltpu.VMEM((seq // SSM_CHUNK, 128, SSM_CHUNK), F32),
            pltpu.VMEM((seq // SSM_CHUNK, SSM_D_INNER, SSM_CHUNK), F32),
            pltpu.VMEM((2, 512, 128), F32),
        ],
        compiler_params=_params("arbitrary"),
        name="ssd",
    )(*args)


def _merge_kernel(x_ref, h_ref, g_ref, b0, b1, b2, b3, wg0, wg1, wg2, wg3, wb_ref, wo_ref, o_ref):
    n = pl.program_id(1)

    @pl.when(n == 0)
    def _():
        o_ref[...] = x_ref[...]

    h = h_ref[...]
    merged = None
    for k, (b_ref, wg_ref) in enumerate(((b0, wg0), (b1, wg1), (b2, wg2), (b3, wg3))):
        gate = _sigmoid(_dot_nt(h, wg_ref[0]))
        term = gate * _dot(b_ref[...], wb_ref[k])
        merged = term if merged is None else merged + term
    o_ref[...] += g_ref[...] * _dot(merged.astype(BF16), wo_ref[...])


def _merge(x, h, mod3, branches, wt, gate_row0, w_branch, w_out, layer, row_fn):
    t = x.shape[0]
    tm, tn = 512, 512
    assert t % tm == 0
    nn = D_MODEL // tn
    rf = lambda m: row_fn(m * tm)
    gate_spec = lambda k: pl.BlockSpec(
        (pl.Element(1), pl.Element(tn), pl.Element(D_MODEL)),
        lambda m, n: (0, pl.multiple_of(gate_row0 + k * D_MODEL + n * tn, 16), 0))
    return pl.pallas_call(
        _merge_kernel,
        grid=(t // tm, nn),
        in_specs=[
            pl.BlockSpec((tm, D_MODEL), lambda m, n: (m, 0)),
            pl.BlockSpec((tm, D_MODEL), lambda m, n: (m, 0)),
            _mod_spec(2, rf),
        ] + [pl.BlockSpec((tm, MIX_W), lambda m, n: (m, 0))] * 4
          + [gate_spec(k) for k in range(4)] + [
            pl.BlockSpec((None, N_BRANCH, MIX_W, tn), lambda m, n: (layer, 0, 0, n)),
            pl.BlockSpec((None, tn, D_MODEL), lambda m, n: (layer, n, 0)),
        ],
        out_specs=pl.BlockSpec((tm, D_MODEL), lambda m, n: (m, 0)),
        out_shape=jax.ShapeDtypeStruct((t, D_MODEL), F32),
        compiler_params=_params("arbitrary", "arbitrary"),
        name="merge",
    )(x, h, mod3, *branches, wt, wt, wt, wt, w_branch, w_out)


def _ffn_kernel(x_ref, sc_ref, sh_ref, g_ref, w1_ref, w2_ref, o_ref, h_scr):
    n = pl.program_id(1)

    @pl.when(n == 0)
    def _():
        _rms_mod_rows(x_ref, sc_ref, sh_ref, h_scr)
        o_ref[...] = x_ref[...]

    a = jnp.maximum(_dot(h_scr[...], w1_ref[...].astype(BF16)), 0.0)
    o_ref[...] += g_ref[...] * _dot((a * a).astype(BF16), w2_ref[...].astype(BF16))


def _ffn(x, mod3, w1, w2, layer, row_fn):
    t = x.shape[0]
    tm, tf = 1024, 512
    assert t % tm == 0
    rf = lambda m: row_fn(m * tm)
    return pl.pallas_call(
        _ffn_kernel,
        grid=(t // tm, D_FF // tf),
        in_specs=[
            pl.BlockSpec((tm, D_MODEL), lambda m, n: (m, 0)),
            _mod_spec(4, rf), _mod_spec(3, rf), _mod_spec(5, rf),
            pl.BlockSpec((None, D_MODEL, tf), lambda m, n: (layer, 0, n)),
            pl.BlockSpec((None, tf, D_MODEL), lambda m, n: (layer, n, 0)),
        ],
        out_specs=pl.BlockSpec((tm, D_MODEL), lambda m, n: (m, 0)),
        out_shape=jax.ShapeDtypeStruct((t, D_MODEL), F32),
        scratch_shapes=[pltpu.VMEM((tm, D_MODEL), BF16)],
        compiler_params=_params("arbitrary", "arbitrary"),
        name="ffn",
    )(x, mod3, mod3, mod3, w1, w2)


def _rope_tables(seq):
    half = 16
    freqs = ROPE_THETA ** (-jnp.arange(half, dtype=F32) / half)
    n_rows = seq // GRID_W
    rows = jnp.repeat(jnp.arange(n_rows), GRID_W).astype(F32)
    cols = jnp.tile(jnp.arange(GRID_W), n_rows).astype(F32)
    ar = rows[:, None] * freqs[None, :]
    ac = cols[:, None] * freqs[None, :]
    cos64 = jnp.concatenate([jnp.cos(ar), jnp.cos(ar), jnp.cos(ac), jnp.cos(ac)], axis=-1)
    sin64 = jnp.concatenate([-jnp.sin(ar), jnp.sin(ar), -jnp.sin(ac), jnp.sin(ac)], axis=-1)
    return (jnp.tile(cos64, (1, ATT_HEADS)), jnp.tile(sin64, (1, ATT_HEADS)),
            jnp.tile(cos64, (1, ATT_KV_HEADS)), jnp.tile(sin64, (1, ATT_KV_HEADS)))


def _layer_path(x, mod3, lw, row_fn, nb, seq, lat, s_prev=None, side_casts=(), side_mod=None):
    (wt, qg, kg, gsum, conv_w, conv_b, bias128, alog128, dskip, ssm_g, tri, gmlp_g, ws, bs_t,
     w_ff1, w_ff2, layer) = lw[:-1]
    bf = lw[-1]
    h, qkv, f_in, xbc, z, dt, o_gmlp, *side = _inproj(x, mod3, wt, gmlp_g, ws, bs_t, row_fn, side_casts)
    if bf is None:
        bf = (side[1], 0, side[2].reshape(DEPTH, N_BRANCH, MIX_W, D_MODEL), side[3].reshape(DEPTH, D_MODEL, D_MODEL))
    wg, gate_row0, w_branch, w_out = bf
    if lat is None:
        o_att, k_new, v_new = _attention_ctx(qkv, qg, kg, gsum, nb, seq)
        h0 = None
    else:
        rope_tabs, cache_k, cache_v, h0 = lat
        o_att = _attention_lat(qkv, qg, kg, gsum, rope_tabs, cache_k, cache_v, nb, seq)
        k_new = v_new = None
    o_fnet = _fourier(f_in, nb, seq)
    ssd_out = _ssd(xbc, z, dt, conv_w, conv_b, bias128, alog128, dskip, ssm_g, tri, h0, s_prev, nb, seq,
                   emit_state=lat is None, side_mod=side_mod)
    o_ssm = ssd_out[0]
    s_fin = ssd_out[1] if lat is None else None
    mod_next = ssd_out[-1] if side_mod is not None else None
    x = _merge(x, h, mod3, (o_att, o_fnet, o_ssm, o_gmlp), wg, gate_row0, w_branch, w_out, layer, row_fn)
    x = _ffn(x, mod3, w_ff1, w_ff2, layer, row_fn)
    return x, k_new, v_new, s_fin, side, bf, mod_next


def kernel(x_prompt, x_sample, c, cache_k, cache_v, state_ssm, c_ctx, w_mod, b_mod, w_in, q_norm_g, k_norm_g,
           conv_w, conv_b, a_log, dt_bias, d_skip, ssm_norm_g, gmlp_norm_g, w_spatial, b_spatial, w_branch,
           w_out, w_ff1, w_ff2):
    nb_ctx, seq_ctx, _ = x_prompt.shape
    nb_lat, seq_lat, _ = x_sample.shape
    assert seq_lat == LAT_LEN and 1 + nb_lat <= 8

    cond8 = jnp.zeros((8, D_MODEL), F32).at[0].set(c_ctx).at[1:1 + nb_lat].set(c)
    b_mod3 = b_mod.reshape(DEPTH, 1, 6 * D_MODEL)
    mod_l = _modulation(cond8, w_mod, b_mod3, 0)

    gsum = jnp.asarray(np.kron(np.eye(ATT_HEADS), np.ones((HEAD_DIM, HEAD_DIM))), BF16)
    tri = jnp.asarray(np.tril(np.ones((SSM_CHUNK, SSM_CHUNK))), BF16)
    rope_tabs = _rope_tables(seq_lat)

    xp = x_prompt.reshape(nb_ctx * seq_ctx, D_MODEL)
    xs = x_sample.reshape(nb_lat * seq_lat, D_MODEL)
    ctx_row = lambda tok: 0
    lat_row = lambda tok: 1 + tok // LAT_LEN

    w_in_t = jnp.swapaxes(w_in, 1, 2)
    wt = _cast_layer_bf16(w_in_t, 0, GATE_COL0)
    bf = None

    assert DEPTH == 2
    new_k, new_v, s_l = [], [], None
    for l in range(DEPTH):
        zeros80 = jnp.zeros((128 - 48,), F32)
        bias128 = jnp.concatenate([jnp.tile(dt_bias[l].reshape(16), 3), zeros80]).reshape(1, 128)
        alog128 = jnp.concatenate([jnp.zeros((16,), F32), jnp.tile(a_log[l].reshape(16), 2), zeros80]).reshape(1, 128)
        lw = (
            wt,
            jnp.tile(q_norm_g[l], ATT_HEADS).reshape(1, 512), jnp.tile(k_norm_g[l], ATT_KV_HEADS).reshape(1, 128),
            gsum, conv_w[l], conv_b[l].reshape(1, SSM_CONV_CH), bias128, alog128,
            d_skip[l], ssm_norm_g[l].reshape(1, SSM_D_INNER), tri,
            gmlp_norm_g[l].reshape(1, MIX_W), w_spatial[l].astype(BF16), b_spatial[l].T,
            w_ff1, w_ff2, l, bf,
        )
        mod3 = mod_l.reshape(8, 1, 6 * D_MODEL)
        side_mod = (cond8, w_mod, b_mod3, l + 1) if l + 1 < DEPTH else None
        side_casts = []
        if l == 0:
            side_casts = [(w_in_t, 1, 0, N_IN), (w_in_t, 0, GATE_COL0, N_BRANCH * D_MODEL),
                          (w_branch.reshape(-1, D_MODEL), None, 0, DEPTH * N_BRANCH * MIX_W),
                          (w_out.reshape(-1, D_MODEL), None, 0, DEPTH * D_MODEL)]
        xp, k_l, v_l, s_l, side, bf, mod_next = _layer_path(xp, mod3, lw, ctx_row, nb_ctx, seq_ctx, None, s_prev=s_l,
                                                            side_casts=side_casts, side_mod=side_mod)
        lw = lw[:-1] + (bf,)
        new_k.append(k_l.reshape(nb_ctx, seq_ctx, ATT_KV_HEADS, HEAD_DIM))
        new_v.append(v_l.reshape(nb_ctx, seq_ctx, ATT_KV_HEADS, HEAD_DIM))
        lat = (rope_tabs,
               cache_k[:, l].reshape(nb_lat, -1, ATT_KV_HEADS * HEAD_DIM),
               cache_v[:, l].reshape(nb_lat, -1, ATT_KV_HEADS * HEAD_DIM),
               state_ssm[:, l].reshape(nb_lat, 2, SSM_HEADS * SSM_HEAD_DIM, SSM_STATE))
        xs = _layer_path(xs, mod3, lw, lat_row, nb_lat, seq_lat, lat)[0]
        if l == 0:
            wt = side[0]
            bf = (wt, GATE_COL0) + bf[2:]
            mod_l = mod_next

    return (xp.reshape(nb_ctx, seq_ctx, D_MODEL), xs.reshape(nb_lat, seq_lat, D_MODEL),
            jnp.stack(new_k, axis=1), jnp.stack(new_v, axis=1),
            s_l.reshape(nb_ctx, DEPTH, 2, SSM_HEADS, SSM_HEAD_DIM, SSM_STATE))
```
